```python
import math
import jax, jax.numpy as jnp
from jax import lax
import numpy as np

D_MODEL = 2048
BATCH = 8
SEQ = 2048
DEPTH = 1
DEC_BATCH = 16
DEC_SEQ = 64
PAST_LEN = 4096

CHUNK = 64
ATTN_WIDTH = 1024
SGU_WIDTH = 1024
HEAD_DIM = 64
N_HEADS = ATTN_WIDTH // HEAD_DIM
N_KV_HEADS = 2
GQA_GROUP = N_HEADS // N_KV_HEADS
WINDOW = 128
N_BAND = WINDOW // CHUNK
N_BUCKETS = 32
MAX_DISTANCE = 128
SGU_CHUNK = 128
SGU_GROUPS = 8
SGU_GROUP_CH = SGU_WIDTH // SGU_GROUPS
D_FF = 4 * D_MODEL
KV_WIDTH = N_KV_HEADS * HEAD_DIM
IN_WIDTH = ATTN_WIDTH + 2 * KV_WIDTH + 2 * SGU_WIDTH
EPS = 1e-6
NEG_INF = -1e30

kernel_name = 'hymba_swa_sink_gmlp_stream_step'


def rmsnorm(x, g):
    xf = x.astype(jnp.float32)
    y = xf * lax.rsqrt(jnp.mean(xf * xf, axis=-1, keepdims=True) + EPS)
    return (y * g.astype(jnp.float32)).astype(x.dtype)


def t5_bucket(n):
    half = N_BUCKETS // 2
    max_exact = half // 2
    offset = jnp.where(n < 0, half, 0)
    a = jnp.abs(n)
    af = jnp.maximum(a, 1).astype(jnp.float32)
    large = max_exact + (jnp.log(af / max_exact) / math.log(MAX_DISTANCE / max_exact)
                         * (half - max_exact)).astype(jnp.int32)
    large = jnp.minimum(large, half - 1)
    return offset + jnp.where(a < max_exact, a, large)


def rel_bias(table, q_pos, k_pos):
    n = q_pos[:, None] - k_pos[None, :]
    b = table[t5_bucket(n)].astype(jnp.float32)
    return jnp.transpose(b, (2, 0, 1)).reshape(N_KV_HEADS, GQA_GROUP, q_pos.shape[0], k_pos.shape[0])


def sink_attention(q, k, v, bias, sinks, valid=None):
    s = jnp.einsum('...qhgd,...khd->...hgqk', q, k).astype(jnp.float32) * (HEAD_DIM ** -0.5) + bias
    if valid is not None:
        s = jnp.where(valid, s, NEG_INF)
    sink = sinks.astype(jnp.float32)[..., None]
    m = jnp.maximum(jnp.max(s, axis=-1), sink)
    p = jnp.exp(s - m[..., None])
    w = p / (jnp.sum(p, axis=-1, keepdims=True) + jnp.exp(sink - m)[..., None])
    return jnp.einsum('...hgqk,...khd->...qhgd', w.astype(v.dtype), v)


def project(x, ln_g, w_in, q_g, k_g):
    lead = x.shape[:-1]
    z = rmsnorm(x, ln_g) @ w_in
    q = z[..., :ATTN_WIDTH].reshape(*lead, N_KV_HEADS, GQA_GROUP, HEAD_DIM)
    k = z[..., ATTN_WIDTH:ATTN_WIDTH + KV_WIDTH].reshape(*lead, N_KV_HEADS, HEAD_DIM)
    v = z[..., ATTN_WIDTH + KV_WIDTH:ATTN_WIDTH + 2 * KV_WIDTH].reshape(*lead, N_KV_HEADS, HEAD_DIM)
    zs = z[..., ATTN_WIDTH + 2 * KV_WIDTH:]
    return rmsnorm(q, q_g), rmsnorm(k, k_g), v, zs


def attn_prompt(q, k, v, table, sinks):
    b, s = q.shape[0], q.shape[1]
    n_c = s // CHUNK
    band = (N_BAND + 1) * CHUNK
    qc = q.reshape(b, n_c, CHUNK, N_KV_HEADS, GQA_GROUP, HEAD_DIM)
    pad = ((0, 0), (N_BAND * CHUNK, 0), (0, 0), (0, 0))
    kc = jnp.pad(k, pad).reshape(b, n_c + N_BAND, CHUNK, N_KV_HEADS, HEAD_DIM)
    vc = jnp.pad(v, pad).reshape(b, n_c + N_BAND, CHUNK, N_KV_HEADS, HEAD_DIM)
    kb = jnp.concatenate([kc[:, i:i + n_c] for i in range(N_BAND + 1)], axis=2)
    vb = jnp.concatenate([vc[:, i:i + n_c] for i in range(N_BAND + 1)], axis=2)
    bias = rel_bias(table, N_BAND * CHUNK + jnp.arange(CHUNK), jnp.arange(band))
    key_pos = jnp.arange(n_c)[:, None] * CHUNK + jnp.arange(band)[None, :] - N_BAND * CHUNK
    valid = (key_pos >= 0).reshape(n_c, 1, 1, 1, band)
    o = sink_attention(qc, kb, vb, bias, sinks, valid)
    return o.reshape(b, s, ATTN_WIDTH)


def attn_sample(q, k_all, v_all, table, sinks):
    b, t = q.shape[0], q.shape[1]
    n_k = k_all.shape[1]
    past = n_k - t
    bias = rel_bias(table, past + jnp.arange(t), jnp.arange(n_k))
    o = sink_attention(q, k_all, v_all, bias, sinks)
    return o.reshape(b, t, ATTN_WIDTH)


def sgu_inputs(zs, g):
    a = jax.nn.gelu(zs)
    return a[..., :SGU_WIDTH], rmsnorm(a[..., SGU_WIDTH:], g)


def sgu_mask(n):
    i = jnp.arange(n)
    return (i[None, :] // CHUNK) <= (i[:, None] // CHUNK)


def sgu_prompt(u, v, w_s, b_s):
    b, s = u.shape[0], u.shape[1]
    nb = s // SGU_CHUNK
    vb = v.reshape(b, nb, SGU_CHUNK, SGU_GROUPS, SGU_GROUP_CH)
    ws = jnp.where(sgu_mask(SGU_CHUNK)[None], w_s, 0)
    sp = jnp.einsum('gij,bnjgc->bnigc', ws, vb) + b_s.T[:, :, None]
    return u * sp.reshape(b, s, SGU_WIDTH)


def sgu_sample(u, v, w_s, b_s):
    b, t = u.shape[0], u.shape[1]
    ws = jnp.where(sgu_mask(t)[None], w_s[:, :t, :t], 0)
    sp = jnp.einsum('gij,bjgc->bigc', ws, v.reshape(b, t, SGU_GROUPS, SGU_GROUP_CH)) + b_s[:, :t].T[:, :, None]
    return u * sp.reshape(b, t, SGU_WIDTH)


def finish(x, attn_o, sgu_o, g_a, g_s, w_out, ln_ffn, w_up, w_down):
    mix = jnp.concatenate([rmsnorm(attn_o, g_a), rmsnorm(sgu_o, g_s)], axis=-1)
    x = x + mix @ w_out
    h = rmsnorm(x, ln_ffn)
    return x + jnp.square(jax.nn.relu(h @ w_up)) @ w_down


def setup_inputs(seed: int = 0) -> dict:
    key = jax.random.key(seed)
    ks = jax.random.split(key, 20)
    f32 = jnp.float32
    cache_len = min(WINDOW, PAST_LEN)

    def nrm(k, shape, scale):
        return jax.random.normal(k, shape, f32) * scale

    def gain(k, shape):
        return 1.0 + 0.05 * jax.random.normal(k, shape, f32)

    return {
        'x_prompt': nrm(ks[0], (BATCH, SEQ, D_MODEL), 1.0),
        'x_sample': nrm(ks[1], (DEC_BATCH, DEC_SEQ, D_MODEL), 1.0),
        'cache_attn_k': nrm(ks[2], (DEPTH, DEC_BATCH, cache_len, N_KV_HEADS, HEAD_DIM), 1.0),
        'cache_attn_v': nrm(ks[3], (DEPTH, DEC_BATCH, cache_len, N_KV_HEADS, HEAD_DIM), 1.0),
        'rel_bias_table': nrm(ks[4], (N_BUCKETS, N_HEADS), 0.5),
        'ln_mix_g': gain(ks[5], (DEPTH, D_MODEL)),
        'w_in': nrm(ks[6], (DEPTH, D_MODEL, IN_WIDTH), D_MODEL ** -0.5),
        'q_norm_g': gain(ks[7], (DEPTH, HEAD_DIM)),
        'k_norm_g': gain(ks[8], (DEPTH, HEAD_DIM)),
        'attn_sinks': nrm(ks[9], (DEPTH, N_KV_HEADS, GQA_GROUP), 0.5),
        'sgu_norm_g': gain(ks[10], (DEPTH, SGU_WIDTH)),
        'sgu_w': nrm(ks[11], (DEPTH, SGU_GROUPS, SGU_CHUNK, SGU_CHUNK), SGU_CHUNK ** -0.5),
        'sgu_b': 1.0 + 0.1 * jax.random.normal(ks[12], (DEPTH, SGU_GROUPS, SGU_CHUNK), f32),
        'out_norm_attn_g': gain(ks[13], (DEPTH, ATTN_WIDTH)),
        'out_norm_sgu_g': gain(ks[14], (DEPTH, SGU_WIDTH)),
        'w_out': nrm(ks[15], (DEPTH, ATTN_WIDTH + SGU_WIDTH, D_MODEL), (ATTN_WIDTH + SGU_WIDTH) ** -0.5),
        'ln_ffn_g': gain(ks[16], (DEPTH, D_MODEL)),
        'w_ffn_up': nrm(ks[17], (DEPTH, D_MODEL, D_FF), D_MODEL ** -0.5),
        'w_ffn_down': nrm(ks[18], (DEPTH, D_FF, D_MODEL), D_FF ** -0.5),
    }


def reference(x_prompt, x_sample, cache_attn_k, cache_attn_v, rel_bias_table, ln_mix_g, w_in,
              q_norm_g, k_norm_g, attn_sinks, sgu_norm_g, sgu_w, sgu_b, out_norm_attn_g,
              out_norm_sgu_g, w_out, ln_ffn_g, w_ffn_up, w_ffn_down):
    xp, xs = x_prompt, x_sample
    keep_p = min(WINDOW, xp.shape[1])
    kp_rows, vp_rows, ks_rows, vs_rows, sgu_rows = [], [], [], [], []
    for l in range(DEPTH):
        q, k, v, zs = project(xp, ln_mix_g[l], w_in[l], q_norm_g[l], k_norm_g[l])
        a_o = attn_prompt(q, k, v, rel_bias_table, attn_sinks[l])
        u, sv = sgu_inputs(zs, sgu_norm_g[l])
        s_o = sgu_prompt(u, sv, sgu_w[l], sgu_b[l])
        kp_rows.append(k[:, -keep_p:])
        vp_rows.append(v[:, -keep_p:])
        xp = finish(xp, a_o, s_o, out_norm_attn_g[l], out_norm_sgu_g[l], w_out[l],
                    ln_ffn_g[l], w_ffn_up[l], w_ffn_down[l])
        q, k, v, zs = project(xs, ln_mix_g[l], w_in[l], q_norm_g[l], k_norm_g[l])
        k_all = jnp.concatenate([cache_attn_k[l], k], axis=1)
        v_all = jnp.concatenate([cache_attn_v[l], v], axis=1)
        a_o = attn_sample(q, k_all, v_all, rel_bias_table, attn_sinks[l])
        u, sv = sgu_inputs(zs, sgu_norm_g[l])
        s_o = sgu_sample(u, sv, sgu_w[l], sgu_b[l])
        ks_rows.append(k)
        vs_rows.append(v)
        sgu_rows.append(sv)
        xs = finish(xs, a_o, s_o, out_norm_attn_g[l], out_norm_sgu_g[l], w_out[l],
                    ln_ffn_g[l], w_ffn_up[l], w_ffn_down[l])
    new_attn_k_prompt = jnp.stack(kp_rows)
    new_attn_v_prompt = jnp.stack(vp_rows)
    new_attn_k_sample = jnp.stack(ks_rows)
    new_attn_v_sample = jnp.stack(vs_rows)
    new_sgu_v_sample = jnp.stack(sgu_rows)
    return (xp, xs, new_attn_k_prompt, new_attn_v_prompt, new_attn_k_sample, new_attn_v_sample, new_sgu_v_sample)
```

```python
import functools
import math

import jax
import jax.numpy as jnp
from jax import lax
from jax.experimental import pallas as pl
from jax.experimental.pallas import tpu as pltpu

D_MODEL = 2048
CHUNK = 64
ATTN_WIDTH = 1024
SGU_WIDTH = 1024
HEAD_DIM = 64
N_HEADS = ATTN_WIDTH // HEAD_DIM
N_KV_HEADS = 2
GQA_GROUP = N_HEADS // N_KV_HEADS
WINDOW = 128
N_BUCKETS = 32
MAX_DISTANCE = 128
SGU_CHUNK = 128
SGU_GROUPS = 8
SGU_GROUP_CH = SGU_WIDTH // SGU_GROUPS
D_FF = 4 * D_MODEL
KV_WIDTH = N_KV_HEADS * HEAD_DIM
IN_WIDTH = ATTN_WIDTH + 2 * KV_WIDTH + 2 * SGU_WIDTH
EPS = 1e-6
NEG_INF = -1e30

LANES = 128
VMEM_LIMIT = 56 * 1024 * 1024

BF16 = jnp.bfloat16
F32 = jnp.float32

_KV0 = ATTN_WIDTH
_U0 = ATTN_WIDTH + 2 * KV_WIDTH
_V0 = _U0 + SGU_WIDTH


def _rms_rows(x, gain):
    ms = jnp.mean(x * x, axis=-1, keepdims=True)
    return x * lax.rsqrt(ms + EPS) * gain


def _head_pair_rms(blk, gain2, lane_lo):
    sq = blk * blk
    lo = jnp.sum(jnp.where(lane_lo, sq, 0.0), axis=-1, keepdims=True)
    hi = jnp.sum(jnp.where(lane_lo, 0.0, sq), axis=-1, keepdims=True)
    inv = 1.0 / HEAD_DIM
    r = jnp.where(lane_lo, lax.rsqrt(lo * inv + EPS), lax.rsqrt(hi * inv + EPS))
    return blk * r * gain2


def _proj_kernel(chunk, emit_sv, x_ref, lng_ref, win_ref, qg_ref, kg_ref, sg_ref, ws_ref,
                 bs_ref, gs_ref, q_ref, k_ref, v_ref, mixs_ref, *rest):
    if emit_sv:
        sv_ref, so_scr = rest
    else:
        (so_scr,) = rest
    tm = x_ref.shape[0]
    h = _rms_rows(x_ref[...], lng_ref[...]).astype(BF16)
    lane_lo = lax.broadcasted_iota(jnp.int32, (tm, LANES), 1) < HEAD_DIM

    zq = jnp.dot(h, win_ref[:, 0:ATTN_WIDTH], preferred_element_type=F32)
    for c in range(ATTN_WIDTH // LANES):
        sl = slice(c * LANES, (c + 1) * LANES)
        q_ref[:, sl] = (_head_pair_rms(zq[:, sl], qg_ref[...], lane_lo)
                        * (HEAD_DIM ** -0.5)).astype(BF16)

    zkv = jnp.dot(h, win_ref[:, _KV0:_U0], preferred_element_type=F32)
    k_ref[...] = _head_pair_rms(zkv[:, 0:KV_WIDTH], kg_ref[...], lane_lo)
    v_ref[...] = zkv[:, KV_WIDTH:2 * KV_WIDTH]

    u = jax.nn.gelu(jnp.dot(h, win_ref[:, _U0:_V0], preferred_element_type=F32))
    av = jax.nn.gelu(jnp.dot(h, win_ref[:, _V0:IN_WIDTH], preferred_element_type=F32))
    sv = _rms_rows(av, sg_ref[...])
    if emit_sv:
        sv_ref[...] = sv
    svb = sv.astype(BF16)

    row_c = lax.broadcasted_iota(jnp.int32, (chunk, chunk), 0) // CHUNK
    col_c = lax.broadcasted_iota(jnp.int32, (chunk, chunk), 1) // CHUNK
    n_chunks = tm // chunk
    for g in range(SGU_GROUPS):
        cs = slice(g * SGU_GROUP_CH, (g + 1) * SGU_GROUP_CH)
        w = jnp.where(col_c <= row_c, ws_ref[g], 0.0).astype(BF16)
        rhs = jnp.concatenate(
            [svb[n * chunk:(n + 1) * chunk, cs] for n in range(n_chunks)], axis=1)
        sp = jnp.dot(w, rhs, preferred_element_type=F32)
        for n in range(n_chunks):
            rs = slice(n * chunk, (n + 1) * chunk)
            so_scr[rs, cs] = u[rs, cs] * (sp[:, n * LANES:(n + 1) * LANES] + bs_ref[g])
    mixs_ref[...] = _rms_rows(so_scr[...], gs_ref[...]).astype(BF16)


def _proj(x2d, lng, win, qg2, kg2, sg, ws, bsb, gs, *, tm, chunk, emit_sv):
    t = x2d.shape[0]
    const = lambda shape: pl.BlockSpec(shape, lambda i: (0,) * len(shape),
                                       pipeline_mode=pl.Buffered(1))
    row = lambda w: pl.BlockSpec((tm, w), lambda i: (i, 0))
    out_shape = [jax.ShapeDtypeStruct((t, ATTN_WIDTH), BF16),
                 jax.ShapeDtypeStruct((t, KV_WIDTH), F32),
                 jax.ShapeDtypeStruct((t, KV_WIDTH), F32),
                 jax.ShapeDtypeStruct((t, SGU_WIDTH), BF16)]
    out_specs = [row(ATTN_WIDTH), row(KV_WIDTH), row(KV_WIDTH), row(SGU_WIDTH)]
    if emit_sv:
        out_shape.append(jax.ShapeDtypeStruct((t, SGU_WIDTH), F32))
        out_specs.append(row(SGU_WIDTH))
    return pl.pallas_call(
        functools.partial(_proj_kernel, chunk, emit_sv),
        grid=(t // tm,),
        in_specs=[row(D_MODEL), const((1, D_MODEL)), const((D_MODEL, IN_WIDTH)),
                  const((1, LANES)), const((1, LANES)), const((1, SGU_WIDTH)),
                  const((SGU_GROUPS, chunk, chunk)), const((SGU_GROUPS, chunk, LANES)),
                  const((1, SGU_WIDTH))],
        out_specs=out_specs,
        out_shape=out_shape,
        scratch_shapes=[pltpu.VMEM((tm, SGU_WIDTH), F32)],
        compiler_params=pltpu.CompilerParams(dimension_semantics=("parallel",),
                                             vmem_limit_bytes=VMEM_LIMIT),
        name="proj_c%d" % chunk,
    )(x2d, lng, win, qg2, kg2, sg, ws, bsb, gs)


def _bias_kernel(tab_ref, bkt_ref, valid_ref, out_ref):
    bkt = bkt_ref[...]
    for h in range(N_HEADS):
        acc = jnp.zeros(bkt.shape, F32)
        for b in range(N_BUCKETS):
            acc = jnp.where(bkt == b, tab_ref[b, h], acc)
        for s in range(valid_ref.shape[0]):
            out_ref[s, h] = jnp.where(valid_ref[s] != 0, acc, NEG_INF)


def _bias(table, bucket, valid):
    nsel, bq, bk = valid.shape
    return pl.pallas_call(
        _bias_kernel,
        in_specs=[pl.BlockSpec(memory_space=pltpu.SMEM),
                  pl.BlockSpec(memory_space=pltpu.VMEM),
                  pl.BlockSpec(memory_space=pltpu.VMEM)],
        out_specs=pl.BlockSpec(memory_space=pltpu.VMEM),
        out_shape=jax.ShapeDtypeStruct((nsel, N_HEADS, bq, bk), F32),
        name="bias_q%d" % bq,
    )(table, bucket, valid)


def _t5_bucket(n):
    half = N_BUCKETS // 2
    max_exact = half // 2
    offset = jnp.where(n < 0, half, 0)
    a = jnp.abs(n)
    af = jnp.maximum(a, 1).astype(F32)
    large = max_exact + (jnp.log(af / max_exact) / math.log(MAX_DISTANCE / max_exact)
                         * (half - max_exact)).astype(jnp.int32)
    large = jnp.minimum(large, half - 1)
    return offset + jnp.where(a < max_exact, a, large)


def _attn_kernel(nsel, sink_ref, q_ref, kp_ref, kc_ref, vp_ref, vc_ref, bias_ref, ga_ref,
                 o_ref, ao_scr):
    sel = jnp.minimum(pl.program_id(1), nsel - 1) if nsel > 1 else 0
    kb = jnp.concatenate([kp_ref[...], kc_ref[...]], axis=0).astype(BF16)
    vb = jnp.concatenate([vp_ref[...], vc_ref[...]], axis=0).astype(BF16)
    for h in range(N_HEADS):
        kv = slice((h // GQA_GROUP) * HEAD_DIM, (h // GQA_GROUP + 1) * HEAD_DIM)
        hs = slice(h * HEAD_DIM, (h + 1) * HEAD_DIM)
        s = lax.dot_general(q_ref[:, hs], kb[:, kv], (((1,), (1,)), ((), ())),
                            preferred_element_type=F32) + bias_ref[sel, h]
        sink = sink_ref[h]
        m = jnp.maximum(jnp.max(s, axis=-1, keepdims=True), sink)
        p = jnp.exp(s - m)
        denom = jnp.sum(p, axis=-1, keepdims=True) + jnp.exp(sink - m)
        o = jnp.dot(p.astype(BF16), vb[:, kv], preferred_element_type=F32)
        ao_scr[:, hs] = o / denom
    o_ref[...] = _rms_rows(ao_scr[...], ga_ref[...]).astype(BF16)


def _attn(sinks, q, kprev, kcur, vprev, vcur, bias, ga, *, nblk, prev_map, cur_map):
    nb, _, _ = q.shape
    nsel, _, bq, bk = bias.shape
    lp, lc = bk - bq, bq
    grid = (nb, nblk)
    return pl.pallas_call(
        functools.partial(_attn_kernel, nsel),
        grid=grid,
        in_specs=[pl.BlockSpec(memory_space=pltpu.SMEM),
                  pl.BlockSpec((None, bq, ATTN_WIDTH), cur_map),
                  pl.BlockSpec((None, lp, KV_WIDTH), prev_map),
                  pl.BlockSpec((None, lc, KV_WIDTH), cur_map),
                  pl.BlockSpec((None, lp, KV_WIDTH), prev_map),
                  pl.BlockSpec((None, lc, KV_WIDTH), cur_map),
                  pl.BlockSpec(bias.shape, lambda b, i: (0, 0, 0, 0)),
                  pl.BlockSpec((1, ATTN_WIDTH), lambda b, i: (0, 0))],
        out_specs=pl.BlockSpec((None, bq, ATTN_WIDTH), cur_map),
        out_shape=jax.ShapeDtypeStruct(q.shape, BF16),
        scratch_shapes=[pltpu.VMEM((bq, ATTN_WIDTH), F32)],
        compiler_params=pltpu.CompilerParams(dimension_semantics=("parallel", "arbitrary"),
                                             vmem_limit_bytes=VMEM_LIMIT),
        name="attn_q%d" % bq,
    )(sinks, q, kprev, kcur, vprev, vcur, bias, ga)


def _outproj_kernel(x_ref, ma_ref, ms_ref, wo_ref, o_ref):
    acc = jnp.dot(ma_ref[...], wo_ref[0:ATTN_WIDTH, :], preferred_element_type=F32)
    acc = acc + jnp.dot(ms_ref[...], wo_ref[ATTN_WIDTH:, :], preferred_element_type=F32)
    o_ref[...] = x_ref[...] + acc


def _outproj(x2d, mix_a, mix_s, wo, *, tm):
    t = x2d.shape[0]
    row = lambda w: pl.BlockSpec((tm, w), lambda i: (i, 0))
    return pl.pallas_call(
        _outproj_kernel,
        grid=(t // tm,),
        in_specs=[row(D_MODEL), row(ATTN_WIDTH), row(SGU_WIDTH),
                  pl.BlockSpec(wo.shape, lambda i: (0, 0), pipeline_mode=pl.Buffered(1))],
        out_specs=row(D_MODEL),
        out_shape=jax.ShapeDtypeStruct((t, D_MODEL), F32),
        compiler_params=pltpu.CompilerParams(dimension_semantics=("parallel",),
                                             vmem_limit_bytes=VMEM_LIMIT),
        name="outproj",
    )(x2d, mix_a, mix_s, wo)


def _ffn_kernel(x_ref, g_ref, wup_ref, wdn_ref, y_ref, h_scr):
    @pl.when(pl.program_id(1) == 0)
    def _():
        x = x_ref[...]
        h_scr[...] = _rms_rows(x, g_ref[...]).astype(BF16)
        y_ref[...] = x

    a = jnp.dot(h_scr[...], wup_ref[...], preferred_element_type=F32)
    a = jnp.square(jnp.maximum(a, 0.0)).astype(BF16)
    y_ref[...] += jnp.dot(a, wdn_ref[...], preferred_element_type=F32)


def _ffn(x2d, g, wup, wdn, *, tm, tf):
    t = x2d.shape[0]
    return pl.pallas_call(
        _ffn_kernel,
        grid=(t // tm, D_FF // tf),
        in_specs=[pl.BlockSpec((tm, D_MODEL), lambda i, j: (i, 0)),
                  pl.BlockSpec((1, D_MODEL), lambda i, j: (0, 0)),
                  pl.BlockSpec((D_MODEL, tf), lambda i, j: (0, j)),
                  pl.BlockSpec((tf, D_MODEL), lambda i, j: (j, 0))],
        out_specs=pl.BlockSpec((tm, D_MODEL), lambda i, j: (i, 0)),
        out_shape=jax.ShapeDtypeStruct((t, D_MODEL), F32),
        scratch_shapes=[pltpu.VMEM((tm, D_MODEL), BF16)],
        compiler_params=pltpu.CompilerParams(dimension_semantics=("parallel", "arbitrary"),
                                             vmem_limit_bytes=VMEM_LIMIT),
        name="ffn",
    )(x2d, g, wup, wdn)


def _band_tables(bq, bk, nsel):
    qi = jnp.arange(bq)[:, None]
    kj = jnp.arange(bk)[None, :] - (bk - bq)
    n = qi - kj
    qc, kc = qi // CHUNK, jnp.floor_divide(kj, CHUNK)
    in_band = (kc <= qc) & (kc >= qc - WINDOW // CHUNK)
    valid = [in_band & (kj >= 0)] if nsel == 2 else []
    valid.append(in_band)
    return _t5_bucket(n), jnp.stack(valid).astype(jnp.int32)


def kernel(x_prompt, x_sample, cache_attn_k, cache_attn_v, rel_bias_table, ln_mix_g, w_in,
           q_norm_g, k_norm_g, attn_sinks, sgu_norm_g, sgu_w, sgu_b, out_norm_attn_g,
           out_norm_sgu_g, w_out, ln_ffn_g, w_ffn_up, w_ffn_down):
    bp, sp, _ = x_prompt.shape
    bs, ss, _ = x_sample.shape
    depth = w_in.shape[0]
    assert depth == 1 and ss == CHUNK and cache_attn_k.shape[2] == WINDOW
    l = 0
    xp = x_prompt.reshape(bp * sp, D_MODEL)
    xs = x_sample.reshape(bs * ss, D_MODEL)

    win = w_in[l].astype(BF16)
    wo = w_out[l].astype(BF16)
    wup = w_ffn_up[l].astype(BF16)
    wdn = w_ffn_down[l].astype(BF16)
    lng = ln_mix_g[l].reshape(1, D_MODEL)
    qg2 = jnp.tile(q_norm_g[l], 2).reshape(1, LANES)
    kg2 = jnp.tile(k_norm_g[l], 2).reshape(1, LANES)
    sg = sgu_norm_g[l].reshape(1, SGU_WIDTH)
    gs = out_norm_sgu_g[l].reshape(1, SGU_WIDTH)
    ga = out_norm_attn_g[l].reshape(1, ATTN_WIDTH)
    lnf = ln_ffn_g[l].reshape(1, D_MODEL)
    sinks = attn_sinks[l].reshape(N_HEADS)

    def sgu_params(c):
        return (sgu_w[l][:, :c, :c],
                jnp.broadcast_to(sgu_b[l][:, :c, None], (SGU_GROUPS, c, LANES)))

    ws_p, bs_p = sgu_params(SGU_CHUNK)
    q, k, v, mix_s = _proj(xp, lng, win, qg2, kg2, sg, ws_p, bs_p, gs,
                           tm=512, chunk=SGU_CHUNK, emit_sv=False)
    bq = 2 * CHUNK
    bucket, valid = _band_tables(bq, bq + WINDOW, 2)
    bias_p = _bias(rel_bias_table, bucket, valid)
    k3 = k.reshape(bp, sp, KV_WIDTH)
    v3 = v.reshape(bp, sp, KV_WIDTH)
    mix_a = _attn(sinks, q.reshape(bp, sp, ATTN_WIDTH), k3, k3, v3, v3, bias_p, ga,
                  nblk=sp // bq,
                  prev_map=lambda b, i: (b, jnp.maximum(i - 1, 0), 0),
                  cur_map=lambda b, i: (b, i, 0))
    x1p = _outproj(xp, mix_a.reshape(bp * sp, ATTN_WIDTH), mix_s, wo, tm=512)
    yp = _ffn(x1p, lnf, wup, wdn, tm=1024, tf=512)

    ws_s, bs_s = sgu_params(ss)
    qs, ks, vs, mix_ss, svs = _proj(xs, lng, win, qg2, kg2, sg, ws_s, bs_s, gs,
                                    tm=512, chunk=ss, emit_sv=True)
    bucket_s, valid_s = _band_tables(ss, ss + WINDOW, 1)
    bias_s = _bias(rel_bias_table, bucket_s, valid_s)
    ck = cache_attn_k[l].reshape(bs, WINDOW, KV_WIDTH)
    cv = cache_attn_v[l].reshape(bs, WINDOW, KV_WIDTH)
    whole = lambda b, i: (b, 0, 0)
    mix_as = _attn(sinks, qs.reshape(bs, ss, ATTN_WIDTH), ck, ks.reshape(bs, ss, KV_WIDTH),
                   cv, vs.reshape(bs, ss, KV_WIDTH), bias_s, ga,
                   nblk=1, prev_map=whole, cur_map=whole)
    x1s = _outproj(xs, mix_as.reshape(bs * ss, ATTN_WIDTH), mix_ss, wo, tm=512)
    ys = _ffn(x1s, lnf, wup, wdn, tm=1024, tf=512)

    keep = min(WINDOW, sp)
    kv_shape = (N_KV_HEADS, HEAD_DIM)
    return (yp.reshape(bp, sp, D_MODEL),
            ys.reshape(bs, ss, D_MODEL),
            k3[:, -keep:].reshape(1, bp, keep, *kv_shape),
            v3[:, -keep:].reshape(1, bp, keep, *kv_shape),
            ks.reshape(1, bs, ss, *kv_shape),
            vs.reshape(1, bs, ss, *kv_shape),
            svs.reshape(1, bs, ss, SGU_WIDTH))
```

```python
import functools
import math

import jax
import jax.numpy as jnp
from jax import lax
from jax.experimental import pallas as pl
from jax.experimental.pallas import tpu as pltpu

D_MODEL = 2048
CHUNK = 64
ATTN_WIDTH = 1024
SGU_WIDTH = 1024
HEAD_DIM = 64
N_HEADS = ATTN_WIDTH // HEAD_DIM
N_KV_HEADS = 2
GQA_GROUP = N_HEADS // N_KV_HEADS
WINDOW = 128
N_BUCKETS = 32
MAX_DISTANCE = 128
SGU_CHUNK = 128
SGU_GROUPS = 8
SGU_GROUP_CH = SGU_WIDTH // SGU_GROUPS
D_FF = 4 * D_MODEL
KV_WIDTH = N_KV_HEADS * HEAD_DIM
IN_WIDTH = ATTN_WIDTH + 2 * KV_WIDTH + 2 * SGU_WIDTH
EPS = 1e-6
NEG_INF = -1e30

LANES = 128
VMEM_LIMIT = 56 * 1024 * 1024

BF16 = jnp.bfloat16
F32 = jnp.float32

_KV0 = ATTN_WIDTH
_U0 = ATTN_WIDTH + 2 * KV_WIDTH
_V0 = _U0 + SGU_WIDTH


def _rms_rows(x, gain):
    ms = jnp.mean(x * x, axis=-1, keepdims=True)
    return x * lax.rsqrt(ms + EPS) * gain


def _head_pair_rms(blk, gain2, lane_lo):
    sq = blk * blk
    lo = jnp.sum(jnp.where(lane_lo, sq, 0.0), axis=-1, keepdims=True)
    hi = jnp.sum(jnp.where(lane_lo, 0.0, sq), axis=-1, keepdims=True)
    inv = 1.0 / HEAD_DIM
    r = jnp.where(lane_lo, lax.rsqrt(lo * inv + EPS), lax.rsqrt(hi * inv + EPS))
    return blk * r * gain2


def _proj_kernel(chunk, emit_sv, x_ref, lng_ref, win_ref, qg_ref, kg_ref, sg_ref, ws_ref,
                 bs_ref, gs_ref, q_ref, k_ref, v_ref, mixs_ref, *rest):
    if emit_sv:
        sv_ref, so_scr = rest
    else:
        (so_scr,) = rest
    tm = x_ref.shape[0]
    h = _rms_rows(x_ref[...], lng_ref[...]).astype(BF16)
    lane_lo = lax.broadcasted_iota(jnp.int32, (tm, LANES), 1) < HEAD_DIM

    zq = jnp.dot(h, win_ref[:, 0:ATTN_WIDTH], preferred_element_type=F32)
    for c in range(ATTN_WIDTH // LANES):
        sl = slice(c * LANES, (c + 1) * LANES)
        q_ref[:, sl] = (_head_pair_rms(zq[:, sl], qg_ref[...], lane_lo)
                        * (HEAD_DIM ** -0.5)).astype(BF16)

    zkv = jnp.dot(h, win_ref[:, _KV0:_U0], preferred_element_type=F32)
    k_ref[...] = _head_pair_rms(zkv[:, 0:KV_WIDTH], kg_ref[...], lane_lo)
    v_ref[...] = zkv[:, KV_WIDTH:2 * KV_WIDTH]

    u = jax.nn.gelu(jnp.dot(h, win_ref[:, _U0:_V0], preferred_element_type=F32))
    av = jax.nn.gelu(jnp.dot(h, win_ref[:, _V0:IN_WIDTH], preferred_element_type=F32))
    sv = _rms_rows(av, sg_ref[...])
    if emit_sv:
        sv_ref[...] = sv
    svb = sv.astype(BF16)

    row_c = lax.broadcasted_iota(jnp.int32, (chunk, chunk), 0) // CHUNK
    col_c = lax.broadcasted_iota(jnp.int32, (chunk, chunk), 1) // CHUNK
    n_chunks = tm // chunk
    for g in range(SGU_GROUPS):
        cs = slice(g * SGU_GROUP_CH, (g + 1) * SGU_GROUP_CH)
        w = jnp.where(col_c <= row_c, ws_ref[g], 0.0).astype(BF16)
        rhs = jnp.concatenate(
            [svb[n * chunk:(n + 1) * chunk, cs] for n in range(n_chunks)], axis=1)
        sp = jnp.dot(w, rhs, preferred_element_type=F32)
        for n in range(n_chunks):
            rs = slice(n * chunk, (n + 1) * chunk)
            so_scr[rs, cs] = u[rs, cs] * (sp[:, n * LANES:(n + 1) * LANES] + bs_ref[g])
    mixs_ref[...] = _rms_rows(so_scr[...], gs_ref[...]).astype(BF16)


def _proj(x2d, lng, win, qg2, kg2, sg, ws, bsb, gs, *, tm, chunk, emit_sv):
    t = x2d.shape[0]
    const = lambda shape: pl.BlockSpec(shape, lambda i: (0,) * len(shape),
                                       pipeline_mode=pl.Buffered(1))
    row = lambda w: pl.BlockSpec((tm, w), lambda i: (i, 0))
    out_shape = [jax.ShapeDtypeStruct((t, ATTN_WIDTH), BF16),
                 jax.ShapeDtypeStruct((t, KV_WIDTH), F32),
                 jax.ShapeDtypeStruct((t, KV_WIDTH), F32),
                 jax.ShapeDtypeStruct((t, SGU_WIDTH), BF16)]
    out_specs = [row(ATTN_WIDTH), row(KV_WIDTH), row(KV_WIDTH), row(SGU_WIDTH)]
    if emit_sv:
        out_shape.append(jax.ShapeDtypeStruct((t, SGU_WIDTH), F32))
        out_specs.append(row(SGU_WIDTH))
    return pl.pallas_call(
        functools.partial(_proj_kernel, chunk, emit_sv),
        grid=(t // tm,),
        in_specs=[row(D_MODEL), const((1, D_MODEL)), const((D_MODEL, IN_WIDTH)),
                  const((1, LANES)), const((1, LANES)), const((1, SGU_WIDTH)),
                  const((SGU_GROUPS, chunk, chunk)), const((SGU_GROUPS, chunk, LANES)),
                  const((1, SGU_WIDTH))],
        out_specs=out_specs,
        out_shape=out_shape,
        scratch_shapes=[pltpu.VMEM((tm, SGU_WIDTH), F32)],
        compiler_params=pltpu.CompilerParams(dimension_semantics=("parallel",),
                                             vmem_limit_bytes=VMEM_LIMIT),
        name="proj_c%d" % chunk,
    )(x2d, lng, win, qg2, kg2, sg, ws, bsb, gs)


BAND = 2 * LANES
PAIRS = GQA_GROUP // 2


def _bias_kernel(bq, tab_ref, bkt_ref, valid_ref, out_ref):
    bkt = bkt_ref[...]
    for h in range(N_HEADS):
        g, pr, half = h // GQA_GROUP, (h % GQA_GROUP) // 2, h % 2
        acc = jnp.zeros(bkt.shape, F32)
        for b in range(N_BUCKETS):
            acc = jnp.where(bkt == b, tab_ref[b, h], acc)
        for s in range(valid_ref.shape[0]):
            out_ref[s, g, pr * bq:(pr + 1) * bq, half * BAND:(half + 1) * BAND] = (
                jnp.where(valid_ref[s] != 0, acc, NEG_INF))


def _bias(table, bucket, valid):
    nsel, bq, _ = valid.shape
    return pl.pallas_call(
        functools.partial(_bias_kernel, bq),
        in_specs=[pl.BlockSpec(memory_space=pltpu.SMEM),
                  pl.BlockSpec(memory_space=pltpu.VMEM),
                  pl.BlockSpec(memory_space=pltpu.VMEM)],
        out_specs=pl.BlockSpec(memory_space=pltpu.VMEM),
        out_shape=jax.ShapeDtypeStruct((nsel, N_KV_HEADS, PAIRS * bq, 2 * BAND), F32),
        name="bias_q%d" % bq,
    )(table, bucket, valid)


def _t5_bucket(n):
    half = N_BUCKETS // 2
    max_exact = half // 2
    offset = jnp.where(n < 0, half, 0)
    a = jnp.abs(n)
    af = jnp.maximum(a, 1).astype(F32)
    large = max_exact + (jnp.log(af / max_exact) / math.log(MAX_DISTANCE / max_exact)
                         * (half - max_exact)).astype(jnp.int32)
    large = jnp.minimum(large, half - 1)
    return offset + jnp.where(a < max_exact, a, large)


def _attn_kernel(bq, n_units, k_step, nsel, n_kparts, *refs):
    q_ref = refs[0]
    k_refs = refs[1:1 + n_kparts]
    v_refs = refs[1 + n_kparts:1 + 2 * n_kparts]
    bias_ref, smax_ref, sl_ref, ga_ref, o_ref, kk_scr, vv_scr, ao_scr = refs[1 + 2 * n_kparts:]
    rows = PAIRS * bq

    def split_heads(parts, scr):
        x = jnp.concatenate([r[...] for r in parts], axis=0) if len(parts) > 1 else parts[0][...]
        lo = lax.broadcasted_iota(jnp.int32, x.shape, 1) < HEAD_DIM
        xr = pltpu.roll(x, HEAD_DIM, 1)
        scr[0, 0] = jnp.where(lo, x, 0.0).astype(BF16)
        scr[0, 1] = jnp.where(lo, 0.0, xr).astype(BF16)
        scr[1, 0] = jnp.where(lo, xr, 0.0).astype(BF16)
        scr[1, 1] = jnp.where(lo, 0.0, x).astype(BF16)

    split_heads(k_refs, kk_scr)
    split_heads(v_refs, vv_scr)
    r_i = lax.broadcasted_iota(jnp.int32, (2 * BAND, LANES), 0)
    l_i = lax.broadcasted_iota(jnp.int32, (2 * BAND, LANES), 1)
    ones_blk = jnp.where((r_i < BAND) == (l_i < HEAD_DIM), 1.0, 0.0).astype(BF16)
    lane_lo = lax.broadcasted_iota(jnp.int32, (rows, LANES), 1) < HEAD_DIM
    first_block = pl.program_id(1) == 0

    def unit(u, carry):
        q0 = pl.multiple_of(u * bq, bq)
        k0 = pl.multiple_of(u * k_step, k_step)
        sel = jnp.where(first_block & (u == 0), 0, 1) if nsel > 1 else 0
        for g in range(N_KV_HEADS):
            cols = [slice((PAIRS * g + p) * LANES, (PAIRS * g + p + 1) * LANES)
                    for p in range(PAIRS)]
            qs = jnp.concatenate([q_ref[pl.ds(q0, bq), c] for c in cols], axis=0)
            k2 = jnp.concatenate([kk_scr[g, 0, pl.ds(k0, BAND), :],
                                  kk_scr[g, 1, pl.ds(k0, BAND), :]], axis=0)
            s = lax.dot_general(qs, k2, (((1,), (1,)), ((), ())),
                                preferred_element_type=F32) + bias_ref[sel, g]
            sa, sb = s[:, :BAND], s[:, BAND:]
            ma = jnp.max(jnp.maximum(jnp.maximum(sa[:, :LANES], sa[:, LANES:]), smax_ref[g, 0]),
                         axis=-1, keepdims=True)
            mb = jnp.max(jnp.maximum(jnp.maximum(sb[:, :LANES], sb[:, LANES:]), smax_ref[g, 1]),
                         axis=-1, keepdims=True)
            p = jnp.concatenate([jnp.exp(sa - ma), jnp.exp(sb - mb)], axis=1).astype(BF16)
            v2 = jnp.concatenate([vv_scr[g, 0, pl.ds(k0, BAND), :],
                                  vv_scr[g, 1, pl.ds(k0, BAND), :]], axis=0)
            ol = jnp.dot(p, jnp.concatenate([v2, ones_blk], axis=1), preferred_element_type=F32)
            denom = ol[:, LANES:] + jnp.exp(sl_ref[g] - jnp.where(lane_lo, ma, mb))
            out = ol[:, :LANES] / denom
            for p_i, c in enumerate(cols):
                ao_scr[pl.ds(q0, bq), c] = out[p_i * bq:(p_i + 1) * bq]
        return carry

    lax.fori_loop(0, n_units, unit, 0)
    o_ref[...] = _rms_rows(ao_scr[...], ga_ref[...]).astype(BF16)


def _attn(q, k_parts, v_parts, k_maps, bias, smax, sl, ga, *, bq, n_units, k_step):
    nb, s, _ = q.shape
    nsel = bias.shape[0]
    tq = n_units * bq
    k_rows = [rows for _, rows in k_parts]
    kr = sum(k_rows)
    cur = lambda b, i: (b, i, 0)
    full = lambda a: pl.BlockSpec(a.shape, lambda b, i: (0,) * a.ndim,
                                  pipeline_mode=pl.Buffered(1))
    kv_specs = [pl.BlockSpec((None, r, KV_WIDTH), m) for r, m in zip(k_rows, k_maps)]
    return pl.pallas_call(
        functools.partial(_attn_kernel, bq, n_units, k_step, nsel, len(k_parts)),
        grid=(nb, s // tq),
        in_specs=([pl.BlockSpec((None, tq, ATTN_WIDTH), cur)] + kv_specs + kv_specs
                  + [full(bias), full(smax), full(sl), full(ga)]),
        out_specs=pl.BlockSpec((None, tq, ATTN_WIDTH), cur),
        out_shape=jax.ShapeDtypeStruct(q.shape, BF16),
        scratch_shapes=[pltpu.VMEM((N_KV_HEADS, 2, kr, KV_WIDTH), BF16),
                        pltpu.VMEM((N_KV_HEADS, 2, kr, KV_WIDTH), BF16),
                        pltpu.VMEM((tq, ATTN_WIDTH), F32)],
        compiler_params=pltpu.CompilerParams(dimension_semantics=("parallel", "arbitrary"),
                                             vmem_limit_bytes=VMEM_LIMIT),
        name="attn_q%d" % bq,
    )(q, *[a for a, _ in k_parts], *[a for a, _ in v_parts], bias, smax, sl, ga)


def _outproj_kernel(x_ref, ma_ref, ms_ref, wo_ref, o_ref):
    acc = jnp.dot(ma_ref[...], wo_ref[0:ATTN_WIDTH, :], preferred_element_type=F32)
    acc = acc + jnp.dot(ms_ref[...], wo_ref[ATTN_WIDTH:, :], preferred_element_type=F32)
    o_ref[...] = x_ref[...] + acc


def _outproj(x2d, mix_a, mix_s, wo, *, tm):
    t = x2d.shape[0]
    row = lambda w: pl.BlockSpec((tm, w), lambda i: (i, 0))
    return pl.pallas_call(
        _outproj_kernel,
        grid=(t // tm,),
        in_specs=[row(D_MODEL), row(ATTN_WIDTH), row(SGU_WIDTH),
                  pl.BlockSpec(wo.shape, lambda i: (0, 0), pipeline_mode=pl.Buffered(1))],
        out_specs=row(D_MODEL),
        out_shape=jax.ShapeDtypeStruct((t, D_MODEL), F32),
        compiler_params=pltpu.CompilerParams(dimension_semantics=("parallel",),
                                             vmem_limit_bytes=VMEM_LIMIT),
        name="outproj",
    )(x2d, mix_a, mix_s, wo)


def _ffn_kernel(x_ref, g_ref, wup_ref, wdn_ref, y_ref, h_scr):
    @pl.when(pl.program_id(1) == 0)
    def _():
        x = x_ref[...]
        h_scr[...] = _rms_rows(x, g_ref[...]).astype(BF16)
        y_ref[...] = x

    a = jnp.dot(h_scr[...], wup_ref[...], preferred_element_type=F32)
    a = jnp.square(jnp.maximum(a, 0.0)).astype(BF16)
    y_ref[...] += jnp.dot(a, wdn_ref[...], preferred_element_type=F32)


def _ffn(x2d, g, wup, wdn, *, tm, tf):
    t = x2d.shape[0]
    return pl.pallas_call(
        _ffn_kernel,
        grid=(t // tm, D_FF // tf),
        in_specs=[pl.BlockSpec((tm, D_MODEL), lambda i, j: (i, 0)),
                  pl.BlockSpec((1, D_MODEL), lambda i, j: (0, 0)),
                  pl.BlockSpec((D_MODEL, tf), lambda i, j: (0, j)),
                  pl.BlockSpec((tf, D_MODEL), lambda i, j: (j, 0))],
        out_specs=pl.BlockSpec((tm, D_MODEL), lambda i, j: (i, 0)),
        out_shape=jax.ShapeDtypeStruct((t, D_MODEL), F32),
        scratch_shapes=[pltpu.VMEM((tm, D_MODEL), BF16)],
        compiler_params=pltpu.CompilerParams(dimension_semantics=("parallel", "arbitrary"),
                                             vmem_limit_bytes=VMEM_LIMIT),
        name="ffn",
    )(x2d, g, wup, wdn)


def _band_tables(bq, nsel):
    qi = jnp.arange(bq)[:, None]
    kj = jnp.arange(BAND)[None, :] - WINDOW
    qc, kc = qi // CHUNK, jnp.floor_divide(kj, CHUNK)
    in_band = (kc <= qc) & (kc >= qc - WINDOW // CHUNK)
    valid = [in_band & (kj >= 0)] if nsel == 2 else []
    valid.append(in_band)
    return _t5_bucket(qi - kj), jnp.stack(valid).astype(jnp.int32)


def _sink_tables(sinks, bq):
    s3 = sinks.reshape(N_KV_HEADS, PAIRS, 2)
    rows = PAIRS * bq
    smax = jnp.broadcast_to(jnp.transpose(s3, (0, 2, 1))[:, :, :, None, None],
                            (N_KV_HEADS, 2, PAIRS, bq, LANES)).reshape(N_KV_HEADS, 2, rows, LANES)
    sl = jnp.broadcast_to(jnp.repeat(s3, HEAD_DIM, axis=-1)[:, :, None, :],
                          (N_KV_HEADS, PAIRS, bq, LANES)).reshape(N_KV_HEADS, rows, LANES)
    return smax, sl


def kernel(x_prompt, x_sample, cache_attn_k, cache_attn_v, rel_bias_table, ln_mix_g, w_in,
           q_norm_g, k_norm_g, attn_sinks, sgu_norm_g, sgu_w, sgu_b, out_norm_attn_g,
           out_norm_sgu_g, w_out, ln_ffn_g, w_ffn_up, w_ffn_down):
    bp, sp, _ = x_prompt.shape
    bs, ss, _ = x_sample.shape
    depth = w_in.shape[0]
    assert depth == 1 and ss == CHUNK and cache_attn_k.shape[2] == WINDOW
    l = 0
    xp = x_prompt.reshape(bp * sp, D_MODEL)
    xs = x_sample.reshape(bs * ss, D_MODEL)

    win = w_in[l].astype(BF16)
    wo = w_out[l].astype(BF16)
    wup = w_ffn_up[l].astype(BF16)
    wdn = w_ffn_down[l].astype(BF16)
    lng = ln_mix_g[l].reshape(1, D_MODEL)
    qg2 = jnp.tile(q_norm_g[l], 2).reshape(1, LANES)
    kg2 = jnp.tile(k_norm_g[l], 2).reshape(1, LANES)
    sg = sgu_norm_g[l].reshape(1, SGU_WIDTH)
    gs = out_norm_sgu_g[l].reshape(1, SGU_WIDTH)
    ga = out_norm_attn_g[l].reshape(1, ATTN_WIDTH)
    lnf = ln_ffn_g[l].reshape(1, D_MODEL)
    sinks = attn_sinks[l].reshape(N_HEADS)

    def sgu_params(c):
        return (sgu_w[l][:, :c, :c],
                jnp.broadcast_to(sgu_b[l][:, :c, None], (SGU_GROUPS, c, LANES)))

    ws_p, bs_p = sgu_params(SGU_CHUNK)
    q, k, v, mix_s = _proj(xp, lng, win, qg2, kg2, sg, ws_p, bs_p, gs,
                           tm=512, chunk=SGU_CHUNK, emit_sv=False)
    bq, units = 2 * CHUNK, 4
    bias_p = _bias(rel_bias_table, *_band_tables(bq, 2))
    k3 = k.reshape(bp, sp, KV_WIDTH)
    v3 = v.reshape(bp, sp, KV_WIDTH)
    prev = lambda b, i: (b, jnp.maximum(i * units - 1, 0), 0)
    cur = lambda b, i: (b, i, 0)
    mix_a = _attn(q.reshape(bp, sp, ATTN_WIDTH),
                  [(k3, WINDOW), (k3, units * bq)], [(v3, WINDOW), (v3, units * bq)],
                  [prev, cur], bias_p, *_sink_tables(sinks, bq), ga,
                  bq=bq, n_units=units, k_step=bq)
    x1p = _outproj(xp, mix_a.reshape(bp * sp, ATTN_WIDTH), mix_s, wo, tm=512)
    yp = _ffn(x1p, lnf, wup, wdn, tm=1024, tf=512)

    ws_s, bs_s = sgu_params(ss)
    qs, ks, vs, mix_ss, svs = _proj(xs, lng, win, qg2, kg2, sg, ws_s, bs_s, gs,
                                    tm=512, chunk=ss, emit_sv=True)
    bias_s = _bias(rel_bias_table, *_band_tables(ss, 1))
    seqs = 4
    pad = jnp.zeros((bs, BAND - WINDOW - ss, KV_WIDTH), F32)
    k_all = jnp.concatenate([cache_attn_k[l].reshape(bs, WINDOW, KV_WIDTH),
                             ks.reshape(bs, ss, KV_WIDTH), pad], axis=1)
    v_all = jnp.concatenate([cache_attn_v[l].reshape(bs, WINDOW, KV_WIDTH),
                             vs.reshape(bs, ss, KV_WIDTH), pad], axis=1)
    mix_as = _attn(qs.reshape(bs // seqs, seqs * ss, ATTN_WIDTH),
                   [(k_all.reshape(bs // seqs, seqs * BAND, KV_WIDTH), seqs * BAND)],
                   [(v_all.reshape(bs // seqs, seqs * BAND, KV_WIDTH), seqs * BAND)],
                   [cur], bias_s, *_sink_tables(sinks, ss), ga,
                   bq=ss, n_units=seqs, k_step=BAND)
    x1s = _outproj(xs, mix_as.reshape(bs * ss, ATTN_WIDTH), mix_ss, wo, tm=512)
    ys = _ffn(x1s, lnf, wup, wdn, tm=1024, tf=512)

    keep = min(WINDOW, sp)
    kv_shape = (N_KV_HEADS, HEAD_DIM)
    return (yp.reshape(bp, sp, D_MODEL),
            ys.reshape(bs, ss, D_MODEL),
            k3[:, -keep:].reshape(1, bp, keep, *kv_shape),
            v3[:, -keep:].reshape(1, bp, keep, *kv_shape),
            ks.reshape(1, bs, ss, *kv_shape),
            vs.reshape(1, bs, ss, *kv_shape),
            svs.reshape(1, bs, ss, SGU_WIDTH))
```

```python
import functools
import math

import jax
import jax.numpy as jnp
from jax import lax
from jax.experimental import pallas as pl
from jax.experimental.pallas import tpu as pltpu

D_MODEL = 2048
CHUNK = 64
ATTN_WIDTH = 1024
SGU_WIDTH = 1024
HEAD_DIM = 64
N_HEADS = ATTN_WIDTH // HEAD_DIM
N_KV_HEADS = 2
GQA_GROUP = N_HEADS // N_KV_HEADS
WINDOW = 128
N_BUCKETS = 32
MAX_DISTANCE = 128
SGU_CHUNK = 128
SGU_GROUPS = 8
SGU_GROUP_CH = SGU_WIDTH // SGU_GROUPS
D_FF = 4 * D_MODEL
KV_WIDTH = N_KV_HEADS * HEAD_DIM
IN_WIDTH = ATTN_WIDTH + 2 * KV_WIDTH + 2 * SGU_WIDTH
EPS = 1e-6
NEG_INF = -1e30

LANES = 128
VMEM_LIMIT = 56 * 1024 * 1024
FFN_TF = 512

BF16 = jnp.bfloat16
F32 = jnp.float32

_KV0 = ATTN_WIDTH
_U0 = ATTN_WIDTH + 2 * KV_WIDTH
_V0 = _U0 + SGU_WIDTH


def _rms_rows(x, gain):
    ms = jnp.mean(x * x, axis=-1, keepdims=True)
    return x * lax.rsqrt(ms + EPS) * gain


def _head_pair_rms(blk, gain2, lane_lo):
    sq = blk * blk
    lo = jnp.sum(jnp.where(lane_lo, sq, 0.0), axis=-1, keepdims=True)
    hi = jnp.sum(jnp.where(lane_lo, 0.0, sq), axis=-1, keepdims=True)
    inv = 1.0 / HEAD_DIM
    r = jnp.where(lane_lo, lax.rsqrt(lo * inv + EPS), lax.rsqrt(hi * inv + EPS))
    return blk * r * gain2


def _proj_kernel(chunk, emit_sv, x_ref, lng_ref, win_ref, qg_ref, kg_ref, sg_ref, ws_ref,
                 bs_ref, gs_ref, q_ref, k_ref, v_ref, mixs_ref, *rest):
    if emit_sv:
        sv_ref, so_scr = rest
    else:
        (so_scr,) = rest
    tm = x_ref.shape[0]
    h = _rms_rows(x_ref[...], lng_ref[...]).astype(BF16)
    lane_lo = lax.broadcasted_iota(jnp.int32, (tm, LANES), 1) < HEAD_DIM

    zq = jnp.dot(h, win_ref[:, 0:ATTN_WIDTH], preferred_element_type=F32)
    for c in range(ATTN_WIDTH // LANES):
        sl = slice(c * LANES, (c + 1) * LANES)
        q_ref[:, sl] = (_head_pair_rms(zq[:, sl], qg_ref[...], lane_lo)
                        * (HEAD_DIM ** -0.5)).astype(BF16)

    zkv = jnp.dot(h, win_ref[:, _KV0:_U0], preferred_element_type=F32)
    k_ref[...] = _head_pair_rms(zkv[:, 0:KV_WIDTH], kg_ref[...], lane_lo)
    v_ref[...] = zkv[:, KV_WIDTH:2 * KV_WIDTH]

    u = jax.nn.gelu(jnp.dot(h, win_ref[:, _U0:_V0], preferred_element_type=F32))
    av = jax.nn.gelu(jnp.dot(h, win_ref[:, _V0:IN_WIDTH], preferred_element_type=F32))
    sv = _rms_rows(av, sg_ref[...])
    if emit_sv:
        sv_ref[...] = sv
    svb = sv.astype(BF16)

    row_c = lax.broadcasted_iota(jnp.int32, (chunk, chunk), 0) // CHUNK
    col_c = lax.broadcasted_iota(jnp.int32, (chunk, chunk), 1) // CHUNK
    n_chunks = tm // chunk
    for g in range(SGU_GROUPS):
        cs = slice(g * SGU_GROUP_CH, (g + 1) * SGU_GROUP_CH)
        w = jnp.where(col_c <= row_c, ws_ref[g], 0.0).astype(BF16)
        rhs = jnp.concatenate(
            [svb[n * chunk:(n + 1) * chunk, cs] for n in range(n_chunks)], axis=1)
        sp = jnp.dot(w, rhs, preferred_element_type=F32)
        for n in range(n_chunks):
            rs = slice(n * chunk, (n + 1) * chunk)
            so_scr[rs, cs] = u[rs, cs] * (sp[:, n * LANES:(n + 1) * LANES] + bs_ref[g])
    mixs_ref[...] = _rms_rows(so_scr[...], gs_ref[...]).astype(BF16)


def _proj(x2d, lng, win, qg2, kg2, sg, ws, bsb, gs, *, tm, chunk, emit_sv):
    t = x2d.shape[0]
    const = lambda shape: pl.BlockSpec(shape, lambda i: (0,) * len(shape),
                                       pipeline_mode=pl.Buffered(1))
    row = lambda w: pl.BlockSpec((tm, w), lambda i: (i, 0))
    out_shape = [jax.ShapeDtypeStruct((t, ATTN_WIDTH), BF16),
                 jax.ShapeDtypeStruct((t, KV_WIDTH), F32),
                 jax.ShapeDtypeStruct((t, KV_WIDTH), F32),
                 jax.ShapeDtypeStruct((t, SGU_WIDTH), BF16)]
    out_specs = [row(ATTN_WIDTH), row(KV_WIDTH), row(KV_WIDTH), row(SGU_WIDTH)]
    if emit_sv:
        out_shape.append(jax.ShapeDtypeStruct((t, SGU_WIDTH), F32))
        out_specs.append(row(SGU_WIDTH))
    return pl.pallas_call(
        functools.partial(_proj_kernel, chunk, emit_sv),
        grid=(t // tm,),
        in_specs=[row(D_MODEL), const((1, D_MODEL)), const((D_MODEL, IN_WIDTH)),
                  const((1, LANES)), const((1, LANES)), const((1, SGU_WIDTH)),
                  const((SGU_GROUPS, chunk, chunk)), const((SGU_GROUPS, chunk, LANES)),
                  const((1, SGU_WIDTH))],
        out_specs=out_specs,
        out_shape=out_shape,
        scratch_shapes=[pltpu.VMEM((tm, SGU_WIDTH), F32)],
        compiler_params=pltpu.CompilerParams(dimension_semantics=("parallel",),
                                             vmem_limit_bytes=VMEM_LIMIT),
        name="proj_c%d" % chunk,
    )(x2d, lng, win, qg2, kg2, sg, ws, bsb, gs)


BAND = 2 * LANES
PAIRS = GQA_GROUP // 2


def _bias_kernel(bq, tab_ref, bkt_ref, valid_ref, out_ref):
    bkt = bkt_ref[...]
    for h in range(N_HEADS):
        g, pr, half = h // GQA_GROUP, (h % GQA_GROUP) // 2, h % 2
        acc = jnp.zeros(bkt.shape, F32)
        for b in range(N_BUCKETS):
            acc = jnp.where(bkt == b, tab_ref[b, h], acc)
        for s in range(valid_ref.shape[0]):
            out_ref[s, g, pr * bq:(pr + 1) * bq, half * BAND:(half + 1) * BAND] = (
                jnp.where(valid_ref[s] != 0, acc, NEG_INF))


def _bias(table, bucket, valid):
    nsel, bq, _ = valid.shape
    return pl.pallas_call(
        functools.partial(_bias_kernel, bq),
        in_specs=[pl.BlockSpec(memory_space=pltpu.SMEM),
                  pl.BlockSpec(memory_space=pltpu.VMEM),
                  pl.BlockSpec(memory_space=pltpu.VMEM)],
        out_specs=pl.BlockSpec(memory_space=pltpu.VMEM),
        out_shape=jax.ShapeDtypeStruct((nsel, N_KV_HEADS, PAIRS * bq, 2 * BAND), F32),
        name="bias_q%d" % bq,
    )(table, bucket, valid)


def _t5_bucket(n):
    half = N_BUCKETS // 2
    max_exact = half // 2
    offset = jnp.where(n < 0, half, 0)
    a = jnp.abs(n)
    af = jnp.maximum(a, 1).astype(F32)
    large = max_exact + (jnp.log(af / max_exact) / math.log(MAX_DISTANCE / max_exact)
                         * (half - max_exact)).astype(jnp.int32)
    large = jnp.minimum(large, half - 1)
    return offset + jnp.where(a < max_exact, a, large)


def _attn_kernel(bq, n_units, k_step, nsel, n_kparts, cast_wo, *refs):
    q_ref = refs[0]
    k_refs = refs[1:1 + n_kparts]
    v_refs = refs[1 + n_kparts:1 + 2 * n_kparts]
    rest = refs[1 + 2 * n_kparts:]
    if cast_wo:
        bias_ref, smax_ref, sl_ref, ga_ref, wo_ref, o_ref, wob_ref, kk_scr, vv_scr, ao_scr = rest
        wob_ref[...] = wo_ref[...].astype(BF16)
    else:
        bias_ref, smax_ref, sl_ref, ga_ref, o_ref, kk_scr, vv_scr, ao_scr = rest
    rows = PAIRS * bq

    def split_heads(parts, scr):
        x = jnp.concatenate([r[...] for r in parts], axis=0) if len(parts) > 1 else parts[0][...]
        lo = lax.broadcasted_iota(jnp.int32, x.shape, 1) < HEAD_DIM
        xr = pltpu.roll(x, HEAD_DIM, 1)
        scr[0, 0] = jnp.where(lo, x, 0.0).astype(BF16)
        scr[0, 1] = jnp.where(lo, 0.0, xr).astype(BF16)
        scr[1, 0] = jnp.where(lo, xr, 0.0).astype(BF16)
        scr[1, 1] = jnp.where(lo, 0.0, x).astype(BF16)

    split_heads(k_refs, kk_scr)
    split_heads(v_refs, vv_scr)
    r_i = lax.broadcasted_iota(jnp.int32, (2 * BAND, LANES), 0)
    l_i = lax.broadcasted_iota(jnp.int32, (2 * BAND, LANES), 1)
    ones_blk = jnp.where((r_i < BAND) == (l_i < HEAD_DIM), 1.0, 0.0).astype(BF16)
    lane_lo = lax.broadcasted_iota(jnp.int32, (rows, LANES), 1) < HEAD_DIM
    first_block = pl.program_id(1) == 0

    def unit(u, carry):
        q0 = pl.multiple_of(u * bq, bq)
        k0 = pl.multiple_of(u * k_step, k_step)
        sel = jnp.where(first_block & (u == 0), 0, 1) if nsel > 1 else 0
        for g in range(N_KV_HEADS):
            cols = [slice((PAIRS * g + p) * LANES, (PAIRS * g + p + 1) * LANES)
                    for p in range(PAIRS)]
            qs = jnp.concatenate([q_ref[pl.ds(q0, bq), c] for c in cols], axis=0)
            k2 = jnp.concatenate([kk_scr[g, 0, pl.ds(k0, BAND), :],
                                  kk_scr[g, 1, pl.ds(k0, BAND), :]], axis=0)
            s = lax.dot_general(qs, k2, (((1,), (1,)), ((), ())),
                                preferred_element_type=F32) + bias_ref[sel, g]
            sa, sb = s[:, :BAND], s[:, BAND:]
            ma = jnp.max(jnp.maximum(jnp.maximum(sa[:, :LANES], sa[:, LANES:]), smax_ref[g, 0]),
                         axis=-1, keepdims=True)
            mb = jnp.max(jnp.maximum(jnp.maximum(sb[:, :LANES], sb[:, LANES:]), smax_ref[g, 1]),
                         axis=-1, keepdims=True)
            p = jnp.concatenate([jnp.exp(sa - ma), jnp.exp(sb - mb)], axis=1).astype(BF16)
            v2 = jnp.concatenate([vv_scr[g, 0, pl.ds(k0, BAND), :],
                                  vv_scr[g, 1, pl.ds(k0, BAND), :]], axis=0)
            ol = jnp.dot(p, jnp.concatenate([v2, ones_blk], axis=1), preferred_element_type=F32)
            denom = ol[:, LANES:] + jnp.exp(sl_ref[g] - jnp.where(lane_lo, ma, mb))
            out = ol[:, :LANES] / denom
            for p_i, c in enumerate(cols):
                ao_scr[pl.ds(q0, bq), c] = out[p_i * bq:(p_i + 1) * bq]
        return carry

    lax.fori_loop(0, n_units, unit, 0)
    o_ref[...] = _rms_rows(ao_scr[...], ga_ref[...]).astype(BF16)


def _attn(q, k_parts, v_parts, k_maps, bias, smax, sl, ga, wo=None, *, bq, n_units, k_step):
    nb, s, _ = q.shape
    nsel = bias.shape[0]
    tq = n_units * bq
    k_rows = [rows for _, rows in k_parts]
    kr = sum(k_rows)
    cur = lambda b, i: (b, i, 0)
    full = lambda a: pl.BlockSpec(a.shape, lambda b, i: (0,) * a.ndim,
                                  pipeline_mode=pl.Buffered(1))
    kv_specs = [pl.BlockSpec((None, r, KV_WIDTH), m) for r, m in zip(k_rows, k_maps)]
    n_steps = nb * (s // tq)
    out_specs = [pl.BlockSpec((None, tq, ATTN_WIDTH), cur)]
    out_shape = [jax.ShapeDtypeStruct(q.shape, BF16)]
    extra_in, extra_specs = [], []
    if wo is not None:
        slab = pl.BlockSpec((wo.shape[0] // n_steps, wo.shape[1]),
                            lambda b, i: (b * (s // tq) + i, 0))
        extra_in, extra_specs = [wo], [slab]
        out_specs.append(slab)
        out_shape.append(jax.ShapeDtypeStruct(wo.shape, BF16))
    return pl.pallas_call(
        functools.partial(_attn_kernel, bq, n_units, k_step, nsel, len(k_parts), wo is not None),
        grid=(nb, s // tq),
        in_specs=([pl.BlockSpec((None, tq, ATTN_WIDTH), cur)] + kv_specs + kv_specs
                  + [full(bias), full(smax), full(sl), full(ga)] + extra_specs),
        out_specs=out_specs,
        out_shape=out_shape,
        scratch_shapes=[pltpu.VMEM((N_KV_HEADS, 2, kr, KV_WIDTH), BF16),
                        pltpu.VMEM((N_KV_HEADS, 2, kr, KV_WIDTH), BF16),
                        pltpu.VMEM((tq, ATTN_WIDTH), F32)],
        compiler_params=pltpu.CompilerParams(dimension_semantics=("parallel", "arbitrary"),
                                             vmem_limit_bytes=VMEM_LIMIT),
        name="attn_q%d" % bq,
    )(q, *[a for a, _ in k_parts], *[a for a, _ in v_parts], bias, smax, sl, ga, *extra_in)


def _outproj_kernel(cast_w, x_ref, ma_ref, ms_ref, wo_ref, g_ref, *rest):
    if cast_w:
        wup_ref, wdn_ref, o_ref, h_ref, wupb_ref, wdnb_ref = rest
        for j in range(wupb_ref.shape[0]):
            tf = wupb_ref.shape[2]
            wupb_ref[j] = wup_ref[:, j * tf:(j + 1) * tf].astype(BF16)
        wdnb_ref[...] = wdn_ref[...].astype(BF16)
    else:
        o_ref, h_ref = rest
    acc = jnp.dot(ma_ref[...], wo_ref[0:ATTN_WIDTH, :], preferred_element_type=F32)
    acc = acc + jnp.dot(ms_ref[...], wo_ref[ATTN_WIDTH:, :], preferred_element_type=F32)
    x1 = x_ref[...] + acc
    o_ref[...] = x1
    h_ref[...] = _rms_rows(x1, g_ref[...]).astype(BF16)


def _outproj(x2d, mix_a, mix_s, wo, g, wup=None, wdn=None, *, tm, tf):
    t = x2d.shape[0]
    n_steps = t // tm
    row = lambda w: pl.BlockSpec((tm, w), lambda i: (i, 0))
    const = lambda a: pl.BlockSpec(a.shape, lambda i: (0,) * a.ndim, pipeline_mode=pl.Buffered(1))
    in_specs = [row(D_MODEL), row(ATTN_WIDTH), row(SGU_WIDTH), const(wo), const(g)]
    out_specs = [row(D_MODEL), row(D_MODEL)]
    out_shape = [jax.ShapeDtypeStruct((t, D_MODEL), F32), jax.ShapeDtypeStruct((t, D_MODEL), BF16)]
    args = [x2d, mix_a, mix_s, wo, g]
    cast_w = wup is not None
    if cast_w:
        up_rows, dn_rows = D_MODEL // n_steps, D_FF // n_steps
        in_specs += [pl.BlockSpec((up_rows, D_FF), lambda i: (i, 0)),
                     pl.BlockSpec((dn_rows, D_MODEL), lambda i: (i, 0))]
        out_specs += [pl.BlockSpec((D_FF // tf, up_rows, tf), lambda i: (0, i, 0)),
                      pl.BlockSpec((dn_rows, D_MODEL), lambda i: (i, 0))]
        out_shape += [jax.ShapeDtypeStruct((D_FF // tf, D_MODEL, tf), BF16),
                      jax.ShapeDtypeStruct((D_FF, D_MODEL), BF16)]
        args += [wup, wdn]
    return pl.pallas_call(
        functools.partial(_outproj_kernel, cast_w),
        grid=(n_steps,),
        in_specs=in_specs,
        out_specs=out_specs,
        out_shape=out_shape,
        compiler_params=pltpu.CompilerParams(dimension_semantics=("parallel",),
                                             vmem_limit_bytes=VMEM_LIMIT),
        name="outproj",
    )(*args)


def _ffn_kernel(x_ref, h_ref, wup_ref, wdn_ref, y_ref):
    a = jnp.dot(h_ref[...], wup_ref[...], preferred_element_type=F32)
    a = jnp.square(jnp.maximum(a, 0.0)).astype(BF16)
    base = jnp.where(pl.program_id(1) == 0, x_ref[...], y_ref[...])
    y_ref[...] = base + jnp.dot(a, wdn_ref[...], preferred_element_type=F32)


def _ffn(x2d, h2d, wup_t, wdn, *, tm):
    t = x2d.shape[0]
    n_f, _, tf = wup_t.shape
    return pl.pallas_call(
        _ffn_kernel,
        grid=(t // tm, n_f),
        in_specs=[pl.BlockSpec((tm, D_MODEL), lambda i, j: (i, 0)),
                  pl.BlockSpec((tm, D_MODEL), lambda i, j: (i, 0)),
                  pl.BlockSpec((None, D_MODEL, tf), lambda i, j: (j, 0, 0)),
                  pl.BlockSpec((tf, D_MODEL), lambda i, j: (j, 0))],
        out_specs=pl.BlockSpec((tm, D_MODEL), lambda i, j: (i, 0)),
        out_shape=jax.ShapeDtypeStruct((t, D_MODEL), F32),
        compiler_params=pltpu.CompilerParams(dimension_semantics=("parallel", "arbitrary"),
                                             vmem_limit_bytes=VMEM_LIMIT),
        name="ffn",
    )(x2d, h2d, wup_t, wdn)


def _band_tables(bq, nsel):
    qi = jnp.arange(bq)[:, None]
    kj = jnp.arange(BAND)[None, :] - WINDOW
    qc, kc = qi // CHUNK, jnp.floor_divide(kj, CHUNK)
    in_band = (kc <= qc) & (kc >= qc - WINDOW // CHUNK)
    valid = [in_band & (kj >= 0)] if nsel == 2 else []
    valid.append(in_band)
    return _t5_bucket(qi - kj), jnp.stack(valid).astype(jnp.int32)


def _sink_tables(sinks, bq):
    s3 = sinks.reshape(N_KV_HEADS, PAIRS, 2)
    rows = PAIRS * bq
    smax = jnp.broadcast_to(jnp.transpose(s3, (0, 2, 1))[:, :, :, None, None],
                            (N_KV_HEADS, 2, PAIRS, bq, LANES)).reshape(N_KV_HEADS, 2, rows, LANES)
    sl = jnp.broadcast_to(jnp.repeat(s3, HEAD_DIM, axis=-1)[:, :, None, :],
                          (N_KV_HEADS, PAIRS, bq, LANES)).reshape(N_KV_HEADS, rows, LANES)
    return smax, sl


def kernel(x_prompt, x_sample, cache_attn_k, cache_attn_v, rel_bias_table, ln_mix_g, w_in,
           q_norm_g, k_norm_g, attn_sinks, sgu_norm_g, sgu_w, sgu_b, out_norm_attn_g,
           out_norm_sgu_g, w_out, ln_ffn_g, w_ffn_up, w_ffn_down):
    bp, sp, _ = x_prompt.shape
    bs, ss, _ = x_sample.shape
    depth = w_in.shape[0]
    assert depth == 1 and ss == CHUNK and cache_attn_k.shape[2] == WINDOW
    l = 0
    xp = x_prompt.reshape(bp * sp, D_MODEL)
    xs = x_sample.reshape(bs * ss, D_MODEL)

    win = w_in[l].astype(BF16)
    lng = ln_mix_g[l].reshape(1, D_MODEL)
    qg2 = jnp.tile(q_norm_g[l], 2).reshape(1, LANES)
    kg2 = jnp.tile(k_norm_g[l], 2).reshape(1, LANES)
    sg = sgu_norm_g[l].reshape(1, SGU_WIDTH)
    gs = out_norm_sgu_g[l].reshape(1, SGU_WIDTH)
    ga = out_norm_attn_g[l].reshape(1, ATTN_WIDTH)
    lnf = ln_ffn_g[l].reshape(1, D_MODEL)
    sinks = attn_sinks[l].reshape(N_HEADS)

    def sgu_params(c):
        return (sgu_w[l][:, :c, :c],
                jnp.broadcast_to(sgu_b[l][:, :c, None], (SGU_GROUPS, c, LANES)))

    ws_p, bs_p = sgu_params(SGU_CHUNK)
    q, k, v, mix_s = _proj(xp, lng, win, qg2, kg2, sg, ws_p, bs_p, gs,
                           tm=512, chunk=SGU_CHUNK, emit_sv=False)
    bq, units = 2 * CHUNK, 4
    bias_p = _bias(rel_bias_table, *_band_tables(bq, 2))
    k3 = k.reshape(bp, sp, KV_WIDTH)
    v3 = v.reshape(bp, sp, KV_WIDTH)
    prev = lambda b, i: (b, jnp.maximum(i * units - 1, 0), 0)
    cur = lambda b, i: (b, i, 0)
    mix_a, wo = _attn(q.reshape(bp, sp, ATTN_WIDTH),
                      [(k3, WINDOW), (k3, units * bq)], [(v3, WINDOW), (v3, units * bq)],
                      [prev, cur], bias_p, *_sink_tables(sinks, bq), ga, w_out[l],
                      bq=bq, n_units=units, k_step=bq)
    x1p, h2p, wup, wdn = _outproj(xp, mix_a.reshape(bp * sp, ATTN_WIDTH), mix_s, wo, lnf,
                                  w_ffn_up[l], w_ffn_down[l], tm=512, tf=FFN_TF)
    yp = _ffn(x1p, h2p, wup, wdn, tm=1024)

    ws_s, bs_s = sgu_params(ss)
    qs, ks, vs, mix_ss, svs = _proj(xs, lng, win, qg2, kg2, sg, ws_s, bs_s, gs,
                                    tm=512, chunk=ss, emit_sv=True)
    bias_s = _bias(rel_bias_table, *_band_tables(ss, 1))
    seqs = 4
    pad = jnp.zeros((bs, BAND - WINDOW - ss, KV_WIDTH), F32)
    k_all = jnp.concatenate([cache_attn_k[l].reshape(bs, WINDOW, KV_WIDTH),
                             ks.reshape(bs, ss, KV_WIDTH), pad], axis=1)
    v_all = jnp.concatenate([cache_attn_v[l].reshape(bs, WINDOW, KV_WIDTH),
                             vs.reshape(bs, ss, KV_WIDTH), pad], axis=1)
    (mix_as,) = _attn(qs.reshape(bs // seqs, seqs * ss, ATTN_WIDTH),
                      [(k_all.reshape(bs // seqs, seqs * BAND, KV_WIDTH), seqs * BAND)],
                      [(v_all.reshape(bs // seqs, seqs * BAND, KV_WIDTH), seqs * BAND)],
                      [cur], bias_s, *_sink_tables(sinks, ss), ga,
                      bq=ss, n_units=seqs, k_step=BAND)
    x1s, h2s = _outproj(xs, mix_as.reshape(bs * ss, ATTN_WIDTH), mix_ss, wo, lnf, tm=512, tf=FFN_TF)
    ys = _ffn(x1s, h2s, wup, wdn, tm=1024)

    keep = min(WINDOW, sp)
    kv_shape = (N_KV_HEADS, HEAD_DIM)
    return (yp.reshape(bp, sp, D_MODEL),
            ys.reshape(bs, ss, D_MODEL),
            k3[:, -keep:].reshape(1, bp, keep, *kv_shape),
            v3[:, -keep:].reshape(1, bp, keep, *kv_shape),
            ks.reshape(1, bs, ss, *kv_shape),
            vs.reshape(1, bs, ss, *kv_shape),
            svs.reshape(1, bs, ss, SGU_WIDTH))
```

```python
import functools
import math

import jax
import jax.numpy as jnp
from jax import lax
from jax.experimental import pallas as pl
from jax.experimental.pallas import tpu as pltpu

D_MODEL = 2048
CHUNK = 64
ATTN_WIDTH = 1024
SGU_WIDTH = 1024
HEAD_DIM = 64
N_HEADS = ATTN_WIDTH // HEAD_DIM
N_KV_HEADS = 2
GQA_GROUP = N_HEADS // N_KV_HEADS
WINDOW = 128
N_BUCKETS = 32
MAX_DISTANCE = 128
SGU_CHUNK = 128
SGU_GROUPS = 8
SGU_GROUP_CH = SGU_WIDTH // SGU_GROUPS
D_FF = 4 * D_MODEL
KV_WIDTH = N_KV_HEADS * HEAD_DIM
IN_WIDTH = ATTN_WIDTH + 2 * KV_WIDTH + 2 * SGU_WIDTH
EPS = 1e-6
NEG_INF = -1e30

LANES = 128
VMEM_LIMIT = 56 * 1024 * 1024
FFN_TF = 512

BF16 = jnp.bfloat16
F32 = jnp.float32

_KV0 = ATTN_WIDTH
_U0 = ATTN_WIDTH + 2 * KV_WIDTH
_V0 = _U0 + SGU_WIDTH


def _rms_rows(x, gain):
    ms = jnp.mean(x * x, axis=-1, keepdims=True)
    return x * lax.rsqrt(ms + EPS) * gain


def _head_pair_rms(blk, gain2, lane_lo):
    sq = blk * blk
    lo = jnp.sum(jnp.where(lane_lo, sq, 0.0), axis=-1, keepdims=True)
    hi = jnp.sum(jnp.where(lane_lo, 0.0, sq), axis=-1, keepdims=True)
    inv = 1.0 / HEAD_DIM
    r = jnp.where(lane_lo, lax.rsqrt(lo * inv + EPS), lax.rsqrt(hi * inv + EPS))
    return blk * r * gain2


def _cast_ffn_weights(wup_ref, wdn_ref, wupb_ref, wdnb_ref):
    tf = wupb_ref.shape[2]
    for j in range(wupb_ref.shape[0]):
        wupb_ref[j] = wup_ref[:, j * tf:(j + 1) * tf].astype(BF16)
    wdnb_ref[...] = wdn_ref[...].astype(BF16)


def _proj_kernel(chunk, emit_sv, cast_w, x_ref, lng_ref, win_ref, qg_ref, kg_ref, sg_ref,
                 ws_ref, bs_ref, gs_ref, *rest):
    if cast_w:
        _cast_ffn_weights(rest[0], rest[1], rest[-3], rest[-2])
        rest = rest[2:-3] + rest[-1:]
    q_ref, k_ref, v_ref, mixs_ref = rest[:4]
    so_scr = rest[-1]
    tm = x_ref.shape[0]
    h = _rms_rows(x_ref[...], lng_ref[...]).astype(BF16)
    lane_lo = lax.broadcasted_iota(jnp.int32, (tm, LANES), 1) < HEAD_DIM

    av = jax.nn.gelu(jnp.dot(h, win_ref[:, _V0:IN_WIDTH], preferred_element_type=F32))
    sv = _rms_rows(av, sg_ref[...])
    if emit_sv:
        rest[4][...] = sv
    svb = sv.astype(BF16)
    u = jax.nn.gelu(jnp.dot(h, win_ref[:, _U0:_V0], preferred_element_type=F32))

    zq = jnp.dot(h, win_ref[:, 0:ATTN_WIDTH], preferred_element_type=F32)
    for c in range(ATTN_WIDTH // LANES):
        sl = slice(c * LANES, (c + 1) * LANES)
        q_ref[:, sl] = (_head_pair_rms(zq[:, sl], qg_ref[...], lane_lo)
                        * (HEAD_DIM ** -0.5)).astype(BF16)

    row_c = lax.broadcasted_iota(jnp.int32, (chunk, chunk), 0) // CHUNK
    col_c = lax.broadcasted_iota(jnp.int32, (chunk, chunk), 1) // CHUNK
    n_chunks = tm // chunk
    for g in range(SGU_GROUPS):
        cs = slice(g * SGU_GROUP_CH, (g + 1) * SGU_GROUP_CH)
        w = jnp.where(col_c <= row_c, ws_ref[g], 0.0).astype(BF16)
        rhs = jnp.concatenate(
            [svb[n * chunk:(n + 1) * chunk, cs] for n in range(n_chunks)], axis=1)
        sp = jnp.dot(w, rhs, preferred_element_type=F32)
        for n in range(n_chunks):
            rs = slice(n * chunk, (n + 1) * chunk)
            so_scr[rs, cs] = u[rs, cs] * (sp[:, n * LANES:(n + 1) * LANES] + bs_ref[g])

    zkv = jnp.dot(h, win_ref[:, _KV0:_U0], preferred_element_type=F32)
    k_ref[...] = _head_pair_rms(zkv[:, 0:KV_WIDTH], kg_ref[...], lane_lo)
    v_ref[...] = zkv[:, KV_WIDTH:2 * KV_WIDTH]
    mixs_ref[...] = _rms_rows(so_scr[...], gs_ref[...]).astype(BF16)


def _proj(x2d, lng, win, qg2, kg2, sg, ws, bsb, gs, wup=None, wdn=None, *, tm, chunk, emit_sv,
          tf=None):
    t = x2d.shape[0]
    const = lambda shape: pl.BlockSpec(shape, lambda i: (0,) * len(shape),
                                       pipeline_mode=pl.Buffered(1))
    row = lambda w: pl.BlockSpec((tm, w), lambda i: (i, 0))
    out_shape = [jax.ShapeDtypeStruct((t, ATTN_WIDTH), BF16),
                 jax.ShapeDtypeStruct((t, KV_WIDTH), F32),
                 jax.ShapeDtypeStruct((t, KV_WIDTH), F32),
                 jax.ShapeDtypeStruct((t, SGU_WIDTH), BF16)]
    out_specs = [row(ATTN_WIDTH), row(KV_WIDTH), row(KV_WIDTH), row(SGU_WIDTH)]
    if emit_sv:
        out_shape.append(jax.ShapeDtypeStruct((t, SGU_WIDTH), F32))
        out_specs.append(row(SGU_WIDTH))
    in_specs = [row(D_MODEL), const((1, D_MODEL)), const((D_MODEL, IN_WIDTH)),
                const((1, LANES)), const((1, LANES)), const((1, SGU_WIDTH)),
                const((SGU_GROUPS, chunk, chunk)), const((SGU_GROUPS, chunk, LANES)),
                const((1, SGU_WIDTH))]
    args = [x2d, lng, win, qg2, kg2, sg, ws, bsb, gs]
    if wup is not None:
        n_steps = t // tm
        up_rows, dn_rows = D_MODEL // n_steps, D_FF // n_steps
        in_specs += [pl.BlockSpec((up_rows, D_FF), lambda i: (i, 0)),
                     pl.BlockSpec((dn_rows, D_MODEL), lambda i: (i, 0))]
        out_specs += [pl.BlockSpec((D_FF // tf, up_rows, tf), lambda i: (0, i, 0)),
                      pl.BlockSpec((dn_rows, D_MODEL), lambda i: (i, 0))]
        out_shape += [jax.ShapeDtypeStruct((D_FF // tf, D_MODEL, tf), BF16),
                      jax.ShapeDtypeStruct((D_FF, D_MODEL), BF16)]
        args += [wup, wdn]
    return pl.pallas_call(
        functools.partial(_proj_kernel, chunk, emit_sv, wup is not None),
        grid=(t // tm,),
        in_specs=in_specs,
        out_specs=out_specs,
        out_shape=out_shape,
        scratch_shapes=[pltpu.VMEM((tm, SGU_WIDTH), F32)],
        compiler_params=pltpu.CompilerParams(dimension_semantics=("parallel",),
                                             vmem_limit_bytes=VMEM_LIMIT),
        name="proj_c%d" % chunk,
    )(*args)


BAND = 2 * LANES
PAIRS = GQA_GROUP // 2


def _bias_kernel(bq, tab_ref, bkt_ref, valid_ref, out_ref):
    bkt = bkt_ref[...]
    for h in range(N_HEADS):
        g, pr, half = h // GQA_GROUP, (h % GQA_GROUP) // 2, h % 2
        acc = jnp.zeros(bkt.shape, F32)
        for b in range(N_BUCKETS):
            acc = jnp.where(bkt == b, tab_ref[b, h], acc)
        for s in range(valid_ref.shape[0]):
            out_ref[s, g, pr * bq:(pr + 1) * bq, half * BAND:(half + 1) * BAND] = (
                jnp.where(valid_ref[s] != 0, acc, NEG_INF))


def _bias(table, bucket, valid):
    nsel, bq, _ = valid.shape
    return pl.pallas_call(
        functools.partial(_bias_kernel, bq),
        in_specs=[pl.BlockSpec(memory_space=pltpu.SMEM),
                  pl.BlockSpec(memory_space=pltpu.VMEM),
                  pl.BlockSpec(memory_space=pltpu.VMEM)],
        out_specs=pl.BlockSpec(memory_space=pltpu.VMEM),
        out_shape=jax.ShapeDtypeStruct((nsel, N_KV_HEADS, PAIRS * bq, 2 * BAND), F32),
        name="bias_q%d" % bq,
    )(table, bucket, valid)


def _t5_bucket(n):
    half = N_BUCKETS // 2
    max_exact = half // 2
    offset = jnp.where(n < 0, half, 0)
    a = jnp.abs(n)
    af = jnp.maximum(a, 1).astype(F32)
    large = max_exact + (jnp.log(af / max_exact) / math.log(MAX_DISTANCE / max_exact)
                         * (half - max_exact)).astype(jnp.int32)
    large = jnp.minimum(large, half - 1)
    return offset + jnp.where(a < max_exact, a, large)


def _attn_kernel(bq, n_units, k_step, nsel, n_kparts, cast_wo, *refs):
    q_ref = refs[0]
    k_refs = refs[1:1 + n_kparts]
    v_refs = refs[1 + n_kparts:1 + 2 * n_kparts]
    rest = refs[1 + 2 * n_kparts:]
    if cast_wo:
        bias_ref, smax_ref, sl_ref, ga_ref, wo_ref, o_ref, wob_ref, kk_scr, vv_scr, ao_scr = rest
        wob_ref[...] = wo_ref[...].astype(BF16)
    else:
        bias_ref, smax_ref, sl_ref, ga_ref, o_ref, kk_scr, vv_scr, ao_scr = rest
    rows = PAIRS * bq

    def split_heads(parts, scr):
        x = jnp.concatenate([r[...] for r in parts], axis=0) if len(parts) > 1 else parts[0][...]
        lo = lax.broadcasted_iota(jnp.int32, x.shape, 1) < HEAD_DIM
        xr = pltpu.roll(x, HEAD_DIM, 1)
        scr[0, 0] = jnp.where(lo, x, 0.0).astype(BF16)
        scr[0, 1] = jnp.where(lo, 0.0, xr).astype(BF16)
        scr[1, 0] = jnp.where(lo, xr, 0.0).astype(BF16)
        scr[1, 1] = jnp.where(lo, 0.0, x).astype(BF16)

    split_heads(k_refs, kk_scr)
    split_heads(v_refs, vv_scr)
    r_i = lax.broadcasted_iota(jnp.int32, (2 * BAND, LANES), 0)
    l_i = lax.broadcasted_iota(jnp.int32, (2 * BAND, LANES), 1)
    ones_blk = jnp.where((r_i < BAND) == (l_i < HEAD_DIM), 1.0, 0.0).astype(BF16)
    lane_lo = lax.broadcasted_iota(jnp.int32, (rows, LANES), 1) < HEAD_DIM
    first_block = pl.program_id(1) == 0

    def unit(u, carry):
        q0 = pl.multiple_of(u * bq, bq)
        k0 = pl.multiple_of(u * k_step, k_step)
        sel = jnp.where(first_block & (u == 0), 0, 1) if nsel > 1 else 0
        for g in range(N_KV_HEADS):
            cols = [slice((PAIRS * g + p) * LANES, (PAIRS * g + p + 1) * LANES)
                    for p in range(PAIRS)]
            qs = jnp.concatenate([q_ref[pl.ds(q0, bq), c] for c in cols], axis=0)
            k2 = jnp.concatenate([kk_scr[g, 0, pl.ds(k0, BAND), :],
                                  kk_scr[g, 1, pl.ds(k0, BAND), :]], axis=0)
            s = lax.dot_general(qs, k2, (((1,), (1,)), ((), ())),
                                preferred_element_type=F32) + bias_ref[sel, g]
            sa, sb = s[:, :BAND], s[:, BAND:]
            ma = jnp.max(jnp.maximum(jnp.maximum(sa[:, :LANES], sa[:, LANES:]), smax_ref[g, 0]),
                         axis=-1, keepdims=True)
            mb = jnp.max(jnp.maximum(jnp.maximum(sb[:, :LANES], sb[:, LANES:]), smax_ref[g, 1]),
                         axis=-1, keepdims=True)
            p = jnp.concatenate([jnp.exp(sa - ma), jnp.exp(sb - mb)], axis=1).astype(BF16)
            v2 = jnp.concatenate([vv_scr[g, 0, pl.ds(k0, BAND), :],
                                  vv_scr[g, 1, pl.ds(k0, BAND), :]], axis=0)
            ol = jnp.dot(p, jnp.concatenate([v2, ones_blk], axis=1), preferred_element_type=F32)
            denom = ol[:, LANES:] + jnp.exp(sl_ref[g] - jnp.where(lane_lo, ma, mb))
            out = ol[:, :LANES] / denom
            for p_i, c in enumerate(cols):
                ao_scr[pl.ds(q0, bq), c] = out[p_i * bq:(p_i + 1) * bq]
        return carry

    lax.fori_loop(0, n_units, unit, 0, unroll=2)
    o_ref[...] = _rms_rows(ao_scr[...], ga_ref[...]).astype(BF16)


def _attn(q, k_parts, v_parts, k_maps, bias, smax, sl, ga, wo=None, *, bq, n_units, k_step):
    nb, s, _ = q.shape
    nsel = bias.shape[0]
    tq = n_units * bq
    k_rows = [rows for _, rows in k_parts]
    kr = sum(k_rows)
    cur = lambda b, i: (b, i, 0)
    full = lambda a: pl.BlockSpec(a.shape, lambda b, i: (0,) * a.ndim,
                                  pipeline_mode=pl.Buffered(1))
    kv_specs = [pl.BlockSpec((None, r, KV_WIDTH), m) for r, m in zip(k_rows, k_maps)]
    n_steps = nb * (s // tq)
    out_specs = [pl.BlockSpec((None, tq, ATTN_WIDTH), cur)]
    out_shape = [jax.ShapeDtypeStruct(q.shape, BF16)]
    extra_in, extra_specs = [], []
    if wo is not None:
        slab = pl.BlockSpec((wo.shape[0] // n_steps, wo.shape[1]),
                            lambda b, i: (b * (s // tq) + i, 0))
        extra_in, extra_specs = [wo], [slab]
        out_specs.append(slab)
        out_shape.append(jax.ShapeDtypeStruct(wo.shape, BF16))
    return pl.pallas_call(
        functools.partial(_attn_kernel, bq, n_units, k_step, nsel, len(k_parts), wo is not None),
        grid=(nb, s // tq),
        in_specs=([pl.BlockSpec((None, tq, ATTN_WIDTH), cur)] + kv_specs + kv_specs
                  + [full(bias), full(smax), full(sl), full(ga)] + extra_specs),
        out_specs=out_specs,
        out_shape=out_shape,
        scratch_shapes=[pltpu.VMEM((N_KV_HEADS, 2, kr, KV_WIDTH), BF16),
                        pltpu.VMEM((N_KV_HEADS, 2, kr, KV_WIDTH), BF16),
                        pltpu.VMEM((tq, ATTN_WIDTH), F32)],
        compiler_params=pltpu.CompilerParams(dimension_semantics=("parallel", "arbitrary"),
                                             vmem_limit_bytes=VMEM_LIMIT),
        name="attn_q%d" % bq,
    )(q, *[a for a, _ in k_parts], *[a for a, _ in v_parts], bias, smax, sl, ga, *extra_in)


def _outproj_kernel(x_ref, ma_ref, ms_ref, wo_ref, g_ref, o_ref, h_ref):
    acc = jnp.dot(ma_ref[...], wo_ref[0:ATTN_WIDTH, :], preferred_element_type=F32)
    acc = acc + jnp.dot(ms_ref[...], wo_ref[ATTN_WIDTH:, :], preferred_element_type=F32)
    x1 = x_ref[...] + acc
    o_ref[...] = x1
    h_ref[...] = _rms_rows(x1, g_ref[...]).astype(BF16)


def _outproj(x2d, mix_a, mix_s, wo, g, *, tm):
    t = x2d.shape[0]
    row = lambda w: pl.BlockSpec((tm, w), lambda i: (i, 0))
    const = lambda a: pl.BlockSpec(a.shape, lambda i: (0,) * a.ndim, pipeline_mode=pl.Buffered(1))
    return pl.pallas_call(
        _outproj_kernel,
        grid=(t // tm,),
        in_specs=[row(D_MODEL), row(ATTN_WIDTH), row(SGU_WIDTH), const(wo), const(g)],
        out_specs=[row(D_MODEL), row(D_MODEL)],
        out_shape=[jax.ShapeDtypeStruct((t, D_MODEL), F32),
                   jax.ShapeDtypeStruct((t, D_MODEL), BF16)],
        compiler_params=pltpu.CompilerParams(dimension_semantics=("parallel",),
                                             vmem_limit_bytes=VMEM_LIMIT),
        name="outproj",
    )(x2d, mix_a, mix_s, wo, g)


def _ffn_kernel(x_ref, h_ref, wup_ref, wdn_ref, y_ref):
    a = jnp.dot(h_ref[...], wup_ref[...], preferred_element_type=F32)
    a = jnp.square(jnp.maximum(a, 0.0)).astype(BF16)
    base = jnp.where(pl.program_id(1) == 0, x_ref[...], y_ref[...])
    y_ref[...] = base + jnp.dot(a, wdn_ref[...], preferred_element_type=F32)


def _ffn(x2d, h2d, wup_t, wdn, *, tm):
    t = x2d.shape[0]
    n_f, _, tf = wup_t.shape
    return pl.pallas_call(
        _ffn_kernel,
        grid=(t // tm, n_f),
        in_specs=[pl.BlockSpec((tm, D_MODEL), lambda i, j: (i, 0)),
                  pl.BlockSpec((tm, D_MODEL), lambda i, j: (i, 0)),
                  pl.BlockSpec((None, D_MODEL, tf), lambda i, j: (j, 0, 0)),
                  pl.BlockSpec((tf, D_MODEL), lambda i, j: (j, 0))],
        out_specs=pl.BlockSpec((tm, D_MODEL), lambda i, j: (i, 0)),
        out_shape=jax.ShapeDtypeStruct((t, D_MODEL), F32),
        compiler_params=pltpu.CompilerParams(dimension_semantics=("parallel", "arbitrary"),
                                             vmem_limit_bytes=VMEM_LIMIT),
        name="ffn",
    )(x2d, h2d, wup_t, wdn)


def _band_tables(bq, nsel):
    qi = jnp.arange(bq)[:, None]
    kj = jnp.arange(BAND)[None, :] - WINDOW
    qc, kc = qi // CHUNK, jnp.floor_divide(kj, CHUNK)
    in_band = (kc <= qc) & (kc >= qc - WINDOW // CHUNK)
    valid = [in_band & (kj >= 0)] if nsel == 2 else []
    valid.append(in_band)
    return _t5_bucket(qi - kj), jnp.stack(valid).astype(jnp.int32)


def _sink_tables(sinks, bq):
    s3 = sinks.reshape(N_KV_HEADS, PAIRS, 2)
    rows = PAIRS * bq
    smax = jnp.broadcast_to(jnp.transpose(s3, (0, 2, 1))[:, :, :, None, None],
                            (N_KV_HEADS, 2, PAIRS, bq, LANES)).reshape(N_KV_HEADS, 2, rows, LANES)
    sl = jnp.broadcast_to(jnp.repeat(s3, HEAD_DIM, axis=-1)[:, :, None, :],
                          (N_KV_HEADS, PAIRS, bq, LANES)).reshape(N_KV_HEADS, rows, LANES)
    return smax, sl


def kernel(x_prompt, x_sample, cache_attn_k, cache_attn_v, rel_bias_table, ln_mix_g, w_in,
           q_norm_g, k_norm_g, attn_sinks, sgu_norm_g, sgu_w, sgu_b, out_norm_attn_g,
           out_norm_sgu_g, w_out, ln_ffn_g, w_ffn_up, w_ffn_down):
    bp, sp, _ = x_prompt.shape
    bs, ss, _ = x_sample.shape
    depth = w_in.shape[0]
    assert depth == 1 and ss == CHUNK and cache_attn_k.shape[2] == WINDOW
    l = 0
    xp = x_prompt.reshape(bp * sp, D_MODEL)
    xs = x_sample.reshape(bs * ss, D_MODEL)

    win = w_in[l].astype(BF16)
    lng = ln_mix_g[l].reshape(1, D_MODEL)
    qg2 = jnp.tile(q_norm_g[l], 2).reshape(1, LANES)
    kg2 = jnp.tile(k_norm_g[l], 2).reshape(1, LANES)
    sg = sgu_norm_g[l].reshape(1, SGU_WIDTH)
    gs = out_norm_sgu_g[l].reshape(1, SGU_WIDTH)
    ga = out_norm_attn_g[l].reshape(1, ATTN_WIDTH)
    lnf = ln_ffn_g[l].reshape(1, D_MODEL)
    sinks = attn_sinks[l].reshape(N_HEADS)

    def sgu_params(c):
        return (sgu_w[l][:, :c, :c],
                jnp.broadcast_to(sgu_b[l][:, :c, None], (SGU_GROUPS, c, LANES)))

    ws_p, bs_p = sgu_params(SGU_CHUNK)
    q, k, v, mix_s, wup, wdn = _proj(xp, lng, win, qg2, kg2, sg, ws_p, bs_p, gs,
                                     w_ffn_up[l], w_ffn_down[l],
                                     tm=512, chunk=SGU_CHUNK, emit_sv=False, tf=FFN_TF)
    bq, units = 2 * CHUNK, 4
    bias_p = _bias(rel_bias_table, *_band_tables(bq, 2))
    k3 = k.reshape(bp, sp, KV_WIDTH)
    v3 = v.reshape(bp, sp, KV_WIDTH)
    prev = lambda b, i: (b, jnp.maximum(i * units - 1, 0), 0)
    cur = lambda b, i: (b, i, 0)
    mix_a, wo = _attn(q.reshape(bp, sp, ATTN_WIDTH),
                      [(k3, WINDOW), (k3, units * bq)], [(v3, WINDOW), (v3, units * bq)],
                      [prev, cur], bias_p, *_sink_tables(sinks, bq), ga, w_out[l],
                      bq=bq, n_units=units, k_step=bq)
    x1p, h2p = _outproj(xp, mix_a.reshape(bp * sp, ATTN_WIDTH), mix_s, wo, lnf, tm=512)
    yp = _ffn(x1p, h2p, wup, wdn, tm=1024)

    ws_s, bs_s = sgu_params(ss)
    qs, ks, vs, mix_ss, svs = _proj(xs, lng, win, qg2, kg2, sg, ws_s, bs_s, gs,
                                    tm=512, chunk=ss, emit_sv=True)
    bias_s = _bias(rel_bias_table, *_band_tables(ss, 1))
    seqs = 4
    pad = jnp.zeros((bs, BAND - WINDOW - ss, KV_WIDTH), F32)
    k_all = jnp.concatenate([cache_attn_k[l].reshape(bs, WINDOW, KV_WIDTH),
                             ks.reshape(bs, ss, KV_WIDTH), pad], axis=1)
    v_all = jnp.concatenate([cache_attn_v[l].reshape(bs, WINDOW, KV_WIDTH),
                             vs.reshape(bs, ss, KV_WIDTH), pad], axis=1)
    (mix_as,) = _attn(qs.reshape(bs // seqs, seqs * ss, ATTN_WIDTH),
                      [(k_all.reshape(bs // seqs, seqs * BAND, KV_WIDTH), seqs * BAND)],
                      [(v_all.reshape(bs // seqs, seqs * BAND, KV_WIDTH), seqs * BAND)],
                      [cur], bias_s, *_sink_tables(sinks, ss), ga,
                      bq=ss, n_units=seqs, k_step=BAND)
    x1s, h2s = _outproj(xs, mix_as.reshape(bs * ss, ATTN_WIDTH), mix_ss, wo, lnf, tm=512)
    ys = _ffn(x1s, h2s, wup, wdn, tm=1024)

    keep = min(WINDOW, sp)
    kv_shape = (N_KV_HEADS, HEAD_DIM)
    return (yp.reshape(bp, sp, D_MODEL),
            ys.reshape(bs, ss, D_MODEL),
            k3[:, -keep:].reshape(1, bp, keep, *kv_shape),
            v3[:, -keep:].reshape(1, bp, keep, *kv_shape),
            ks.reshape(1, bs, ss, *kv_shape),
            vs.reshape(1, bs, ss, *kv_shape),
            svs.reshape(1, bs, ss, SGU_WIDTH))
```

```python
import functools
import math

import jax
import jax.numpy as jnp
from jax import lax
from jax.experimental import pallas as pl
from jax.experimental.pallas import tpu as pltpu

D_MODEL = 2048
CHUNK = 64
ATTN_WIDTH = 1024
SGU_WIDTH = 1024
HEAD_DIM = 64
N_HEADS = ATTN_WIDTH // HEAD_DIM
N_KV_HEADS = 2
GQA_GROUP = N_HEADS // N_KV_HEADS
WINDOW = 128
N_BUCKETS = 32
MAX_DISTANCE = 128
SGU_CHUNK = 128
SGU_GROUPS = 8
SGU_GROUP_CH = SGU_WIDTH // SGU_GROUPS
D_FF = 4 * D_MODEL
KV_WIDTH = N_KV_HEADS * HEAD_DIM
IN_WIDTH = ATTN_WIDTH + 2 * KV_WIDTH + 2 * SGU_WIDTH
EPS = 1e-6
NEG_INF = -1e30

LANES = 128
VMEM_LIMIT = 56 * 1024 * 1024
FFN_VMEM_LIMIT = 62 * 1024 * 1024
FFN_TF = 1024
FFN_SUB = 512

BF16 = jnp.bfloat16
F32 = jnp.float32

_KV0 = ATTN_WIDTH
_U0 = ATTN_WIDTH + 2 * KV_WIDTH
_V0 = _U0 + SGU_WIDTH


def _rms_rows(x, gain):
    ms = jnp.mean(x * x, axis=-1, keepdims=True)
    return x * lax.rsqrt(ms + EPS) * gain


def _head_pair_rms(blk, gain2, lane_lo):
    sq = blk * blk
    lo = jnp.sum(jnp.where(lane_lo, sq, 0.0), axis=-1, keepdims=True)
    hi = jnp.sum(jnp.where(lane_lo, 0.0, sq), axis=-1, keepdims=True)
    inv = 1.0 / HEAD_DIM
    r = jnp.where(lane_lo, lax.rsqrt(lo * inv + EPS), lax.rsqrt(hi * inv + EPS))
    return blk * r * gain2


def _cast_ffn_weights(wup_ref, wdn_ref, wupb_ref, wdnb_ref):
    tf = wupb_ref.shape[2]
    for j in range(wupb_ref.shape[0]):
        wupb_ref[j] = wup_ref[:, j * tf:(j + 1) * tf].astype(BF16)
    wdnb_ref[...] = wdn_ref[...].astype(BF16)


def _proj_kernel(chunk, emit_sv, cast_w, x_ref, lng_ref, win_ref, qg_ref, kg_ref, sg_ref,
                 ws_ref, bs_ref, gs_ref, *rest):
    if cast_w:
        _cast_ffn_weights(rest[0], rest[1], rest[-3], rest[-2])
        rest = rest[2:-3] + rest[-1:]
    q_ref, k_ref, v_ref, mixs_ref = rest[:4]
    so_scr = rest[-1]
    tm = x_ref.shape[0]
    h = _rms_rows(x_ref[...], lng_ref[...]).astype(BF16)
    lane_lo = lax.broadcasted_iota(jnp.int32, (tm, LANES), 1) < HEAD_DIM

    av = jax.nn.gelu(jnp.dot(h, win_ref[:, _V0:IN_WIDTH], preferred_element_type=F32))
    sv = _rms_rows(av, sg_ref[...])
    if emit_sv:
        rest[4][...] = sv
    svb = sv.astype(BF16)
    u = jax.nn.gelu(jnp.dot(h, win_ref[:, _U0:_V0], preferred_element_type=F32))

    zq = jnp.dot(h, win_ref[:, 0:ATTN_WIDTH], preferred_element_type=F32)
    for c in range(ATTN_WIDTH // LANES):
        sl = slice(c * LANES, (c + 1) * LANES)
        q_ref[:, sl] = (_head_pair_rms(zq[:, sl], qg_ref[...], lane_lo)
                        * (HEAD_DIM ** -0.5)).astype(BF16)

    row_c = lax.broadcasted_iota(jnp.int32, (chunk, chunk), 0) // CHUNK
    col_c = lax.broadcasted_iota(jnp.int32, (chunk, chunk), 1) // CHUNK
    n_chunks = tm // chunk
    for g in range(SGU_GROUPS):
        cs = slice(g * SGU_GROUP_CH, (g + 1) * SGU_GROUP_CH)
        w = jnp.where(col_c <= row_c, ws_ref[g], 0.0).astype(BF16)
        rhs = jnp.concatenate(
            [svb[n * chunk:(n + 1) * chunk, cs] for n in range(n_chunks)], axis=1)
        sp = jnp.dot(w, rhs, preferred_element_type=F32)
        for n in range(n_chunks):
            rs = slice(n * chunk, (n + 1) * chunk)
            so_scr[rs, cs] = u[rs, cs] * (sp[:, n * LANES:(n + 1) * LANES] + bs_ref[g])

    zkv = jnp.dot(h, win_ref[:, _KV0:_U0], preferred_element_type=F32)
    k_ref[...] = _head_pair_rms(zkv[:, 0:KV_WIDTH], kg_ref[...], lane_lo)
    v_ref[...] = zkv[:, KV_WIDTH:2 * KV_WIDTH]
    mixs_ref[...] = _rms_rows(so_scr[...], gs_ref[...]).astype(BF16)


def _proj(x2d, lng, win, qg2, kg2, sg, ws, bsb, gs, wup=None, wdn=None, *, tm, chunk, emit_sv,
          tf=None):
    t = x2d.shape[0]
    const = lambda shape: pl.BlockSpec(shape, lambda i: (0,) * len(shape),
                                       pipeline_mode=pl.Buffered(1))
    row = lambda w: pl.BlockSpec((tm, w), lambda i: (i, 0))
    out_shape = [jax.ShapeDtypeStruct((t, ATTN_WIDTH), BF16),
                 jax.ShapeDtypeStruct((t, KV_WIDTH), F32),
                 jax.ShapeDtypeStruct((t, KV_WIDTH), F32),
                 jax.ShapeDtypeStruct((t, SGU_WIDTH), BF16)]
    out_specs = [row(ATTN_WIDTH), row(KV_WIDTH), row(KV_WIDTH), row(SGU_WIDTH)]
    if emit_sv:
        out_shape.append(jax.ShapeDtypeStruct((t, SGU_WIDTH), F32))
        out_specs.append(row(SGU_WIDTH))
    in_specs = [row(D_MODEL), const((1, D_MODEL)), const((D_MODEL, IN_WIDTH)),
                const((1, LANES)), const((1, LANES)), const((1, SGU_WIDTH)),
                const((SGU_GROUPS, chunk, chunk)), const((SGU_GROUPS, chunk, LANES)),
                const((1, SGU_WIDTH))]
    args = [x2d, lng, win, qg2, kg2, sg, ws, bsb, gs]
    if wup is not None:
        n_steps = t // tm
        up_rows, dn_rows = D_MODEL // n_steps, D_FF // n_steps
        in_specs += [pl.BlockSpec((up_rows, D_FF), lambda i: (i, 0)),
                     pl.BlockSpec((dn_rows, D_MODEL), lambda i: (i, 0))]
        out_specs += [pl.BlockSpec((D_FF // tf, up_rows, tf), lambda i: (0, i, 0)),
                      pl.BlockSpec((dn_rows, D_MODEL), lambda i: (i, 0))]
        out_shape += [jax.ShapeDtypeStruct((D_FF // tf, D_MODEL, tf), BF16),
                      jax.ShapeDtypeStruct((D_FF, D_MODEL), BF16)]
        args += [wup, wdn]
    return pl.pallas_call(
        functools.partial(_proj_kernel, chunk, emit_sv, wup is not None),
        grid=(t // tm,),
        in_specs=in_specs,
        out_specs=out_specs,
        out_shape=out_shape,
        scratch_shapes=[pltpu.VMEM((tm, SGU_WIDTH), F32)],
        compiler_params=pltpu.CompilerParams(dimension_semantics=("parallel",),
                                             vmem_limit_bytes=VMEM_LIMIT),
        name="proj_c%d" % chunk,
    )(*args)


BAND = 2 * LANES
PAIRS = GQA_GROUP // 2


def _bias_kernel(bq, tab_ref, bkt_ref, valid_ref, out_ref):
    bkt = bkt_ref[...]
    for h in range(N_HEADS):
        g, pr, half = h // GQA_GROUP, (h % GQA_GROUP) // 2, h % 2
        acc = jnp.zeros(bkt.shape, F32)
        for b in range(N_BUCKETS):
            acc = jnp.where(bkt == b, tab_ref[b, h], acc)
        for s in range(valid_ref.shape[0]):
            out_ref[s, g, pr * bq:(pr + 1) * bq, half * BAND:(half + 1) * BAND] = (
                jnp.where(valid_ref[s] != 0, acc, NEG_INF))


def _bias(table, bucket, valid):
    nsel, bq, _ = valid.shape
    return pl.pallas_call(
        functools.partial(_bias_kernel, bq),
        in_specs=[pl.BlockSpec(memory_space=pltpu.SMEM),
                  pl.BlockSpec(memory_space=pltpu.VMEM),
                  pl.BlockSpec(memory_space=pltpu.VMEM)],
        out_specs=pl.BlockSpec(memory_space=pltpu.VMEM),
        out_shape=jax.ShapeDtypeStruct((nsel, N_KV_HEADS, PAIRS * bq, 2 * BAND), F32),
        name="bias_q%d" % bq,
    )(table, bucket, valid)


def _t5_bucket(n):
    half = N_BUCKETS // 2
    max_exact = half // 2
    offset = jnp.where(n < 0, half, 0)
    a = jnp.abs(n)
    af = jnp.maximum(a, 1).astype(F32)
    large = max_exact + (jnp.log(af / max_exact) / math.log(MAX_DISTANCE / max_exact)
                         * (half - max_exact)).astype(jnp.int32)
    large = jnp.minimum(large, half - 1)
    return offset + jnp.where(a < max_exact, a, large)


def _attn_kernel(bq, n_units, k_step, nsel, n_kparts, cast_wo, *refs):
    q_ref = refs[0]
    k_refs = refs[1:1 + n_kparts]
    v_refs = refs[1 + n_kparts:1 + 2 * n_kparts]
    rest = refs[1 + 2 * n_kparts:]
    if cast_wo:
        bias_ref, smax_ref, sl_ref, ga_ref, wo_ref, o_ref, wob_ref, kk_scr, vv_scr, ao_scr = rest
        wob_ref[...] = wo_ref[...].astype(BF16)
    else:
        bias_ref, smax_ref, sl_ref, ga_ref, o_ref, kk_scr, vv_scr, ao_scr = rest
    rows = PAIRS * bq

    def split_heads(parts, scr):
        x = jnp.concatenate([r[...] for r in parts], axis=0) if len(parts) > 1 else parts[0][...]
        lo = lax.broadcasted_iota(jnp.int32, x.shape, 1) < HEAD_DIM
        xr = pltpu.roll(x, HEAD_DIM, 1)
        scr[0, 0] = jnp.where(lo, x, 0.0).astype(BF16)
        scr[0, 1] = jnp.where(lo, 0.0, xr).astype(BF16)
        scr[1, 0] = jnp.where(lo, xr, 0.0).astype(BF16)
        scr[1, 1] = jnp.where(lo, 0.0, x).astype(BF16)

    split_heads(k_refs, kk_scr)
    split_heads(v_refs, vv_scr)
    r_i = lax.broadcasted_iota(jnp.int32, (2 * BAND, LANES), 0)
    l_i = lax.broadcasted_iota(jnp.int32, (2 * BAND, LANES), 1)
    ones_blk = jnp.where((r_i < BAND) == (l_i < HEAD_DIM), 1.0, 0.0).astype(BF16)
    lane_lo = lax.broadcasted_iota(jnp.int32, (rows, LANES), 1) < HEAD_DIM
    first_block = pl.program_id(1) == 0

    def unit(u, carry):
        q0 = pl.multiple_of(u * bq, bq)
        k0 = pl.multiple_of(u * k_step, k_step)
        sel = jnp.where(first_block & (u == 0), 0, 1) if nsel > 1 else 0
        for g in range(N_KV_HEADS):
            cols = [slice((PAIRS * g + p) * LANES, (PAIRS * g + p + 1) * LANES)
                    for p in range(PAIRS)]
            qs = jnp.concatenate([q_ref[pl.ds(q0, bq), c] for c in cols], axis=0)
            k2 = jnp.concatenate([kk_scr[g, 0, pl.ds(k0, BAND), :],
                                  kk_scr[g, 1, pl.ds(k0, BAND), :]], axis=0)
            s = lax.dot_general(qs, k2, (((1,), (1,)), ((), ())),
                                preferred_element_type=F32) + bias_ref[sel, g]
            sa, sb = s[:, :BAND], s[:, BAND:]
            ma = jnp.max(jnp.maximum(jnp.maximum(sa[:, :LANES], sa[:, LANES:]), smax_ref[g, 0]),
                         axis=-1, keepdims=True)
            mb = jnp.max(jnp.maximum(jnp.maximum(sb[:, :LANES], sb[:, LANES:]), smax_ref[g, 1]),
                         axis=-1, keepdims=True)
            p = jnp.concatenate([jnp.exp(sa - ma), jnp.exp(sb - mb)], axis=1).astype(BF16)
            v2 = jnp.concatenate([vv_scr[g, 0, pl.ds(k0, BAND), :],
                                  vv_scr[g, 1, pl.ds(k0, BAND), :]], axis=0)
            ol = jnp.dot(p, jnp.concatenate([v2, ones_blk], axis=1), preferred_element_type=F32)
            denom = ol[:, LANES:] + jnp.exp(sl_ref[g] - jnp.where(lane_lo, ma, mb))
            out = ol[:, :LANES] / denom
            for p_i, c in enumerate(cols):
                ao_scr[pl.ds(q0, bq), c] = out[p_i * bq:(p_i + 1) * bq]
        return carry

    lax.fori_loop(0, n_units, unit, 0, unroll=2)
    o_ref[...] = _rms_rows(ao_scr[...], ga_ref[...]).astype(BF16)


def _attn(q, k_parts, v_parts, k_maps, bias, smax, sl, ga, wo=None, *, bq, n_units, k_step):
    nb, s, _ = q.shape
    nsel = bias.shape[0]
    tq = n_units * bq
    k_rows = [rows for _, rows in k_parts]
    kr = sum(k_rows)
    cur = lambda b, i: (b, i, 0)
    full = lambda a: pl.BlockSpec(a.shape, lambda b, i: (0,) * a.ndim,
                                  pipeline_mode=pl.Buffered(1))
    kv_specs = [pl.BlockSpec((None, r, KV_WIDTH), m) for r, m in zip(k_rows, k_maps)]
    n_steps = nb * (s // tq)
    out_specs = [pl.BlockSpec((None, tq, ATTN_WIDTH), cur)]
    out_shape = [jax.ShapeDtypeStruct(q.shape, BF16)]
    extra_in, extra_specs = [], []
    if wo is not None:
        slab = pl.BlockSpec((wo.shape[0] // n_steps, wo.shape[1]),
                            lambda b, i: (b * (s // tq) + i, 0))
        extra_in, extra_specs = [wo], [slab]
        out_specs.append(slab)
        out_shape.append(jax.ShapeDtypeStruct(wo.shape, BF16))
    return pl.pallas_call(
        functools.partial(_attn_kernel, bq, n_units, k_step, nsel, len(k_parts), wo is not None),
        grid=(nb, s // tq),
        in_specs=([pl.BlockSpec((None, tq, ATTN_WIDTH), cur)] + kv_specs + kv_specs
                  + [full(bias), full(smax), full(sl), full(ga)] + extra_specs),
        out_specs=out_specs,
        out_shape=out_shape,
        scratch_shapes=[pltpu.VMEM((N_KV_HEADS, 2, kr, KV_WIDTH), BF16),
                        pltpu.VMEM((N_KV_HEADS, 2, kr, KV_WIDTH), BF16),
                        pltpu.VMEM((tq, ATTN_WIDTH), F32)],
        compiler_params=pltpu.CompilerParams(dimension_semantics=("parallel", "arbitrary"),
                                             vmem_limit_bytes=VMEM_LIMIT),
        name="attn_q%d" % bq,
    )(q, *[a for a, _ in k_parts], *[a for a, _ in v_parts], bias, smax, sl, ga, *extra_in)


def _outproj_kernel(x_ref, ma_ref, ms_ref, wo_ref, g_ref, o_ref, h_ref):
    acc = jnp.dot(ma_ref[...], wo_ref[0:ATTN_WIDTH, :], preferred_element_type=F32)
    acc = acc + jnp.dot(ms_ref[...], wo_ref[ATTN_WIDTH:, :], preferred_element_type=F32)
    x1 = x_ref[...] + acc
    o_ref[...] = x1
    h_ref[...] = _rms_rows(x1, g_ref[...]).astype(BF16)


def _outproj(x2d, mix_a, mix_s, wo, g, *, tm):
    t = x2d.shape[0]
    row = lambda w: pl.BlockSpec((tm, w), lambda i: (i, 0))
    const = lambda a: pl.BlockSpec(a.shape, lambda i: (0,) * a.ndim, pipeline_mode=pl.Buffered(1))
    return pl.pallas_call(
        _outproj_kernel,
        grid=(t // tm,),
        in_specs=[row(D_MODEL), row(ATTN_WIDTH), row(SGU_WIDTH), const(wo), const(g)],
        out_specs=[row(D_MODEL), row(D_MODEL)],
        out_shape=[jax.ShapeDtypeStruct((t, D_MODEL), F32),
                   jax.ShapeDtypeStruct((t, D_MODEL), BF16)],
        compiler_params=pltpu.CompilerParams(dimension_semantics=("parallel",),
                                             vmem_limit_bytes=VMEM_LIMIT),
        name="outproj",
    )(x2d, mix_a, mix_s, wo, g)


def _ffn_kernel(x_ref, h_ref, wup_ref, wdn_ref, y_ref):
    tf = wup_ref.shape[1]
    for s in range(tf // FFN_SUB):
        cs = slice(s * FFN_SUB, (s + 1) * FFN_SUB)
        a = jnp.dot(h_ref[...], wup_ref[:, cs], preferred_element_type=F32)
        a = jnp.square(jnp.maximum(a, 0.0)).astype(BF16)
        if s == 0:
            base = jnp.where(pl.program_id(1) == 0, x_ref[...], y_ref[...])
        else:
            base = y_ref[...]
        y_ref[...] = base + jnp.dot(a, wdn_ref[cs, :], preferred_element_type=F32)


def _ffn(x2d, h2d, wup_t, wdn, *, tm):
    t = x2d.shape[0]
    n_f, _, tf = wup_t.shape
    return pl.pallas_call(
        _ffn_kernel,
        grid=(t // tm, n_f),
        in_specs=[pl.BlockSpec((tm, D_MODEL), lambda i, j: (i, 0)),
                  pl.BlockSpec((tm, D_MODEL), lambda i, j: (i, 0)),
                  pl.BlockSpec((None, D_MODEL, tf), lambda i, j: (j, 0, 0)),
                  pl.BlockSpec((tf, D_MODEL), lambda i, j: (j, 0))],
        out_specs=pl.BlockSpec((tm, D_MODEL), lambda i, j: (i, 0)),
        out_shape=jax.ShapeDtypeStruct((t, D_MODEL), F32),
        compiler_params=pltpu.CompilerParams(dimension_semantics=("parallel", "arbitrary"),
                                             vmem_limit_bytes=FFN_VMEM_LIMIT),
        name="ffn",
    )(x2d, h2d, wup_t, wdn)


def _band_tables(bq, nsel):
    qi = jnp.arange(bq)[:, None]
    kj = jnp.arange(BAND)[None, :] - WINDOW
    qc, kc = qi // CHUNK, jnp.floor_divide(kj, CHUNK)
    in_band = (kc <= qc) & (kc >= qc - WINDOW // CHUNK)
    valid = [in_band & (kj >= 0)] if nsel == 2 else []
    valid.append(in_band)
    return _t5_bucket(qi - kj), jnp.stack(valid).astype(jnp.int32)


def _sink_tables(sinks, bq):
    s3 = sinks.reshape(N_KV_HEADS, PAIRS, 2)
    rows = PAIRS * bq
    smax = jnp.broadcast_to(jnp.transpose(s3, (0, 2, 1))[:, :, :, None, None],
                            (N_KV_HEADS, 2, PAIRS, bq, LANES)).reshape(N_KV_HEADS, 2, rows, LANES)
    sl = jnp.broadcast_to(jnp.repeat(s3, HEAD_DIM, axis=-1)[:, :, None, :],
                          (N_KV_HEADS, PAIRS, bq, LANES)).reshape(N_KV_HEADS, rows, LANES)
    return smax, sl


def kernel(x_prompt, x_sample, cache_attn_k, cache_attn_v, rel_bias_table, ln_mix_g, w_in,
           q_norm_g, k_norm_g, attn_sinks, sgu_norm_g, sgu_w, sgu_b, out_norm_attn_g,
           out_norm_sgu_g, w_out, ln_ffn_g, w_ffn_up, w_ffn_down):
    bp, sp, _ = x_prompt.shape
    bs, ss, _ = x_sample.shape
    depth = w_in.shape[0]
    assert depth == 1 and ss == CHUNK and cache_attn_k.shape[2] == WINDOW
    l = 0
    xp = x_prompt.reshape(bp * sp, D_MODEL)
    xs = x_sample.reshape(bs * ss, D_MODEL)

    win = w_in[l].astype(BF16)
    lng = ln_mix_g[l].reshape(1, D_MODEL)
    qg2 = jnp.tile(q_norm_g[l], 2).reshape(1, LANES)
    kg2 = jnp.tile(k_norm_g[l], 2).reshape(1, LANES)
    sg = sgu_norm_g[l].reshape(1, SGU_WIDTH)
    gs = out_norm_sgu_g[l].reshape(1, SGU_WIDTH)
    ga = out_norm_attn_g[l].reshape(1, ATTN_WIDTH)
    lnf = ln_ffn_g[l].reshape(1, D_MODEL)
    sinks = attn_sinks[l].reshape(N_HEADS)

    def sgu_params(c):
        return (sgu_w[l][:, :c, :c],
                jnp.broadcast_to(sgu_b[l][:, :c, None], (SGU_GROUPS, c, LANES)))

    ws_p, bs_p = sgu_params(SGU_CHUNK)
    q, k, v, mix_s, wup, wdn = _proj(xp, lng, win, qg2, kg2, sg, ws_p, bs_p, gs,
                                     w_ffn_up[l], w_ffn_down[l],
                                     tm=512, chunk=SGU_CHUNK, emit_sv=False, tf=FFN_TF)
    bq, units = 2 * CHUNK, 4
    bias_p = _bias(rel_bias_table, *_band_tables(bq, 2))
    k3 = k.reshape(bp, sp, KV_WIDTH)
    v3 = v.reshape(bp, sp, KV_WIDTH)
    prev = lambda b, i: (b, jnp.maximum(i * units - 1, 0), 0)
    cur = lambda b, i: (b, i, 0)
    mix_a, wo = _attn(q.reshape(bp, sp, ATTN_WIDTH),
                      [(k3, WINDOW), (k3, units * bq)], [(v3, WINDOW), (v3, units * bq)],
                      [prev, cur], bias_p, *_sink_tables(sinks, bq), ga, w_out[l],
                      bq=bq, n_units=units, k_step=bq)
    x1p, h2p = _outproj(xp, mix_a.reshape(bp * sp, ATTN_WIDTH), mix_s, wo, lnf, tm=512)
    yp = _ffn(x1p, h2p, wup, wdn, tm=1024)

    ws_s, bs_s = sgu_params(ss)
    qs, ks, vs, mix_ss, svs = _proj(xs, lng, win, qg2, kg2, sg, ws_s, bs_s, gs,
                                    tm=512, chunk=ss, emit_sv=True)
    bias_s = _bias(rel_bias_table, *_band_tables(ss, 1))
    seqs = 4
    pad = jnp.zeros((bs, BAND - WINDOW - ss, KV_WIDTH), F32)
    k_all = jnp.concatenate([cache_attn_k[l].reshape(bs, WINDOW, KV_WIDTH),
                             ks.reshape(bs, ss, KV_WIDTH), pad], axis=1)
    v_all = jnp.concatenate([cache_attn_v[l].reshape(bs, WINDOW, KV_WIDTH),
                             vs.reshape(bs, ss, KV_WIDTH), pad], axis=1)
    (mix_as,) = _attn(qs.reshape(bs // seqs, seqs * ss, ATTN_WIDTH),
                      [(k_all.reshape(bs // seqs, seqs * BAND, KV_WIDTH), seqs * BAND)],
                      [(v_all.reshape(bs // seqs, seqs * BAND, KV_WIDTH), seqs * BAND)],
                      [cur], bias_s, *_sink_tables(sinks, ss), ga,
                      bq=ss, n_units=seqs, k_step=BAND)
    x1s, h2s = _outproj(xs, mix_as.reshape(bs * ss, ATTN_WIDTH), mix_ss, wo, lnf, tm=512)
    ys = _ffn(x1s, h2s, wup, wdn, tm=1024)

    keep = min(WINDOW, sp)
    kv_shape = (N_KV_HEADS, HEAD_DIM)
    return (yp.reshape(bp, sp, D_MODEL),
            ys.reshape(bs, ss, D_MODEL),
            k3[:, -keep:].reshape(1, bp, keep, *kv_shape),
            v3[:, -keep:].reshape(1, bp, keep, *kv_shape),
            ks.reshape(1, bs, ss, *kv_shape),
            vs.reshape(1, bs, ss, *kv_shape),
            svs.reshape(1, bs, ss, SGU_WIDTH))
```

```python
import functools
import math

import jax
import jax.numpy as jnp
from jax import lax
from jax.experimental import pallas as pl
from jax.experimental.pallas import tpu as pltpu

D_MODEL = 2048
CHUNK = 64
ATTN_WIDTH = 1024
SGU_WIDTH = 1024
HEAD_DIM = 64
N_HEADS = ATTN_WIDTH // HEAD_DIM
N_KV_HEADS = 2
GQA_GROUP = N_HEADS // N_KV_HEADS
WINDOW = 128
N_BUCKETS = 32
MAX_DISTANCE = 128
SGU_CHUNK = 128
SGU_GROUPS = 8
SGU_GROUP_CH = SGU_WIDTH // SGU_GROUPS
D_FF = 4 * D_MODEL
KV_WIDTH = N_KV_HEADS * HEAD_DIM
IN_WIDTH = ATTN_WIDTH + 2 * KV_WIDTH + 2 * SGU_WIDTH
EPS = 1e-6
NEG_INF = -1e30

LANES = 128
VMEM_LIMIT = 56 * 1024 * 1024
FFN_VMEM_LIMIT = 62 * 1024 * 1024
FFN_TF = 1024
FFN_SUB = 512

BF16 = jnp.bfloat16
F32 = jnp.float32

_KV0 = ATTN_WIDTH
_U0 = ATTN_WIDTH + 2 * KV_WIDTH
_V0 = _U0 + SGU_WIDTH


def _rms_rows(x, gain):
    ms = jnp.mean(x * x, axis=-1, keepdims=True)
    return x * lax.rsqrt(ms + EPS) * gain


def _head_pair_rms(blk, gain2, lane_lo):
    sq = blk * blk
    lo = jnp.sum(jnp.where(lane_lo, sq, 0.0), axis=-1, keepdims=True)
    hi = jnp.sum(jnp.where(lane_lo, 0.0, sq), axis=-1, keepdims=True)
    inv = 1.0 / HEAD_DIM
    r = jnp.where(lane_lo, lax.rsqrt(lo * inv + EPS), lax.rsqrt(hi * inv + EPS))
    return blk * r * gain2


def _cast_weights(wup_ref, wdn_ref, wo_ref, wupb_ref, wdnb_ref, wob_ref):
    tf = wupb_ref.shape[2]
    for j in range(wupb_ref.shape[0]):
        wupb_ref[j] = wup_ref[:, j * tf:(j + 1) * tf].astype(BF16)
    wdnb_ref[...] = wdn_ref[...].astype(BF16)
    wob_ref[...] = wo_ref[...].astype(BF16)


def _proj_kernel(chunk, emit_sv, cast_w, x_ref, lng_ref, win_ref, qg_ref, kg_ref, sg_ref,
                 ws_ref, bs_ref, gs_ref, *rest):
    if cast_w:
        _cast_weights(*rest[:3], *rest[-4:-1])
        rest = rest[3:-4] + rest[-1:]
    q_ref, k_ref, v_ref, mixs_ref = rest[:4]
    so_scr = rest[-1]
    tm = x_ref.shape[0]
    h = _rms_rows(x_ref[...], lng_ref[...]).astype(BF16)
    lane_lo = lax.broadcasted_iota(jnp.int32, (tm, LANES), 1) < HEAD_DIM

    av = jax.nn.gelu(jnp.dot(h, win_ref[:, _V0:IN_WIDTH], preferred_element_type=F32))
    sv = _rms_rows(av, sg_ref[...])
    if emit_sv:
        rest[4][...] = sv
    svb = sv.astype(BF16)
    u = jax.nn.gelu(jnp.dot(h, win_ref[:, _U0:_V0], preferred_element_type=F32))

    zq = jnp.dot(h, win_ref[:, 0:ATTN_WIDTH], preferred_element_type=F32)
    for c in range(ATTN_WIDTH // LANES):
        sl = slice(c * LANES, (c + 1) * LANES)
        q_ref[:, sl] = (_head_pair_rms(zq[:, sl], qg_ref[...], lane_lo)
                        * (HEAD_DIM ** -0.5)).astype(BF16)

    row_c = lax.broadcasted_iota(jnp.int32, (chunk, chunk), 0) // CHUNK
    col_c = lax.broadcasted_iota(jnp.int32, (chunk, chunk), 1) // CHUNK
    n_chunks = tm // chunk
    for g in range(SGU_GROUPS):
        cs = slice(g * SGU_GROUP_CH, (g + 1) * SGU_GROUP_CH)
        w = jnp.where(col_c <= row_c, ws_ref[g], 0.0).astype(BF16)
        rhs = jnp.concatenate(
            [svb[n * chunk:(n + 1) * chunk, cs] for n in range(n_chunks)], axis=1)
        sp = jnp.dot(w, rhs, preferred_element_type=F32)
        for n in range(n_chunks):
            rs = slice(n * chunk, (n + 1) * chunk)
            so_scr[rs, cs] = u[rs, cs] * (sp[:, n * LANES:(n + 1) * LANES] + bs_ref[g])

    zkv = jnp.dot(h, win_ref[:, _KV0:_U0], preferred_element_type=F32)
    k_ref[...] = _head_pair_rms(zkv[:, 0:KV_WIDTH], kg_ref[...], lane_lo)
    v_ref[...] = zkv[:, KV_WIDTH:2 * KV_WIDTH]
    mixs_ref[...] = _rms_rows(so_scr[...], gs_ref[...]).astype(BF16)


def _proj(x2d, lng, win, qg2, kg2, sg, ws, bsb, gs, wup=None, wdn=None, wo=None, *, tm, chunk,
          emit_sv, tf=None):
    t = x2d.shape[0]
    const = lambda shape: pl.BlockSpec(shape, lambda i: (0,) * len(shape),
                                       pipeline_mode=pl.Buffered(1))
    row = lambda w: pl.BlockSpec((tm, w), lambda i: (i, 0))
    out_shape = [jax.ShapeDtypeStruct((t, ATTN_WIDTH), BF16),
                 jax.ShapeDtypeStruct((t, KV_WIDTH), F32),
                 jax.ShapeDtypeStruct((t, KV_WIDTH), F32),
                 jax.ShapeDtypeStruct((t, SGU_WIDTH), BF16)]
    out_specs = [row(ATTN_WIDTH), row(KV_WIDTH), row(KV_WIDTH), row(SGU_WIDTH)]
    if emit_sv:
        out_shape.append(jax.ShapeDtypeStruct((t, SGU_WIDTH), F32))
        out_specs.append(row(SGU_WIDTH))
    in_specs = [row(D_MODEL), const((1, D_MODEL)), const((D_MODEL, IN_WIDTH)),
                const((1, LANES)), const((1, LANES)), const((1, SGU_WIDTH)),
                const((SGU_GROUPS, chunk, chunk)), const((SGU_GROUPS, chunk, LANES)),
                const((1, SGU_WIDTH))]
    args = [x2d, lng, win, qg2, kg2, sg, ws, bsb, gs]
    if wup is not None:
        n_steps = t // tm
        up_rows, dn_rows, wo_rows = D_MODEL // n_steps, D_FF // n_steps, wo.shape[0] // n_steps
        in_specs += [pl.BlockSpec((up_rows, D_FF), lambda i: (i, 0)),
                     pl.BlockSpec((dn_rows, D_MODEL), lambda i: (i, 0)),
                     pl.BlockSpec((wo_rows, D_MODEL), lambda i: (i, 0))]
        out_specs += [pl.BlockSpec((D_FF // tf, up_rows, tf), lambda i: (0, i, 0)),
                      pl.BlockSpec((dn_rows, D_MODEL), lambda i: (i, 0)),
                      pl.BlockSpec((wo_rows, D_MODEL), lambda i: (i, 0))]
        out_shape += [jax.ShapeDtypeStruct((D_FF // tf, D_MODEL, tf), BF16),
                      jax.ShapeDtypeStruct((D_FF, D_MODEL), BF16),
                      jax.ShapeDtypeStruct(wo.shape, BF16)]
        args += [wup, wdn, wo]
    return pl.pallas_call(
        functools.partial(_proj_kernel, chunk, emit_sv, wup is not None),
        grid=(t // tm,),
        in_specs=in_specs,
        out_specs=out_specs,
        out_shape=out_shape,
        scratch_shapes=[pltpu.VMEM((tm, SGU_WIDTH), F32)],
        compiler_params=pltpu.CompilerParams(dimension_semantics=("parallel",),
                                             vmem_limit_bytes=VMEM_LIMIT),
        name="proj_c%d" % chunk,
    )(*args)


BAND = 2 * LANES
PAIRS = GQA_GROUP // 2


def _bias_kernel(bq, tab_ref, bkt_ref, valid_ref, out_ref):
    bkt = bkt_ref[...]
    for h in range(N_HEADS):
        g, pr, half = h // GQA_GROUP, (h % GQA_GROUP) // 2, h % 2
        acc = jnp.zeros(bkt.shape, F32)
        for b in range(N_BUCKETS):
            acc = jnp.where(bkt == b, tab_ref[b, h], acc)
        for s in range(valid_ref.shape[0]):
            out_ref[s, g, pr * bq:(pr + 1) * bq, half * BAND:(half + 1) * BAND] = (
                jnp.where(valid_ref[s] != 0, acc, NEG_INF))


def _bias(table, bucket, valid):
    nsel, bq, _ = valid.shape
    return pl.pallas_call(
        functools.partial(_bias_kernel, bq),
        in_specs=[pl.BlockSpec(memory_space=pltpu.SMEM),
                  pl.BlockSpec(memory_space=pltpu.VMEM),
                  pl.BlockSpec(memory_space=pltpu.VMEM)],
        out_specs=pl.BlockSpec(memory_space=pltpu.VMEM),
        out_shape=jax.ShapeDtypeStruct((nsel, N_KV_HEADS, PAIRS * bq, 2 * BAND), F32),
        name="bias_q%d" % bq,
    )(table, bucket, valid)


def _t5_bucket(n):
    half = N_BUCKETS // 2
    max_exact = half // 2
    offset = jnp.where(n < 0, half, 0)
    a = jnp.abs(n)
    af = jnp.maximum(a, 1).astype(F32)
    large = max_exact + (jnp.log(af / max_exact) / math.log(MAX_DISTANCE / max_exact)
                         * (half - max_exact)).astype(jnp.int32)
    large = jnp.minimum(large, half - 1)
    return offset + jnp.where(a < max_exact, a, large)


def _attn_kernel(bq, n_units, k_step, nsel, n_kparts, *refs):
    q_ref = refs[0]
    k_refs = refs[1:1 + n_kparts]
    v_refs = refs[1 + n_kparts:1 + 2 * n_kparts]
    (bias_ref, smax_ref, sl_ref, ga_ref, x_ref, ms_ref, wo_ref, g_ref,
     o_ref, h_ref, kk_scr, vv_scr, ao_scr) = refs[1 + 2 * n_kparts:]
    rows = PAIRS * bq

    def split_heads(parts, scr):
        x = jnp.concatenate([r[...] for r in parts], axis=0) if len(parts) > 1 else parts[0][...]
        lo = lax.broadcasted_iota(jnp.int32, x.shape, 1) < HEAD_DIM
        xr = pltpu.roll(x, HEAD_DIM, 1)
        scr[0, 0] = jnp.where(lo, x, 0.0).astype(BF16)
        scr[0, 1] = jnp.where(lo, 0.0, xr).astype(BF16)
        scr[1, 0] = jnp.where(lo, xr, 0.0).astype(BF16)
        scr[1, 1] = jnp.where(lo, 0.0, x).astype(BF16)

    split_heads(k_refs, kk_scr)
    split_heads(v_refs, vv_scr)
    r_i = lax.broadcasted_iota(jnp.int32, (2 * BAND, LANES), 0)
    l_i = lax.broadcasted_iota(jnp.int32, (2 * BAND, LANES), 1)
    ones_blk = jnp.where((r_i < BAND) == (l_i < HEAD_DIM), 1.0, 0.0).astype(BF16)
    lane_lo = lax.broadcasted_iota(jnp.int32, (rows, LANES), 1) < HEAD_DIM
    first_block = pl.program_id(1) == 0

    for u in range(n_units):
        q0, k0 = u * bq, u * k_step
        sel = jnp.where(first_block, 0, 1) if (nsel > 1 and u == 0) else nsel - 1
        for g in range(N_KV_HEADS):
            cols = [slice((PAIRS * g + p) * LANES, (PAIRS * g + p + 1) * LANES)
                    for p in range(PAIRS)]
            qs = jnp.concatenate([q_ref[q0:q0 + bq, c] for c in cols], axis=0)
            k2 = jnp.concatenate([kk_scr[g, 0, k0:k0 + BAND, :],
                                  kk_scr[g, 1, k0:k0 + BAND, :]], axis=0)
            s = lax.dot_general(qs, k2, (((1,), (1,)), ((), ())),
                                preferred_element_type=F32) + bias_ref[sel, g]
            sa, sb = s[:, :BAND], s[:, BAND:]
            ma = jnp.max(jnp.maximum(jnp.maximum(sa[:, :LANES], sa[:, LANES:]), smax_ref[g, 0]),
                         axis=-1, keepdims=True)
            mb = jnp.max(jnp.maximum(jnp.maximum(sb[:, :LANES], sb[:, LANES:]), smax_ref[g, 1]),
                         axis=-1, keepdims=True)
            p = jnp.concatenate([jnp.exp(sa - ma), jnp.exp(sb - mb)], axis=1).astype(BF16)
            v2 = jnp.concatenate([vv_scr[g, 0, k0:k0 + BAND, :],
                                  vv_scr[g, 1, k0:k0 + BAND, :]], axis=0)
            ol = jnp.dot(p, jnp.concatenate([v2, ones_blk], axis=1), preferred_element_type=F32)
            denom = ol[:, LANES:] + jnp.exp(sl_ref[g] - jnp.where(lane_lo, ma, mb))
            out = ol[:, :LANES] / denom
            for p_i, c in enumerate(cols):
                ao_scr[q0:q0 + bq, c] = out[p_i * bq:(p_i + 1) * bq]

    mix_a = _rms_rows(ao_scr[...], ga_ref[...]).astype(BF16)
    acc = jnp.dot(ms_ref[...], wo_ref[ATTN_WIDTH:, :], preferred_element_type=F32)
    acc = acc + jnp.dot(mix_a, wo_ref[0:ATTN_WIDTH, :], preferred_element_type=F32)
    x1 = x_ref[...] + acc
    o_ref[...] = x1
    h_ref[...] = _rms_rows(x1, g_ref[...]).astype(BF16)


def _attn(q, k_parts, v_parts, k_maps, bias, smax, sl, ga, x, mix_s, wo, g, *, bq, n_units,
          k_step):
    nb, s, _ = q.shape
    nsel = bias.shape[0]
    tq = n_units * bq
    k_rows = [rows for _, rows in k_parts]
    kr = sum(k_rows)
    cur = lambda b, i: (b, i, 0)
    row = lambda w: pl.BlockSpec((None, tq, w), cur)
    full = lambda a: pl.BlockSpec(a.shape, lambda b, i: (0,) * a.ndim,
                                  pipeline_mode=pl.Buffered(1))
    kv_specs = [pl.BlockSpec((None, r, KV_WIDTH), m) for r, m in zip(k_rows, k_maps)]
    return pl.pallas_call(
        functools.partial(_attn_kernel, bq, n_units, k_step, nsel, len(k_parts)),
        grid=(nb, s // tq),
        in_specs=([row(ATTN_WIDTH)] + kv_specs + kv_specs
                  + [full(bias), full(smax), full(sl), full(ga),
                     row(D_MODEL), row(SGU_WIDTH), full(wo), full(g)]),
        out_specs=[row(D_MODEL), row(D_MODEL)],
        out_shape=[jax.ShapeDtypeStruct((nb, s, D_MODEL), F32),
                   jax.ShapeDtypeStruct((nb, s, D_MODEL), BF16)],
        scratch_shapes=[pltpu.VMEM((N_KV_HEADS, 2, kr, KV_WIDTH), BF16),
                        pltpu.VMEM((N_KV_HEADS, 2, kr, KV_WIDTH), BF16),
                        pltpu.VMEM((tq, ATTN_WIDTH), F32)],
        compiler_params=pltpu.CompilerParams(dimension_semantics=("parallel", "arbitrary"),
                                             vmem_limit_bytes=VMEM_LIMIT),
        name="attn_q%d" % bq,
    )(q, *[a for a, _ in k_parts], *[a for a, _ in v_parts], bias, smax, sl, ga, x, mix_s, wo, g)


def _ffn_kernel(x_ref, h_ref, wup_ref, wdn_ref, y_ref):
    tf = wup_ref.shape[1]
    for s in range(tf // FFN_SUB):
        cs = slice(s * FFN_SUB, (s + 1) * FFN_SUB)
        a = jnp.dot(h_ref[...], wup_ref[:, cs], preferred_element_type=F32)
        a = jnp.square(jnp.maximum(a, 0.0)).astype(BF16)
        if s == 0:
            base = jnp.where(pl.program_id(1) == 0, x_ref[...], y_ref[...])
        else:
            base = y_ref[...]
        y_ref[...] = base + jnp.dot(a, wdn_ref[cs, :], preferred_element_type=F32)


def _ffn(x2d, h2d, wup_t, wdn, *, tm):
    t = x2d.shape[0]
    n_f, _, tf = wup_t.shape
    return pl.pallas_call(
        _ffn_kernel,
        grid=(t // tm, n_f),
        in_specs=[pl.BlockSpec((tm, D_MODEL), lambda i, j: (i, 0)),
                  pl.BlockSpec((tm, D_MODEL), lambda i, j: (i, 0)),
                  pl.BlockSpec((None, D_MODEL, tf), lambda i, j: (j, 0, 0)),
                  pl.BlockSpec((tf, D_MODEL), lambda i, j: (j, 0))],
        out_specs=pl.BlockSpec((tm, D_MODEL), lambda i, j: (i, 0)),
        out_shape=jax.ShapeDtypeStruct((t, D_MODEL), F32),
        compiler_params=pltpu.CompilerParams(dimension_semantics=("parallel", "arbitrary"),
                                             vmem_limit_bytes=FFN_VMEM_LIMIT),
        name="ffn",
    )(x2d, h2d, wup_t, wdn)


def _band_tables(bq, nsel):
    qi = jnp.arange(bq)[:, None]
    kj = jnp.arange(BAND)[None, :] - WINDOW
    qc, kc = qi // CHUNK, jnp.floor_divide(kj, CHUNK)
    in_band = (kc <= qc) & (kc >= qc - WINDOW // CHUNK)
    valid = [in_band & (kj >= 0)] if nsel == 2 else []
    valid.append(in_band)
    return _t5_bucket(qi - kj), jnp.stack(valid).astype(jnp.int32)


def _sink_tables(sinks, bq):
    s3 = sinks.reshape(N_KV_HEADS, PAIRS, 2)
    rows = PAIRS * bq
    smax = jnp.broadcast_to(jnp.transpose(s3, (0, 2, 1))[:, :, :, None, None],
                            (N_KV_HEADS, 2, PAIRS, bq, LANES)).reshape(N_KV_HEADS, 2, rows, LANES)
    sl = jnp.broadcast_to(jnp.repeat(s3, HEAD_DIM, axis=-1)[:, :, None, :],
                          (N_KV_HEADS, PAIRS, bq, LANES)).reshape(N_KV_HEADS, rows, LANES)
    return smax, sl


def kernel(x_prompt, x_sample, cache_attn_k, cache_attn_v, rel_bias_table, ln_mix_g, w_in,
           q_norm_g, k_norm_g, attn_sinks, sgu_norm_g, sgu_w, sgu_b, out_norm_attn_g,
           out_norm_sgu_g, w_out, ln_ffn_g, w_ffn_up, w_ffn_down):
    bp, sp, _ = x_prompt.shape
    bs, ss, _ = x_sample.shape
    depth = w_in.shape[0]
    assert depth == 1 and ss == CHUNK and cache_attn_k.shape[2] == WINDOW
    l = 0
    xp = x_prompt.reshape(bp * sp, D_MODEL)
    xs = x_sample.reshape(bs * ss, D_MODEL)

    win = w_in[l].astype(BF16)
    lng = ln_mix_g[l].reshape(1, D_MODEL)
    qg2 = jnp.tile(q_norm_g[l], 2).reshape(1, LANES)
    kg2 = jnp.tile(k_norm_g[l], 2).reshape(1, LANES)
    sg = sgu_norm_g[l].reshape(1, SGU_WIDTH)
    gs = out_norm_sgu_g[l].reshape(1, SGU_WIDTH)
    ga = out_norm_attn_g[l].reshape(1, ATTN_WIDTH)
    lnf = ln_ffn_g[l].reshape(1, D_MODEL)
    sinks = attn_sinks[l].reshape(N_HEADS)

    def sgu_params(c):
        return (sgu_w[l][:, :c, :c],
                jnp.broadcast_to(sgu_b[l][:, :c, None], (SGU_GROUPS, c, LANES)))

    ws_p, bs_p = sgu_params(SGU_CHUNK)
    q, k, v, mix_s, wup, wdn, wo = _proj(xp, lng, win, qg2, kg2, sg, ws_p, bs_p, gs,
                                         w_ffn_up[l], w_ffn_down[l], w_out[l],
                                         tm=512, chunk=SGU_CHUNK, emit_sv=False, tf=FFN_TF)
    bq, units = 2 * CHUNK, 4
    bias_p = _bias(rel_bias_table, *_band_tables(bq, 2))
    k3 = k.reshape(bp, sp, KV_WIDTH)
    v3 = v.reshape(bp, sp, KV_WIDTH)
    prev = lambda b, i: (b, jnp.maximum(i * units - 1, 0), 0)
    cur = lambda b, i: (b, i, 0)
    x1p, h2p = _attn(q.reshape(bp, sp, ATTN_WIDTH),
                     [(k3, WINDOW), (k3, units * bq)], [(v3, WINDOW), (v3, units * bq)],
                     [prev, cur], bias_p, *_sink_tables(sinks, bq), ga,
                     x_prompt, mix_s.reshape(bp, sp, SGU_WIDTH), wo, lnf,
                     bq=bq, n_units=units, k_step=bq)
    yp = _ffn(x1p.reshape(bp * sp, D_MODEL), h2p.reshape(bp * sp, D_MODEL), wup, wdn, tm=1024)

    ws_s, bs_s = sgu_params(ss)
    qs, ks, vs, mix_ss, svs = _proj(xs, lng, win, qg2, kg2, sg, ws_s, bs_s, gs,
                                    tm=512, chunk=ss, emit_sv=True)
    bias_s = _bias(rel_bias_table, *_band_tables(ss, 1))
    seqs = 4
    pad = jnp.zeros((bs, BAND - WINDOW - ss, KV_WIDTH), F32)
    k_all = jnp.concatenate([cache_attn_k[l].reshape(bs, WINDOW, KV_WIDTH),
                             ks.reshape(bs, ss, KV_WIDTH), pad], axis=1)
    v_all = jnp.concatenate([cache_attn_v[l].reshape(bs, WINDOW, KV_WIDTH),
                             vs.reshape(bs, ss, KV_WIDTH), pad], axis=1)
    grp = lambda a, rows: a.reshape(bs // seqs, seqs * rows, a.shape[-1])
    x1s, h2s = _attn(grp(qs, ss), [(grp(k_all, BAND), seqs * BAND)],
                     [(grp(v_all, BAND), seqs * BAND)], [cur], bias_s,
                     *_sink_tables(sinks, ss), ga, grp(xs, ss), grp(mix_ss, ss), wo, lnf,
                     bq=ss, n_units=seqs, k_step=BAND)
    ys = _ffn(x1s.reshape(bs * ss, D_MODEL), h2s.reshape(bs * ss, D_MODEL), wup, wdn, tm=1024)

    keep = min(WINDOW, sp)
    kv_shape = (N_KV_HEADS, HEAD_DIM)
    return (yp.reshape(bp, sp, D_MODEL),
            ys.reshape(bs, ss, D_MODEL),
            k3[:, -keep:].reshape(1, bp, keep, *kv_shape),
            v3[:, -keep:].reshape(1, bp, keep, *kv_shape),
            ks.reshape(1, bs, ss, *kv_shape),
            vs.reshape(1, bs, ss, *kv_shape),
            svs.reshape(1, bs, ss, SGU_WIDTH))
```

```python
import functools
import math

import jax
import jax.numpy as jnp
from jax import lax
from jax.experimental import pallas as pl
from jax.experimental.pallas import tpu as pltpu

D_MODEL = 2048
CHUNK = 64
ATTN_WIDTH = 1024
SGU_WIDTH = 1024
HEAD_DIM = 64
N_HEADS = ATTN_WIDTH // HEAD_DIM
N_KV_HEADS = 2
GQA_GROUP = N_HEADS // N_KV_HEADS
WINDOW = 128
N_BUCKETS = 32
MAX_DISTANCE = 128
SGU_CHUNK = 128
SGU_GROUPS = 8
SGU_GROUP_CH = SGU_WIDTH // SGU_GROUPS
D_FF = 4 * D_MODEL
KV_WIDTH = N_KV_HEADS * HEAD_DIM
IN_WIDTH = ATTN_WIDTH + 2 * KV_WIDTH + 2 * SGU_WIDTH
EPS = 1e-6
NEG_INF = -1e30

LANES = 128
VMEM_LIMIT = 56 * 1024 * 1024
FFN_VMEM_LIMIT = 62 * 1024 * 1024
FFN_TF = 1024
FFN_SUB = 512

BF16 = jnp.bfloat16
F32 = jnp.float32

_KV0 = ATTN_WIDTH
_U0 = ATTN_WIDTH + 2 * KV_WIDTH
_V0 = _U0 + SGU_WIDTH


def _rms_rows(x, gain):
    ms = jnp.mean(x * x, axis=-1, keepdims=True)
    return x * lax.rsqrt(ms + EPS) * gain


def _head_pair_rms(blk, gain2, lane_lo):
    sq = blk * blk
    lo = jnp.sum(jnp.where(lane_lo, sq, 0.0), axis=-1, keepdims=True)
    hi = jnp.sum(jnp.where(lane_lo, 0.0, sq), axis=-1, keepdims=True)
    inv = 1.0 / HEAD_DIM
    r = jnp.where(lane_lo, lax.rsqrt(lo * inv + EPS), lax.rsqrt(hi * inv + EPS))
    return blk * r * gain2


def _cast_weights(wup_ref, wdn_ref, wo_ref, wupb_ref, wdnb_ref, wob_ref):
    tf = wupb_ref.shape[2]
    for j in range(wupb_ref.shape[0]):
        wupb_ref[j] = wup_ref[:, j * tf:(j + 1) * tf].astype(BF16)
    wdnb_ref[...] = wdn_ref[...].astype(BF16)
    wob_ref[...] = wo_ref[...].astype(BF16)


def _proj_kernel(chunk, emit_sv, cast_w, x_ref, lng_ref, win_ref, qg_ref, kg_ref, sg_ref,
                 ws_ref, bs_ref, gs_ref, *rest):
    if cast_w:
        _cast_weights(*rest[:3], *rest[-4:-1])
        rest = rest[3:-4] + rest[-1:]
    q_ref, k_ref, v_ref, mixs_ref = rest[:4]
    so_scr = rest[-1]
    tm = x_ref.shape[0]
    h = _rms_rows(x_ref[...], lng_ref[...]).astype(BF16)
    lane_lo = lax.broadcasted_iota(jnp.int32, (tm, LANES), 1) < HEAD_DIM

    av = jax.nn.gelu(jnp.dot(h, win_ref[:, _V0:IN_WIDTH], preferred_element_type=F32))
    sv = _rms_rows(av, sg_ref[...])
    if emit_sv:
        rest[4][...] = sv
    svb = sv.astype(BF16)
    u = jax.nn.gelu(jnp.dot(h, win_ref[:, _U0:_V0], preferred_element_type=F32))

    zq = jnp.dot(h, win_ref[:, 0:ATTN_WIDTH], preferred_element_type=F32)
    for c in range(ATTN_WIDTH // LANES):
        sl = slice(c * LANES, (c + 1) * LANES)
        q_ref[:, sl] = (_head_pair_rms(zq[:, sl], qg_ref[...], lane_lo)
                        * (HEAD_DIM ** -0.5)).astype(BF16)

    row_c = lax.broadcasted_iota(jnp.int32, (chunk, chunk), 0) // CHUNK
    col_c = lax.broadcasted_iota(jnp.int32, (chunk, chunk), 1) // CHUNK
    n_chunks = tm // chunk
    for g in range(SGU_GROUPS):
        cs = slice(g * SGU_GROUP_CH, (g + 1) * SGU_GROUP_CH)
        w = jnp.where(col_c <= row_c, ws_ref[g], 0.0).astype(BF16)
        rhs = jnp.concatenate(
            [svb[n * chunk:(n + 1) * chunk, cs] for n in range(n_chunks)], axis=1)
        sp = jnp.dot(w, rhs, preferred_element_type=F32)
        for n in range(n_chunks):
            rs = slice(n * chunk, (n + 1) * chunk)
            so_scr[rs, cs] = u[rs, cs] * (sp[:, n * LANES:(n + 1) * LANES] + bs_ref[g])

    zkv = jnp.dot(h, win_ref[:, _KV0:_U0], preferred_element_type=F32)
    k_ref[...] = _head_pair_rms(zkv[:, 0:KV_WIDTH], kg_ref[...], lane_lo)
    v_ref[...] = zkv[:, KV_WIDTH:2 * KV_WIDTH]
    mixs_ref[...] = _rms_rows(so_scr[...], gs_ref[...]).astype(BF16)


def _proj(x2d, lng, win, qg2, kg2, sg, ws, bsb, gs, wup=None, wdn=None, wo=None, *, tm, chunk,
          emit_sv, tf=None):
    t = x2d.shape[0]
    const = lambda shape: pl.BlockSpec(shape, lambda i: (0,) * len(shape),
                                       pipeline_mode=pl.Buffered(1))
    row = lambda w: pl.BlockSpec((tm, w), lambda i: (i, 0))
    out_shape = [jax.ShapeDtypeStruct((t, ATTN_WIDTH), BF16),
                 jax.ShapeDtypeStruct((t, KV_WIDTH), F32),
                 jax.ShapeDtypeStruct((t, KV_WIDTH), F32),
                 jax.ShapeDtypeStruct((t, SGU_WIDTH), BF16)]
    out_specs = [row(ATTN_WIDTH), row(KV_WIDTH), row(KV_WIDTH), row(SGU_WIDTH)]
    if emit_sv:
        out_shape.append(jax.ShapeDtypeStruct((t, SGU_WIDTH), F32))
        out_specs.append(row(SGU_WIDTH))
    in_specs = [row(D_MODEL), const((1, D_MODEL)), const((D_MODEL, IN_WIDTH)),
                const((1, LANES)), const((1, LANES)), const((1, SGU_WIDTH)),
                const((SGU_GROUPS, chunk, chunk)), const((SGU_GROUPS, chunk, LANES)),
                const((1, SGU_WIDTH))]
    args = [x2d, lng, win, qg2, kg2, sg, ws, bsb, gs]
    if wup is not None:
        n_steps = t // tm
        up_rows, dn_rows, wo_rows = D_MODEL // n_steps, D_FF // n_steps, wo.shape[0] // n_steps
        in_specs += [pl.BlockSpec((up_rows, D_FF), lambda i: (i, 0)),
                     pl.BlockSpec((dn_rows, D_MODEL), lambda i: (i, 0)),
                     pl.BlockSpec((wo_rows, D_MODEL), lambda i: (i, 0))]
        out_specs += [pl.BlockSpec((D_FF // tf, up_rows, tf), lambda i: (0, i, 0)),
                      pl.BlockSpec((dn_rows, D_MODEL), lambda i: (i, 0)),
                      pl.BlockSpec((wo_rows, D_MODEL), lambda i: (i, 0))]
        out_shape += [jax.ShapeDtypeStruct((D_FF // tf, D_MODEL, tf), BF16),
                      jax.ShapeDtypeStruct((D_FF, D_MODEL), BF16),
                      jax.ShapeDtypeStruct(wo.shape, BF16)]
        args += [wup, wdn, wo]
    return pl.pallas_call(
        functools.partial(_proj_kernel, chunk, emit_sv, wup is not None),
        grid=(t // tm,),
        in_specs=in_specs,
        out_specs=out_specs,
        out_shape=out_shape,
        scratch_shapes=[pltpu.VMEM((tm, SGU_WIDTH), F32)],
        compiler_params=pltpu.CompilerParams(dimension_semantics=("parallel",),
                                             vmem_limit_bytes=VMEM_LIMIT),
        name="proj_c%d" % chunk,
    )(*args)


BAND = 2 * LANES
PAIRS = GQA_GROUP // 2


def _prep_kernel(bq_s, tab_ref, bkt_ref, valid_ref, win_ref, bias_p_ref, bias_s_ref, winb_ref):
    winb_ref[...] = win_ref[...].astype(BF16)
    bkt = bkt_ref[...]
    for half in range(2):
        h = 2 * pl.program_id(0) + half
        acc = jnp.zeros(bkt.shape, F32)
        for b in range(N_BUCKETS):
            acc = jnp.where(bkt == b, tab_ref[b, h], acc)
        cs = slice(half * BAND, (half + 1) * BAND)
        for s in range(valid_ref.shape[0]):
            bias_p_ref[s, :, cs] = jnp.where(valid_ref[s] != 0, acc, NEG_INF)
        bias_s_ref[0, :, cs] = jnp.where(valid_ref[valid_ref.shape[0] - 1, 0:bq_s, :] != 0,
                                         acc[0:bq_s], NEG_INF)


def _prep(table, bucket, valid, win, *, bq_s):
    nsel, bq, _ = valid.shape
    n_steps = N_HEADS // 2
    slab = pl.BlockSpec((win.shape[0] // n_steps, win.shape[1]), lambda s: (s, 0))
    pair = lambda n, rows: pl.BlockSpec((n, None, rows, 2 * BAND),
                                        lambda s: (0, s // PAIRS, s % PAIRS, 0))
    return pl.pallas_call(
        functools.partial(_prep_kernel, bq_s),
        grid=(n_steps,),
        in_specs=[pl.BlockSpec(memory_space=pltpu.SMEM),
                  pl.BlockSpec(bucket.shape, lambda s: (0, 0)),
                  pl.BlockSpec(valid.shape, lambda s: (0, 0, 0)),
                  slab],
        out_specs=[pair(nsel, bq), pair(1, bq_s), slab],
        out_shape=[jax.ShapeDtypeStruct((nsel, N_KV_HEADS, PAIRS * bq, 2 * BAND), F32),
                   jax.ShapeDtypeStruct((1, N_KV_HEADS, PAIRS * bq_s, 2 * BAND), F32),
                   jax.ShapeDtypeStruct(win.shape, BF16)],
        compiler_params=pltpu.CompilerParams(dimension_semantics=("parallel",)),
        name="prep",
    )(table, bucket, valid, win)


def _t5_bucket(n):
    half = N_BUCKETS // 2
    max_exact = half // 2
    offset = jnp.where(n < 0, half, 0)
    a = jnp.abs(n)
    af = jnp.maximum(a, 1).astype(F32)
    large = max_exact + (jnp.log(af / max_exact) / math.log(MAX_DISTANCE / max_exact)
                         * (half - max_exact)).astype(jnp.int32)
    large = jnp.minimum(large, half - 1)
    return offset + jnp.where(a < max_exact, a, large)


def _attn_kernel(bq, n_units, k_step, nsel, n_kparts, k_layout, *refs):
    q_ref = refs[0]
    k_refs = refs[1:1 + n_kparts]
    v_refs = refs[1 + n_kparts:1 + 2 * n_kparts]
    (bias_ref, smax_ref, sl_ref, ga_ref, x_ref, ms_ref, wo_ref, g_ref,
     o_ref, h_ref, kk_scr, vv_scr, ao_scr) = refs[1 + 2 * n_kparts:]
    rows = PAIRS * bq

    def split_heads(parts, scr):
        x = jnp.concatenate(
            [parts[p][r0:r0 + n, :] if p >= 0 else jnp.zeros((n, KV_WIDTH), F32)
             for p, r0, n in k_layout], axis=0)
        lo = lax.broadcasted_iota(jnp.int32, x.shape, 1) < HEAD_DIM
        xr = pltpu.roll(x, HEAD_DIM, 1)
        scr[0, 0] = jnp.where(lo, x, 0.0).astype(BF16)
        scr[0, 1] = jnp.where(lo, 0.0, xr).astype(BF16)
        scr[1, 0] = jnp.where(lo, xr, 0.0).astype(BF16)
        scr[1, 1] = jnp.where(lo, 0.0, x).astype(BF16)

    split_heads(k_refs, kk_scr)
    split_heads(v_refs, vv_scr)
    r_i = lax.broadcasted_iota(jnp.int32, (2 * BAND, LANES), 0)
    l_i = lax.broadcasted_iota(jnp.int32, (2 * BAND, LANES), 1)
    ones_blk = jnp.where((r_i < BAND) == (l_i < HEAD_DIM), 1.0, 0.0).astype(BF16)
    lane_lo = lax.broadcasted_iota(jnp.int32, (rows, LANES), 1) < HEAD_DIM
    first_block = pl.program_id(1) == 0

    for u in range(n_units):
        q0, k0 = u * bq, u * k_step
        sel = jnp.where(first_block, 0, 1) if (nsel > 1 and u == 0) else nsel - 1
        for g in range(N_KV_HEADS):
            cols = [slice((PAIRS * g + p) * LANES, (PAIRS * g + p + 1) * LANES)
                    for p in range(PAIRS)]
            qs = jnp.concatenate([q_ref[q0:q0 + bq, c] for c in cols], axis=0)
            k2 = jnp.concatenate([kk_scr[g, 0, k0:k0 + BAND, :],
                                  kk_scr[g, 1, k0:k0 + BAND, :]], axis=0)
            s = lax.dot_general(qs, k2, (((1,), (1,)), ((), ())),
                                preferred_element_type=F32) + bias_ref[sel, g]
            sa, sb = s[:, :BAND], s[:, BAND:]
            ma = jnp.max(jnp.maximum(jnp.maximum(sa[:, :LANES], sa[:, LANES:]), smax_ref[g, 0]),
                         axis=-1, keepdims=True)
            mb = jnp.max(jnp.maximum(jnp.maximum(sb[:, :LANES], sb[:, LANES:]), smax_ref[g, 1]),
                         axis=-1, keepdims=True)
            p = jnp.concatenate([jnp.exp(sa - ma), jnp.exp(sb - mb)], axis=1).astype(BF16)
            v2 = jnp.concatenate([vv_scr[g, 0, k0:k0 + BAND, :],
                                  vv_scr[g, 1, k0:k0 + BAND, :]], axis=0)
            ol = jnp.dot(p, jnp.concatenate([v2, ones_blk], axis=1), preferred_element_type=F32)
            denom = ol[:, LANES:] + jnp.exp(sl_ref[g] - jnp.where(lane_lo, ma, mb))
            out = ol[:, :LANES] / denom
            for p_i, c in enumerate(cols):
                ao_scr[q0:q0 + bq, c] = out[p_i * bq:(p_i + 1) * bq]

    mix_a = _rms_rows(ao_scr[...], ga_ref[...]).astype(BF16)
    acc = jnp.dot(ms_ref[...], wo_ref[ATTN_WIDTH:, :], preferred_element_type=F32)
    acc = acc + jnp.dot(mix_a, wo_ref[0:ATTN_WIDTH, :], preferred_element_type=F32)
    x1 = x_ref[...] + acc
    o_ref[...] = x1
    h_ref[...] = _rms_rows(x1, g_ref[...]).astype(BF16)


def _attn(q, k_parts, v_parts, k_maps, k_layout, bias, smax, sl, ga, x, mix_s, wo, g, *, bq,
          n_units, k_step):
    nb, s, _ = q.shape
    nsel = bias.shape[0]
    tq = n_units * bq
    k_rows = [rows for _, rows in k_parts]
    kr = sum(n for _, _, n in k_layout)
    cur = lambda b, i: (b, i, 0)
    row = lambda w: pl.BlockSpec((None, tq, w), cur)
    full = lambda a: pl.BlockSpec(a.shape, lambda b, i: (0,) * a.ndim,
                                  pipeline_mode=pl.Buffered(1))
    kv_specs = [pl.BlockSpec((None, r, KV_WIDTH), m) for r, m in zip(k_rows, k_maps)]
    return pl.pallas_call(
        functools.partial(_attn_kernel, bq, n_units, k_step, nsel, len(k_parts), tuple(k_layout)),
        grid=(nb, s // tq),
        in_specs=([row(ATTN_WIDTH)] + kv_specs + kv_specs
                  + [full(bias), full(smax), full(sl), full(ga),
                     row(D_MODEL), row(SGU_WIDTH), full(wo), full(g)]),
        out_specs=[row(D_MODEL), row(D_MODEL)],
        out_shape=[jax.ShapeDtypeStruct((nb, s, D_MODEL), F32),
                   jax.ShapeDtypeStruct((nb, s, D_MODEL), BF16)],
        scratch_shapes=[pltpu.VMEM((N_KV_HEADS, 2, kr, KV_WIDTH), BF16),
                        pltpu.VMEM((N_KV_HEADS, 2, kr, KV_WIDTH), BF16),
                        pltpu.VMEM((tq, ATTN_WIDTH), F32)],
        compiler_params=pltpu.CompilerParams(dimension_semantics=("parallel", "arbitrary"),
                                             vmem_limit_bytes=VMEM_LIMIT),
        name="attn_q%d" % bq,
    )(q, *[a for a, _ in k_parts], *[a for a, _ in v_parts], bias, smax, sl, ga, x, mix_s, wo, g)


def _ffn_kernel(x_ref, h_ref, wup_ref, wdn_ref, y_ref):
    tf = wup_ref.shape[1]
    for s in range(tf // FFN_SUB):
        cs = slice(s * FFN_SUB, (s + 1) * FFN_SUB)
        a = jnp.dot(h_ref[...], wup_ref[:, cs], preferred_element_type=F32)
        a = jnp.square(jnp.maximum(a, 0.0)).astype(BF16)
        if s == 0:
            base = jnp.where(pl.program_id(1) == 0, x_ref[...], y_ref[...])
        else:
            base = y_ref[...]
        y_ref[...] = base + jnp.dot(a, wdn_ref[cs, :], preferred_element_type=F32)


def _ffn(x2d, h2d, wup_t, wdn, *, tm):
    t = x2d.shape[0]
    n_f, _, tf = wup_t.shape
    return pl.pallas_call(
        _ffn_kernel,
        grid=(t // tm, n_f),
        in_specs=[pl.BlockSpec((tm, D_MODEL), lambda i, j: (i, 0)),
                  pl.BlockSpec((tm, D_MODEL), lambda i, j: (i, 0)),
                  pl.BlockSpec((None, D_MODEL, tf), lambda i, j: (j, 0, 0)),
                  pl.BlockSpec((tf, D_MODEL), lambda i, j: (j, 0))],
        out_specs=pl.BlockSpec((tm, D_MODEL), lambda i, j: (i, 0)),
        out_shape=jax.ShapeDtypeStruct((t, D_MODEL), F32),
        compiler_params=pltpu.CompilerParams(dimension_semantics=("parallel", "arbitrary"),
                                             vmem_limit_bytes=FFN_VMEM_LIMIT),
        name="ffn",
    )(x2d, h2d, wup_t, wdn)


def _band_tables(bq, nsel):
    qi = jnp.arange(bq)[:, None]
    kj = jnp.arange(BAND)[None, :] - WINDOW
    qc, kc = qi // CHUNK, jnp.floor_divide(kj, CHUNK)
    in_band = (kc <= qc) & (kc >= qc - WINDOW // CHUNK)
    valid = [in_band & (kj >= 0)] if nsel == 2 else []
    valid.append(in_band)
    return _t5_bucket(qi - kj), jnp.stack(valid).astype(jnp.int32)


def _sink_tables(sinks, bq):
    s3 = sinks.reshape(N_KV_HEADS, PAIRS, 2)
    rows = PAIRS * bq
    smax = jnp.broadcast_to(jnp.transpose(s3, (0, 2, 1))[:, :, :, None, None],
                            (N_KV_HEADS, 2, PAIRS, bq, LANES)).reshape(N_KV_HEADS, 2, rows, LANES)
    sl = jnp.broadcast_to(jnp.repeat(s3, HEAD_DIM, axis=-1)[:, :, None, :],
                          (N_KV_HEADS, PAIRS, bq, LANES)).reshape(N_KV_HEADS, rows, LANES)
    return smax, sl


def kernel(x_prompt, x_sample, cache_attn_k, cache_attn_v, rel_bias_table, ln_mix_g, w_in,
           q_norm_g, k_norm_g, attn_sinks, sgu_norm_g, sgu_w, sgu_b, out_norm_attn_g,
           out_norm_sgu_g, w_out, ln_ffn_g, w_ffn_up, w_ffn_down):
    bp, sp, _ = x_prompt.shape
    bs, ss, _ = x_sample.shape
    depth = w_in.shape[0]
    assert depth == 1 and ss == CHUNK and cache_attn_k.shape[2] == WINDOW
    l = 0
    xp = x_prompt.reshape(bp * sp, D_MODEL)
    xs = x_sample.reshape(bs * ss, D_MODEL)

    bq, units = 2 * CHUNK, 4
    bias_p, bias_s, win = _prep(rel_bias_table, *_band_tables(bq, 2), w_in[l], bq_s=ss)
    lng = ln_mix_g[l].reshape(1, D_MODEL)
    qg2 = jnp.tile(q_norm_g[l], 2).reshape(1, LANES)
    kg2 = jnp.tile(k_norm_g[l], 2).reshape(1, LANES)
    sg = sgu_norm_g[l].reshape(1, SGU_WIDTH)
    gs = out_norm_sgu_g[l].reshape(1, SGU_WIDTH)
    ga = out_norm_attn_g[l].reshape(1, ATTN_WIDTH)
    lnf = ln_ffn_g[l].reshape(1, D_MODEL)
    sinks = attn_sinks[l].reshape(N_HEADS)

    def sgu_params(c):
        return (sgu_w[l][:, :c, :c],
                jnp.broadcast_to(sgu_b[l][:, :c, None], (SGU_GROUPS, c, LANES)))

    ws_p, bs_p = sgu_params(SGU_CHUNK)
    q, k, v, mix_s, wup, wdn, wo = _proj(xp, lng, win, qg2, kg2, sg, ws_p, bs_p, gs,
                                         w_ffn_up[l], w_ffn_down[l], w_out[l],
                                         tm=512, chunk=SGU_CHUNK, emit_sv=False, tf=FFN_TF)
    k3 = k.reshape(bp, sp, KV_WIDTH)
    v3 = v.reshape(bp, sp, KV_WIDTH)
    prev = lambda b, i: (b, jnp.maximum(i * units - 1, 0), 0)
    cur = lambda b, i: (b, i, 0)
    x1p, h2p = _attn(q.reshape(bp, sp, ATTN_WIDTH),
                     [(k3, WINDOW), (k3, units * bq)], [(v3, WINDOW), (v3, units * bq)],
                     [prev, cur], [(0, 0, WINDOW), (1, 0, units * bq)],
                     bias_p, *_sink_tables(sinks, bq), ga,
                     x_prompt, mix_s.reshape(bp, sp, SGU_WIDTH), wo, lnf,
                     bq=bq, n_units=units, k_step=bq)
    yp = _ffn(x1p.reshape(bp * sp, D_MODEL), h2p.reshape(bp * sp, D_MODEL), wup, wdn, tm=1024)

    ws_s, bs_s = sgu_params(ss)
    qs, ks, vs, mix_ss, svs = _proj(xs, lng, win, qg2, kg2, sg, ws_s, bs_s, gs,
                                    tm=512, chunk=ss, emit_sv=True)
    seqs = 4
    grp = lambda a, rows: a.reshape(bs // seqs, seqs * rows, a.shape[-1])
    layout = []
    for i in range(seqs):
        layout += [(0, i * WINDOW, WINDOW), (1, i * ss, ss), (-1, 0, BAND - WINDOW - ss)]
    x1s, h2s = _attn(grp(qs, ss),
                     [(grp(cache_attn_k[l].reshape(bs, WINDOW, KV_WIDTH), WINDOW), seqs * WINDOW),
                      (grp(ks, ss), seqs * ss)],
                     [(grp(cache_attn_v[l].reshape(bs, WINDOW, KV_WIDTH), WINDOW), seqs * WINDOW),
                      (grp(vs, ss), seqs * ss)],
                     [cur, cur], layout, bias_s,
                     *_sink_tables(sinks, ss), ga, grp(xs, ss), grp(mix_ss, ss), wo, lnf,
                     bq=ss, n_units=seqs, k_step=BAND)
    ys = _ffn(x1s.reshape(bs * ss, D_MODEL), h2s.reshape(bs * ss, D_MODEL), wup, wdn, tm=1024)

    keep = min(WINDOW, sp)
    kv_shape = (N_KV_HEADS, HEAD_DIM)
    return (yp.reshape(bp, sp, D_MODEL),
            ys.reshape(bs, ss, D_MODEL),
            k3[:, -keep:].reshape(1, bp, keep, *kv_shape),
            v3[:, -keep:].reshape(1, bp, keep, *kv_shape),
            ks.reshape(1, bs, ss, *kv_shape),
            vs.reshape(1, bs, ss, *kv_shape),
            svs.reshape(1, bs, ss, SGU_WIDTH))
```

```python
import functools
import math

import jax
import jax.numpy as jnp
from jax import lax
from jax.experimental import pallas as pl
from jax.experimental.pallas import tpu as pltpu

D_MODEL = 2048
CHUNK = 64
ATTN_WIDTH = 1024
SGU_WIDTH = 1024
HEAD_DIM = 64
N_HEADS = ATTN_WIDTH // HEAD_DIM
N_KV_HEADS = 2
GQA_GROUP = N_HEADS // N_KV_HEADS
WINDOW = 128
N_BUCKETS = 32
MAX_DISTANCE = 128
SGU_CHUNK = 128
SGU_GROUPS = 8
SGU_GROUP_CH = SGU_WIDTH // SGU_GROUPS
D_FF = 4 * D_MODEL
KV_WIDTH = N_KV_HEADS * HEAD_DIM
IN_WIDTH = ATTN_WIDTH + 2 * KV_WIDTH + 2 * SGU_WIDTH
EPS = 1e-6
NEG_INF = -1e30

LANES = 128
VMEM_LIMIT = 56 * 1024 * 1024
FFN_VMEM_LIMIT = 62 * 1024 * 1024
FFN_TF = 1024
FFN_SUB = 1024

BF16 = jnp.bfloat16
F32 = jnp.float32

_KV0 = ATTN_WIDTH
_U0 = ATTN_WIDTH + 2 * KV_WIDTH
_V0 = _U0 + SGU_WIDTH


def _rms_rows(x, gain):
    ms = jnp.mean(x * x, axis=-1, keepdims=True)
    return x * lax.rsqrt(ms + EPS) * gain


def _head_pair_rms(blk, gain2, lane_lo):
    sq = blk * blk
    lo = jnp.sum(jnp.where(lane_lo, sq, 0.0), axis=-1, keepdims=True)
    hi = jnp.sum(jnp.where(lane_lo, 0.0, sq), axis=-1, keepdims=True)
    inv = 1.0 / HEAD_DIM
    r = jnp.where(lane_lo, lax.rsqrt(lo * inv + EPS), lax.rsqrt(hi * inv + EPS))
    return blk * r * gain2


def _cast_weights(wup_ref, wdn_ref, wo_ref, wupb_ref, wdnb_ref, wob_ref):
    tf = wupb_ref.shape[2]
    for j in range(wupb_ref.shape[0]):
        wupb_ref[j] = wup_ref[:, j * tf:(j + 1) * tf].astype(BF16)
    wdnb_ref[...] = wdn_ref[...].astype(BF16)
    wob_ref[...] = wo_ref[...].astype(BF16)


def _proj_kernel(chunk, emit_sv, cast_w, x_ref, lng_ref, win_ref, qg_ref, kg_ref, sg_ref,
                 ws_ref, bs_ref, gs_ref, *rest):
    if cast_w:
        _cast_weights(*rest[:3], *rest[-4:-1])
        rest = rest[3:-4] + rest[-1:]
    q_ref, k_ref, v_ref, mixs_ref = rest[:4]
    so_scr = rest[-1]
    tm = x_ref.shape[0]
    h = _rms_rows(x_ref[...], lng_ref[...]).astype(BF16)
    lane_lo = lax.broadcasted_iota(jnp.int32, (tm, LANES), 1) < HEAD_DIM

    av = jax.nn.gelu(jnp.dot(h, win_ref[:, _V0:IN_WIDTH], preferred_element_type=F32))
    sv = _rms_rows(av, sg_ref[...])
    if emit_sv:
        rest[4][...] = sv
    svb = sv.astype(BF16)
    u = jax.nn.gelu(jnp.dot(h, win_ref[:, _U0:_V0], preferred_element_type=F32))

    zq = jnp.dot(h, win_ref[:, 0:ATTN_WIDTH], preferred_element_type=F32)
    for c in range(ATTN_WIDTH // LANES):
        sl = slice(c * LANES, (c + 1) * LANES)
        q_ref[:, sl] = (_head_pair_rms(zq[:, sl], qg_ref[...], lane_lo)
                        * (HEAD_DIM ** -0.5)).astype(BF16)

    row_c = lax.broadcasted_iota(jnp.int32, (chunk, chunk), 0) // CHUNK
    col_c = lax.broadcasted_iota(jnp.int32, (chunk, chunk), 1) // CHUNK
    n_chunks = tm // chunk
    for g in range(SGU_GROUPS):
        cs = slice(g * SGU_GROUP_CH, (g + 1) * SGU_GROUP_CH)
        w = jnp.where(col_c <= row_c, ws_ref[g], 0.0).astype(BF16)
        rhs = jnp.concatenate(
            [svb[n * chunk:(n + 1) * chunk, cs] for n in range(n_chunks)], axis=1)
        sp = jnp.dot(w, rhs, preferred_element_type=F32)
        for n in range(n_chunks):
            rs = slice(n * chunk, (n + 1) * chunk)
            so_scr[rs, cs] = u[rs, cs] * (sp[:, n * LANES:(n + 1) * LANES] + bs_ref[g])

    zkv = jnp.dot(h, win_ref[:, _KV0:_U0], preferred_element_type=F32)
    k_ref[...] = _head_pair_rms(zkv[:, 0:KV_WIDTH], kg_ref[...], lane_lo)
    v_ref[...] = zkv[:, KV_WIDTH:2 * KV_WIDTH]
    mixs_ref[...] = _rms_rows(so_scr[...], gs_ref[...]).astype(BF16)


def _proj(x2d, lng, win, qg2, kg2, sg, ws, bsb, gs, wup=None, wdn=None, wo=None, *, tm, chunk,
          emit_sv, tf=None):
    t = x2d.shape[0]
    const = lambda shape: pl.BlockSpec(shape, lambda i: (0,) * len(shape),
                                       pipeline_mode=pl.Buffered(1))
    row = lambda w: pl.BlockSpec((tm, w), lambda i: (i, 0))
    out_shape = [jax.ShapeDtypeStruct((t, ATTN_WIDTH), BF16),
                 jax.ShapeDtypeStruct((t, KV_WIDTH), F32),
                 jax.ShapeDtypeStruct((t, KV_WIDTH), F32),
                 jax.ShapeDtypeStruct((t, SGU_WIDTH), BF16)]
    out_specs = [row(ATTN_WIDTH), row(KV_WIDTH), row(KV_WIDTH), row(SGU_WIDTH)]
    if emit_sv:
        out_shape.append(jax.ShapeDtypeStruct((t, SGU_WIDTH), F32))
        out_specs.append(row(SGU_WIDTH))
    in_specs = [row(D_MODEL), const((1, D_MODEL)), const((D_MODEL, IN_WIDTH)),
                const((1, LANES)), const((1, LANES)), const((1, SGU_WIDTH)),
                const((SGU_GROUPS, chunk, chunk)), const((SGU_GROUPS, chunk, LANES)),
                const((1, SGU_WIDTH))]
    args = [x2d, lng, win, qg2, kg2, sg, ws, bsb, gs]
    if wup is not None:
        n_steps = t // tm
        up_rows, dn_rows, wo_rows = D_MODEL // n_steps, D_FF // n_steps, wo.shape[0] // n_steps
        in_specs += [pl.BlockSpec((up_rows, D_FF), lambda i: (i, 0)),
                     pl.BlockSpec((dn_rows, D_MODEL), lambda i: (i, 0)),
                     pl.BlockSpec((wo_rows, D_MODEL), lambda i: (i, 0))]
        out_specs += [pl.BlockSpec((D_FF // tf, up_rows, tf), lambda i: (0, i, 0)),
                      pl.BlockSpec((dn_rows, D_MODEL), lambda i: (i, 0)),
                      pl.BlockSpec((wo_rows, D_MODEL), lambda i: (i, 0))]
        out_shape += [jax.ShapeDtypeStruct((D_FF // tf, D_MODEL, tf), BF16),
                      jax.ShapeDtypeStruct((D_FF, D_MODEL), BF16),
                      jax.ShapeDtypeStruct(wo.shape, BF16)]
        args += [wup, wdn, wo]
    return pl.pallas_call(
        functools.partial(_proj_kernel, chunk, emit_sv, wup is not None),
        grid=(t // tm,),
        in_specs=in_specs,
        out_specs=out_specs,
        out_shape=out_shape,
        scratch_shapes=[pltpu.VMEM((tm, SGU_WIDTH), F32)],
        compiler_params=pltpu.CompilerParams(dimension_semantics=("parallel",),
                                             vmem_limit_bytes=VMEM_LIMIT),
        name="proj_c%d" % chunk,
    )(*args)


BAND = 2 * LANES
PAIRS = GQA_GROUP // 2


def _prep_kernel(bq_s, tab_ref, bkt_ref, valid_ref, win_ref, bias_p_ref, bias_s_ref, winb_ref):
    winb_ref[...] = win_ref[...].astype(BF16)
    bkt = bkt_ref[...]
    for half in range(2):
        h = 2 * pl.program_id(0) + half
        acc = jnp.zeros(bkt.shape, F32)
        for b in range(N_BUCKETS):
            acc = jnp.where(bkt == b, tab_ref[b, h], acc)
        cs = slice(half * BAND, (half + 1) * BAND)
        for s in range(valid_ref.shape[0]):
            bias_p_ref[s, :, cs] = jnp.where(valid_ref[s] != 0, acc, NEG_INF)
        bias_s_ref[0, :, cs] = jnp.where(valid_ref[valid_ref.shape[0] - 1, 0:bq_s, :] != 0,
                                         acc[0:bq_s], NEG_INF)


def _prep(table, bucket, valid, win, *, bq_s):
    nsel, bq, _ = valid.shape
    n_steps = N_HEADS // 2
    slab = pl.BlockSpec((win.shape[0] // n_steps, win.shape[1]), lambda s: (s, 0))
    pair = lambda n, rows: pl.BlockSpec((n, None, rows, 2 * BAND),
                                        lambda s: (0, s // PAIRS, s % PAIRS, 0))
    return pl.pallas_call(
        functools.partial(_prep_kernel, bq_s),
        grid=(n_steps,),
        in_specs=[pl.BlockSpec(memory_space=pltpu.SMEM),
                  pl.BlockSpec(bucket.shape, lambda s: (0, 0)),
                  pl.BlockSpec(valid.shape, lambda s: (0, 0, 0)),
                  slab],
        out_specs=[pair(nsel, bq), pair(1, bq_s), slab],
        out_shape=[jax.ShapeDtypeStruct((nsel, N_KV_HEADS, PAIRS * bq, 2 * BAND), F32),
                   jax.ShapeDtypeStruct((1, N_KV_HEADS, PAIRS * bq_s, 2 * BAND), F32),
                   jax.ShapeDtypeStruct(win.shape, BF16)],
        compiler_params=pltpu.CompilerParams(dimension_semantics=("parallel",)),
        name="prep",
    )(table, bucket, valid, win)


def _t5_bucket(n):
    half = N_BUCKETS // 2
    max_exact = half // 2
    offset = jnp.where(n < 0, half, 0)
    a = jnp.abs(n)
    af = jnp.maximum(a, 1).astype(F32)
    large = max_exact + (jnp.log(af / max_exact) / math.log(MAX_DISTANCE / max_exact)
                         * (half - max_exact)).astype(jnp.int32)
    large = jnp.minimum(large, half - 1)
    return offset + jnp.where(a < max_exact, a, large)


def _attn_kernel(bq, n_units, k_step, nsel, n_kparts, k_layout, *refs):
    q_ref = refs[0]
    k_refs = refs[1:1 + n_kparts]
    v_refs = refs[1 + n_kparts:1 + 2 * n_kparts]
    (bias_ref, smax_ref, sl_ref, ga_ref, x_ref, ms_ref, wo_ref,
     o_ref, r_ref, kk_scr, vv_scr, ao_scr) = refs[1 + 2 * n_kparts:]
    rows = PAIRS * bq

    def split_heads(parts, scr):
        x = jnp.concatenate(
            [parts[p][r0:r0 + n, :] if p >= 0 else jnp.zeros((n, KV_WIDTH), F32)
             for p, r0, n in k_layout], axis=0)
        lo = lax.broadcasted_iota(jnp.int32, x.shape, 1) < HEAD_DIM
        xr = pltpu.roll(x, HEAD_DIM, 1)
        scr[0, 0] = jnp.where(lo, x, 0.0).astype(BF16)
        scr[0, 1] = jnp.where(lo, 0.0, xr).astype(BF16)
        scr[1, 0] = jnp.where(lo, xr, 0.0).astype(BF16)
        scr[1, 1] = jnp.where(lo, 0.0, x).astype(BF16)

    split_heads(k_refs, kk_scr)
    split_heads(v_refs, vv_scr)
    r_i = lax.broadcasted_iota(jnp.int32, (2 * BAND, LANES), 0)
    l_i = lax.broadcasted_iota(jnp.int32, (2 * BAND, LANES), 1)
    ones_blk = jnp.where((r_i < BAND) == (l_i < HEAD_DIM), 1.0, 0.0).astype(BF16)
    lane_lo = lax.broadcasted_iota(jnp.int32, (rows, LANES), 1) < HEAD_DIM
    first_block = pl.program_id(1) == 0

    for u in range(n_units):
        q0, k0 = u * bq, u * k_step
        sel = jnp.where(first_block, 0, 1) if (nsel > 1 and u == 0) else nsel - 1
        for g in range(N_KV_HEADS):
            cols = [slice((PAIRS * g + p) * LANES, (PAIRS * g + p + 1) * LANES)
                    for p in range(PAIRS)]
            qs = jnp.concatenate([q_ref[q0:q0 + bq, c] for c in cols], axis=0)
            k2 = jnp.concatenate([kk_scr[g, 0, k0:k0 + BAND, :],
                                  kk_scr[g, 1, k0:k0 + BAND, :]], axis=0)
            s = lax.dot_general(qs, k2, (((1,), (1,)), ((), ())),
                                preferred_element_type=F32) + bias_ref[sel, g]
            sa, sb = s[:, :BAND], s[:, BAND:]
            ma = jnp.max(jnp.maximum(jnp.maximum(sa[:, :LANES], sa[:, LANES:]), smax_ref[g, 0]),
                         axis=-1, keepdims=True)
            mb = jnp.max(jnp.maximum(jnp.maximum(sb[:, :LANES], sb[:, LANES:]), smax_ref[g, 1]),
                         axis=-1, keepdims=True)
            p = jnp.concatenate([jnp.exp(sa - ma), jnp.exp(sb - mb)], axis=1).astype(BF16)
            v2 = jnp.concatenate([vv_scr[g, 0, k0:k0 + BAND, :],
                                  vv_scr[g, 1, k0:k0 + BAND, :]], axis=0)
            ol = jnp.dot(p, jnp.concatenate([v2, ones_blk], axis=1), preferred_element_type=F32)
            denom = ol[:, LANES:] + jnp.exp(sl_ref[g] - jnp.where(lane_lo, ma, mb))
            out = ol[:, :LANES] / denom
            for p_i, c in enumerate(cols):
                ao_scr[q0:q0 + bq, c] = out[p_i * bq:(p_i + 1) * bq]

    mix_a = _rms_rows(ao_scr[...], ga_ref[...]).astype(BF16)
    acc = jnp.dot(ms_ref[...], wo_ref[ATTN_WIDTH:, :], preferred_element_type=F32)
    acc = acc + jnp.dot(mix_a, wo_ref[0:ATTN_WIDTH, :], preferred_element_type=F32)
    x1 = x_ref[...] + acc
    o_ref[...] = x1
    rinv = lax.rsqrt(jnp.mean(x1 * x1, axis=-1, keepdims=True) + EPS)
    r_ref[...] = jnp.broadcast_to(rinv * rinv, r_ref.shape)


def _attn(q, k_parts, v_parts, k_maps, k_layout, bias, smax, sl, ga, x, mix_s, wo, *, bq,
          n_units, k_step):
    nb, s, _ = q.shape
    nsel = bias.shape[0]
    tq = n_units * bq
    k_rows = [rows for _, rows in k_parts]
    kr = sum(n for _, _, n in k_layout)
    cur = lambda b, i: (b, i, 0)
    row = lambda w: pl.BlockSpec((None, tq, w), cur)
    full = lambda a: pl.BlockSpec(a.shape, lambda b, i: (0,) * a.ndim,
                                  pipeline_mode=pl.Buffered(1))
    kv_specs = [pl.BlockSpec((None, r, KV_WIDTH), m) for r, m in zip(k_rows, k_maps)]
    return pl.pallas_call(
        functools.partial(_attn_kernel, bq, n_units, k_step, nsel, len(k_parts), tuple(k_layout)),
        grid=(nb, s // tq),
        in_specs=([row(ATTN_WIDTH)] + kv_specs + kv_specs
                  + [full(bias), full(smax), full(sl), full(ga),
                     row(D_MODEL), row(SGU_WIDTH), full(wo)]),
        out_specs=[row(D_MODEL), row(LANES)],
        out_shape=[jax.ShapeDtypeStruct((nb, s, D_MODEL), F32),
                   jax.ShapeDtypeStruct((nb, s, LANES), F32)],
        scratch_shapes=[pltpu.VMEM((N_KV_HEADS, 2, kr, KV_WIDTH), BF16),
                        pltpu.VMEM((N_KV_HEADS, 2, kr, KV_WIDTH), BF16),
                        pltpu.VMEM((tq, ATTN_WIDTH), F32)],
        compiler_params=pltpu.CompilerParams(dimension_semantics=("parallel", "arbitrary"),
                                             vmem_limit_bytes=VMEM_LIMIT),
        name="attn_q%d" % bq,
    )(q, *[a for a, _ in k_parts], *[a for a, _ in v_parts], bias, smax, sl, ga, x, mix_s, wo)


def _ffn_kernel(x_ref, r_ref, g_ref, wup_ref, wdn_ref, y_ref):
    tf = wup_ref.shape[1]
    xg = (x_ref[...] * g_ref[...]).astype(BF16)
    r2 = jnp.tile(r_ref[...], (1, FFN_SUB // LANES))
    for s in range(tf // FFN_SUB):
        cs = slice(s * FFN_SUB, (s + 1) * FFN_SUB)
        z = jnp.dot(xg, wup_ref[:, cs], preferred_element_type=F32)
        a = (jnp.square(jnp.maximum(z, 0.0)) * r2).astype(BF16)
        if s == 0:
            base = jnp.where(pl.program_id(1) == 0, x_ref[...], y_ref[...])
        else:
            base = y_ref[...]
        y_ref[...] = base + jnp.dot(a, wdn_ref[cs, :], preferred_element_type=F32)


def _ffn(x2d, r2d, g, wup_t, wdn, *, tm):
    t = x2d.shape[0]
    n_f, _, tf = wup_t.shape
    return pl.pallas_call(
        _ffn_kernel,
        grid=(t // tm, n_f),
        in_specs=[pl.BlockSpec((tm, D_MODEL), lambda i, j: (i, 0)),
                  pl.BlockSpec((tm, LANES), lambda i, j: (i, 0)),
                  pl.BlockSpec((1, D_MODEL), lambda i, j: (0, 0)),
                  pl.BlockSpec((None, D_MODEL, tf), lambda i, j: (j, 0, 0)),
                  pl.BlockSpec((tf, D_MODEL), lambda i, j: (j, 0))],
        out_specs=pl.BlockSpec((tm, D_MODEL), lambda i, j: (i, 0)),
        out_shape=jax.ShapeDtypeStruct((t, D_MODEL), F32),
        compiler_params=pltpu.CompilerParams(dimension_semantics=("parallel", "arbitrary"),
                                             vmem_limit_bytes=FFN_VMEM_LIMIT),
        name="ffn",
    )(x2d, r2d, g, wup_t, wdn)


def _band_tables(bq, nsel):
    qi = jnp.arange(bq)[:, None]
    kj = jnp.arange(BAND)[None, :] - WINDOW
    qc, kc = qi // CHUNK, jnp.floor_divide(kj, CHUNK)
    in_band = (kc <= qc) & (kc >= qc - WINDOW // CHUNK)
    valid = [in_band & (kj >= 0)] if nsel == 2 else []
    valid.append(in_band)
    return _t5_bucket(qi - kj), jnp.stack(valid).astype(jnp.int32)


def _sink_tables(sinks, bq):
    s3 = sinks.reshape(N_KV_HEADS, PAIRS, 2)
    rows = PAIRS * bq
    smax = jnp.broadcast_to(jnp.transpose(s3, (0, 2, 1))[:, :, :, None, None],
                            (N_KV_HEADS, 2, PAIRS, bq, LANES)).reshape(N_KV_HEADS, 2, rows, LANES)
    sl = jnp.broadcast_to(jnp.repeat(s3, HEAD_DIM, axis=-1)[:, :, None, :],
                          (N_KV_HEADS, PAIRS, bq, LANES)).reshape(N_KV_HEADS, rows, LANES)
    return smax, sl


def kernel(x_prompt, x_sample, cache_attn_k, cache_attn_v, rel_bias_table, ln_mix_g, w_in,
           q_norm_g, k_norm_g, attn_sinks, sgu_norm_g, sgu_w, sgu_b, out_norm_attn_g,
           out_norm_sgu_g, w_out, ln_ffn_g, w_ffn_up, w_ffn_down):
    bp, sp, _ = x_prompt.shape
    bs, ss, _ = x_sample.shape
    depth = w_in.shape[0]
    assert depth == 1 and ss == CHUNK and cache_attn_k.shape[2] == WINDOW
    l = 0
    xp = x_prompt.reshape(bp * sp, D_MODEL)
    xs = x_sample.reshape(bs * ss, D_MODEL)

    bq, units = 2 * CHUNK, 4
    bias_p, bias_s, win = _prep(rel_bias_table, *_band_tables(bq, 2), w_in[l], bq_s=ss)
    lng = ln_mix_g[l].reshape(1, D_MODEL)
    qg2 = jnp.tile(q_norm_g[l], 2).reshape(1, LANES)
    kg2 = jnp.tile(k_norm_g[l], 2).reshape(1, LANES)
    sg = sgu_norm_g[l].reshape(1, SGU_WIDTH)
    gs = out_norm_sgu_g[l].reshape(1, SGU_WIDTH)
    ga = out_norm_attn_g[l].reshape(1, ATTN_WIDTH)
    lnf = ln_ffn_g[l].reshape(1, D_MODEL)
    sinks = attn_sinks[l].reshape(N_HEADS)

    def sgu_params(c):
        return (sgu_w[l][:, :c, :c],
                jnp.broadcast_to(sgu_b[l][:, :c, None], (SGU_GROUPS, c, LANES)))

    ws_p, bs_p = sgu_params(SGU_CHUNK)
    q, k, v, mix_s, wup, wdn, wo = _proj(xp, lng, win, qg2, kg2, sg, ws_p, bs_p, gs,
                                         w_ffn_up[l], w_ffn_down[l], w_out[l],
                                         tm=512, chunk=SGU_CHUNK, emit_sv=False, tf=FFN_TF)
    k3 = k.reshape(bp, sp, KV_WIDTH)
    v3 = v.reshape(bp, sp, KV_WIDTH)
    prev = lambda b, i: (b, jnp.maximum(i * units - 1, 0), 0)
    cur = lambda b, i: (b, i, 0)
    x1p, r2p = _attn(q.reshape(bp, sp, ATTN_WIDTH),
                     [(k3, WINDOW), (k3, units * bq)], [(v3, WINDOW), (v3, units * bq)],
                     [prev, cur], [(0, 0, WINDOW), (1, 0, units * bq)],
                     bias_p, *_sink_tables(sinks, bq), ga,
                     x_prompt, mix_s.reshape(bp, sp, SGU_WIDTH), wo,
                     bq=bq, n_units=units, k_step=bq)
    yp = _ffn(x1p.reshape(bp * sp, D_MODEL), r2p.reshape(bp * sp, LANES), lnf, wup, wdn, tm=1024)

    ws_s, bs_s = sgu_params(ss)
    qs, ks, vs, mix_ss, svs = _proj(xs, lng, win, qg2, kg2, sg, ws_s, bs_s, gs,
                                    tm=512, chunk=ss, emit_sv=True)
    seqs = 4
    grp = lambda a, rows: a.reshape(bs // seqs, seqs * rows, a.shape[-1])
    layout = []
    for i in range(seqs):
        layout += [(0, i * WINDOW, WINDOW), (1, i * ss, ss), (-1, 0, BAND - WINDOW - ss)]
    x1s, r2s = _attn(grp(qs, ss),
                     [(grp(cache_attn_k[l].reshape(bs, WINDOW, KV_WIDTH), WINDOW), seqs * WINDOW),
                      (grp(ks, ss), seqs * ss)],
                     [(grp(cache_attn_v[l].reshape(bs, WINDOW, KV_WIDTH), WINDOW), seqs * WINDOW),
                      (grp(vs, ss), seqs * ss)],
                     [cur, cur], layout, bias_s,
                     *_sink_tables(sinks, ss), ga, grp(xs, ss), grp(mix_ss, ss), wo,
                     bq=ss, n_units=seqs, k_step=BAND)
    ys = _ffn(x1s.reshape(bs * ss, D_MODEL), r2s.reshape(bs * ss, LANES), lnf, wup, wdn, tm=1024)

    keep = min(WINDOW, sp)
    kv_shape = (N_KV_HEADS, HEAD_DIM)
    return (yp.reshape(bp, sp, D_MODEL),
            ys.reshape(bs, ss, D_MODEL),
            k3[:, -keep:].reshape(1, bp, keep, *kv_shape),
            v3[:, -keep:].reshape(1, bp, keep, *kv_shape),
            ks.reshape(1, bs, ss, *kv_shape),
            vs.reshape(1, bs, ss, *kv_shape),
            svs.reshape(1, bs, ss, SGU_WIDTH))
```

```python
import functools
import math

import jax
import jax.numpy as jnp
from jax import lax
from jax.experimental import pallas as pl
from jax.experimental.pallas import tpu as pltpu

D_MODEL = 2048
CHUNK = 64
ATTN_WIDTH = 1024
SGU_WIDTH = 1024
HEAD_DIM = 64
N_HEADS = ATTN_WIDTH // HEAD_DIM
N_KV_HEADS = 2
GQA_GROUP = N_HEADS // N_KV_HEADS
WINDOW = 128
N_BUCKETS = 32
MAX_DISTANCE = 128
SGU_CHUNK = 128
SGU_GROUPS = 8
SGU_GROUP_CH = SGU_WIDTH // SGU_GROUPS
D_FF = 4 * D_MODEL
KV_WIDTH = N_KV_HEADS * HEAD_DIM
IN_WIDTH = ATTN_WIDTH + 2 * KV_WIDTH + 2 * SGU_WIDTH
EPS = 1e-6
NEG_INF = -1e30

LANES = 128
VMEM_LIMIT = 56 * 1024 * 1024
FFN_VMEM_LIMIT = 62 * 1024 * 1024
FFN_TF = 1024
FFN_SUB = 1024

BF16 = jnp.bfloat16
F32 = jnp.float32

_KV0 = ATTN_WIDTH
_U0 = ATTN_WIDTH + 2 * KV_WIDTH
_V0 = _U0 + SGU_WIDTH


def _rms_rows(x, gain):
    ms = jnp.mean(x * x, axis=-1, keepdims=True)
    return x * lax.rsqrt(ms + EPS) * gain


def _head_pair_rms(blk, gain2, lane_lo):
    sq = blk * blk
    lo = jnp.sum(jnp.where(lane_lo, sq, 0.0), axis=-1, keepdims=True)
    hi = jnp.sum(jnp.where(lane_lo, 0.0, sq), axis=-1, keepdims=True)
    inv = 1.0 / HEAD_DIM
    r = jnp.where(lane_lo, lax.rsqrt(lo * inv + EPS), lax.rsqrt(hi * inv + EPS))
    return blk * r * gain2


def _cast_weights(wup_ref, wdn_ref, wo_ref, wupb_ref, wdnb_ref, wob_ref):
    tf = wupb_ref.shape[2]
    for j in range(wupb_ref.shape[0]):
        wupb_ref[j] = wup_ref[:, j * tf:(j + 1) * tf].astype(BF16)
    wdnb_ref[...] = wdn_ref[...].astype(BF16)
    wob_ref[...] = wo_ref[...].astype(BF16)


def _proj_kernel(chunk, emit_sv, cast_w, x_ref, lng_ref, win_ref, qg_ref, kg_ref, sg_ref,
                 ws_ref, bs_ref, *rest):
    if cast_w:
        _cast_weights(*rest[:3], *rest[-3:])
        rest = rest[3:-3]
    q_ref, k_ref, v_ref, gated_ref = rest[:4]
    tm = x_ref.shape[0]
    h = _rms_rows(x_ref[...], lng_ref[...]).astype(BF16)
    lane_lo = lax.broadcasted_iota(jnp.int32, (tm, LANES), 1) < HEAD_DIM

    av = jax.nn.gelu(jnp.dot(h, win_ref[:, _V0:IN_WIDTH], preferred_element_type=F32))
    sv = _rms_rows(av, sg_ref[...])
    if emit_sv:
        rest[4][...] = sv
    svb = sv.astype(BF16)
    u = jax.nn.gelu(jnp.dot(h, win_ref[:, _U0:_V0], preferred_element_type=F32))

    zq = jnp.dot(h, win_ref[:, 0:ATTN_WIDTH], preferred_element_type=F32)
    for c in range(ATTN_WIDTH // LANES):
        sl = slice(c * LANES, (c + 1) * LANES)
        q_ref[:, sl] = (_head_pair_rms(zq[:, sl], qg_ref[...], lane_lo)
                        * (HEAD_DIM ** -0.5)).astype(BF16)

    row_c = lax.broadcasted_iota(jnp.int32, (chunk, chunk), 0) // CHUNK
    col_c = lax.broadcasted_iota(jnp.int32, (chunk, chunk), 1) // CHUNK
    n_chunks = tm // chunk
    for g in range(SGU_GROUPS):
        cs = slice(g * SGU_GROUP_CH, (g + 1) * SGU_GROUP_CH)
        w = jnp.where(col_c <= row_c, ws_ref[g], 0.0).astype(BF16)
        rhs = jnp.concatenate(
            [svb[n * chunk:(n + 1) * chunk, cs] for n in range(n_chunks)], axis=1)
        sp = jnp.dot(w, rhs, preferred_element_type=F32)
        for n in range(n_chunks):
            rs = slice(n * chunk, (n + 1) * chunk)
            gated_ref[rs, cs] = u[rs, cs] * (sp[:, n * LANES:(n + 1) * LANES] + bs_ref[g])

    zkv = jnp.dot(h, win_ref[:, _KV0:_U0], preferred_element_type=F32)
    k_ref[...] = _head_pair_rms(zkv[:, 0:KV_WIDTH], kg_ref[...], lane_lo)
    v_ref[...] = zkv[:, KV_WIDTH:2 * KV_WIDTH]


def _proj(x2d, lng, win, qg2, kg2, sg, ws, bsb, wup=None, wdn=None, wo=None, *, tm, chunk,
          emit_sv, tf=None):
    t = x2d.shape[0]
    const = lambda shape: pl.BlockSpec(shape, lambda i: (0,) * len(shape),
                                       pipeline_mode=pl.Buffered(1))
    row = lambda w: pl.BlockSpec((tm, w), lambda i: (i, 0))
    out_shape = [jax.ShapeDtypeStruct((t, ATTN_WIDTH), BF16),
                 jax.ShapeDtypeStruct((t, KV_WIDTH), F32),
                 jax.ShapeDtypeStruct((t, KV_WIDTH), F32),
                 jax.ShapeDtypeStruct((t, SGU_WIDTH), F32)]
    out_specs = [row(ATTN_WIDTH), row(KV_WIDTH), row(KV_WIDTH), row(SGU_WIDTH)]
    if emit_sv:
        out_shape.append(jax.ShapeDtypeStruct((t, SGU_WIDTH), F32))
        out_specs.append(row(SGU_WIDTH))
    in_specs = [row(D_MODEL), const((1, D_MODEL)), const((D_MODEL, IN_WIDTH)),
                const((1, LANES)), const((1, LANES)), const((1, SGU_WIDTH)),
                const((SGU_GROUPS, chunk, chunk)), const((SGU_GROUPS, chunk, LANES))]
    args = [x2d, lng, win, qg2, kg2, sg, ws, bsb]
    if wup is not None:
        n_steps = t // tm
        up_rows, dn_rows, wo_rows = D_MODEL // n_steps, D_FF // n_steps, wo.shape[0] // n_steps
        in_specs += [pl.BlockSpec((up_rows, D_FF), lambda i: (i, 0)),
                     pl.BlockSpec((dn_rows, D_MODEL), lambda i: (i, 0)),
                     pl.BlockSpec((wo_rows, D_MODEL), lambda i: (i, 0))]
        out_specs += [pl.BlockSpec((D_FF // tf, up_rows, tf), lambda i: (0, i, 0)),
                      pl.BlockSpec((dn_rows, D_MODEL), lambda i: (i, 0)),
                      pl.BlockSpec((wo_rows, D_MODEL), lambda i: (i, 0))]
        out_shape += [jax.ShapeDtypeStruct((D_FF // tf, D_MODEL, tf), BF16),
                      jax.ShapeDtypeStruct((D_FF, D_MODEL), BF16),
                      jax.ShapeDtypeStruct(wo.shape, BF16)]
        args += [wup, wdn, wo]
    return pl.pallas_call(
        functools.partial(_proj_kernel, chunk, emit_sv, wup is not None),
        grid=(t // tm,),
        in_specs=in_specs,
        out_specs=out_specs,
        out_shape=out_shape,
        compiler_params=pltpu.CompilerParams(dimension_semantics=("parallel",),
                                             vmem_limit_bytes=VMEM_LIMIT),
        name="proj_c%d" % chunk,
    )(*args)


BAND = 2 * LANES
PAIRS = GQA_GROUP // 2


def _prep_kernel(bq_s, tab_ref, bkt_ref, valid_ref, win_ref, bias_p_ref, bias_s_ref, winb_ref):
    winb_ref[...] = win_ref[...].astype(BF16)
    bkt = bkt_ref[...]
    for half in range(2):
        h = 2 * pl.program_id(0) + half
        acc = jnp.zeros(bkt.shape, F32)
        for b in range(N_BUCKETS):
            acc = jnp.where(bkt == b, tab_ref[b, h], acc)
        cs = slice(half * BAND, (half + 1) * BAND)
        for s in range(valid_ref.shape[0]):
            bias_p_ref[s, :, cs] = jnp.where(valid_ref[s] != 0, acc, NEG_INF)
        bias_s_ref[0, :, cs] = jnp.where(valid_ref[valid_ref.shape[0] - 1, 0:bq_s, :] != 0,
                                         acc[0:bq_s], NEG_INF)


def _prep(table, bucket, valid, win, *, bq_s):
    nsel, bq, _ = valid.shape
    n_steps = N_HEADS // 2
    slab = pl.BlockSpec((win.shape[0] // n_steps, win.shape[1]), lambda s: (s, 0))
    pair = lambda n, rows: pl.BlockSpec((n, None, rows, 2 * BAND),
                                        lambda s: (0, s // PAIRS, s % PAIRS, 0))
    return pl.pallas_call(
        functools.partial(_prep_kernel, bq_s),
        grid=(n_steps,),
        in_specs=[pl.BlockSpec(memory_space=pltpu.SMEM),
                  pl.BlockSpec(bucket.shape, lambda s: (0, 0)),
                  pl.BlockSpec(valid.shape, lambda s: (0, 0, 0)),
                  slab],
        out_specs=[pair(nsel, bq), pair(1, bq_s), slab],
        out_shape=[jax.ShapeDtypeStruct((nsel, N_KV_HEADS, PAIRS * bq, 2 * BAND), F32),
                   jax.ShapeDtypeStruct((1, N_KV_HEADS, PAIRS * bq_s, 2 * BAND), F32),
                   jax.ShapeDtypeStruct(win.shape, BF16)],
        compiler_params=pltpu.CompilerParams(dimension_semantics=("parallel",)),
        name="prep",
    )(table, bucket, valid, win)


def _t5_bucket(n):
    half = N_BUCKETS // 2
    max_exact = half // 2
    offset = jnp.where(n < 0, half, 0)
    a = jnp.abs(n)
    af = jnp.maximum(a, 1).astype(F32)
    large = max_exact + (jnp.log(af / max_exact) / math.log(MAX_DISTANCE / max_exact)
                         * (half - max_exact)).astype(jnp.int32)
    large = jnp.minimum(large, half - 1)
    return offset + jnp.where(a < max_exact, a, large)


def _attn_kernel(bq, n_units, k_step, nsel, n_kparts, k_layout, *refs):
    q_ref = refs[0]
    k_refs = refs[1:1 + n_kparts]
    v_refs = refs[1 + n_kparts:1 + 2 * n_kparts]
    (bias_ref, smax_ref, sl_ref, ga_ref, x_ref, gated_ref, gs_ref, wo_ref,
     o_ref, r_ref, kk_scr, vv_scr, ao_scr) = refs[1 + 2 * n_kparts:]
    rows = PAIRS * bq

    def split_heads(parts, scr):
        x = jnp.concatenate(
            [parts[p][r0:r0 + n, :] if p >= 0 else jnp.zeros((n, KV_WIDTH), F32)
             for p, r0, n in k_layout], axis=0)
        lo = lax.broadcasted_iota(jnp.int32, x.shape, 1) < HEAD_DIM
        xr = pltpu.roll(x, HEAD_DIM, 1)
        scr[0, 0] = jnp.where(lo, x, 0.0).astype(BF16)
        scr[0, 1] = jnp.where(lo, 0.0, xr).astype(BF16)
        scr[1, 0] = jnp.where(lo, xr, 0.0).astype(BF16)
        scr[1, 1] = jnp.where(lo, 0.0, x).astype(BF16)

    split_heads(k_refs, kk_scr)
    split_heads(v_refs, vv_scr)
    r_i = lax.broadcasted_iota(jnp.int32, (2 * BAND, LANES), 0)
    l_i = lax.broadcasted_iota(jnp.int32, (2 * BAND, LANES), 1)
    ones_blk = jnp.where((r_i < BAND) == (l_i < HEAD_DIM), 1.0, 0.0).astype(BF16)
    lane_lo = lax.broadcasted_iota(jnp.int32, (rows, LANES), 1) < HEAD_DIM
    first_block = pl.program_id(1) == 0

    for u in range(n_units):
        q0, k0 = u * bq, u * k_step
        sel = jnp.where(first_block, 0, 1) if (nsel > 1 and u == 0) else nsel - 1
        for g in range(N_KV_HEADS):
            cols = [slice((PAIRS * g + p) * LANES, (PAIRS * g + p + 1) * LANES)
                    for p in range(PAIRS)]
            qs = jnp.concatenate([q_ref[q0:q0 + bq, c] for c in cols], axis=0)
            k2 = jnp.concatenate([kk_scr[g, 0, k0:k0 + BAND, :],
                                  kk_scr[g, 1, k0:k0 + BAND, :]], axis=0)
            s = lax.dot_general(qs, k2, (((1,), (1,)), ((), ())),
                                preferred_element_type=F32) + bias_ref[sel, g]
            sa, sb = s[:, :BAND], s[:, BAND:]
            ma = jnp.max(jnp.maximum(jnp.maximum(sa[:, :LANES], sa[:, LANES:]), smax_ref[g, 0]),
                         axis=-1, keepdims=True)
            mb = jnp.max(jnp.maximum(jnp.maximum(sb[:, :LANES], sb[:, LANES:]), smax_ref[g, 1]),
                         axis=-1, keepdims=True)
            p = jnp.concatenate([jnp.exp(sa - ma), jnp.exp(sb - mb)], axis=1).astype(BF16)
            v2 = jnp.concatenate([vv_scr[g, 0, k0:k0 + BAND, :],
                                  vv_scr[g, 1, k0:k0 + BAND, :]], axis=0)
            ol = jnp.dot(p, jnp.concatenate([v2, ones_blk], axis=1), preferred_element_type=F32)
            denom = ol[:, LANES:] + jnp.exp(sl_ref[g] - jnp.where(lane_lo, ma, mb))
            out = ol[:, :LANES] / denom
            for p_i, c in enumerate(cols):
                ao_scr[q0:q0 + bq, c] = out[p_i * bq:(p_i + 1) * bq]

    mix_a = _rms_rows(ao_scr[...], ga_ref[...]).astype(BF16)
    mix_s = _rms_rows(gated_ref[...], gs_ref[...]).astype(BF16)
    acc = jnp.dot(mix_s, wo_ref[ATTN_WIDTH:, :], preferred_element_type=F32)
    acc = acc + jnp.dot(mix_a, wo_ref[0:ATTN_WIDTH, :], preferred_element_type=F32)
    x1 = x_ref[...] + acc
    o_ref[...] = x1
    rinv = lax.rsqrt(jnp.mean(x1 * x1, axis=-1, keepdims=True) + EPS)
    r_ref[...] = jnp.broadcast_to(rinv * rinv, r_ref.shape)


def _attn(q, k_parts, v_parts, k_maps, k_layout, bias, smax, sl, ga, x, gated, gs, wo, *, bq,
          n_units, k_step):
    nb, s, _ = q.shape
    nsel = bias.shape[0]
    tq = n_units * bq
    k_rows = [rows for _, rows in k_parts]
    kr = sum(n for _, _, n in k_layout)
    cur = lambda b, i: (b, i, 0)
    row = lambda w: pl.BlockSpec((None, tq, w), cur)
    full = lambda a: pl.BlockSpec(a.shape, lambda b, i: (0,) * a.ndim,
                                  pipeline_mode=pl.Buffered(1))
    kv_specs = [pl.BlockSpec((None, r, KV_WIDTH), m) for r, m in zip(k_rows, k_maps)]
    return pl.pallas_call(
        functools.partial(_attn_kernel, bq, n_units, k_step, nsel, len(k_parts), tuple(k_layout)),
        grid=(nb, s // tq),
        in_specs=([row(ATTN_WIDTH)] + kv_specs + kv_specs
                  + [full(bias), full(smax), full(sl), full(ga),
                     row(D_MODEL), row(SGU_WIDTH), full(gs), full(wo)]),
        out_specs=[row(D_MODEL), row(LANES)],
        out_shape=[jax.ShapeDtypeStruct((nb, s, D_MODEL), F32),
                   jax.ShapeDtypeStruct((nb, s, LANES), F32)],
        scratch_shapes=[pltpu.VMEM((N_KV_HEADS, 2, kr, KV_WIDTH), BF16),
                        pltpu.VMEM((N_KV_HEADS, 2, kr, KV_WIDTH), BF16),
                        pltpu.VMEM((tq, ATTN_WIDTH), F32)],
        compiler_params=pltpu.CompilerParams(dimension_semantics=("parallel", "arbitrary"),
                                             vmem_limit_bytes=VMEM_LIMIT),
        name="attn_q%d" % bq,
    )(q, *[a for a, _ in k_parts], *[a for a, _ in v_parts], bias, smax, sl, ga, x, gated,
      gs, wo)


def _ffn_kernel(x_ref, r_ref, g_ref, wup_ref, wdn_ref, y_ref):
    tf = wup_ref.shape[1]
    xg = (x_ref[...] * g_ref[...]).astype(BF16)
    r2 = jnp.tile(r_ref[...], (1, FFN_SUB // LANES))
    for s in range(tf // FFN_SUB):
        cs = slice(s * FFN_SUB, (s + 1) * FFN_SUB)
        z = jnp.dot(xg, wup_ref[:, cs], preferred_element_type=F32)
        a = (jnp.square(jnp.maximum(z, 0.0)) * r2).astype(BF16)
        if s == 0:
            base = jnp.where(pl.program_id(1) == 0, x_ref[...], y_ref[...])
        else:
            base = y_ref[...]
        y_ref[...] = base + jnp.dot(a, wdn_ref[cs, :], preferred_element_type=F32)


def _ffn(x2d, r2d, g, wup_t, wdn, *, tm):
    t = x2d.shape[0]
    n_f, _, tf = wup_t.shape
    return pl.pallas_call(
        _ffn_kernel,
        grid=(t // tm, n_f),
        in_specs=[pl.BlockSpec((tm, D_MODEL), lambda i, j: (i, 0)),
                  pl.BlockSpec((tm, LANES), lambda i, j: (i, 0)),
                  pl.BlockSpec((1, D_MODEL), lambda i, j: (0, 0)),
                  pl.BlockSpec((None, D_MODEL, tf), lambda i, j: (j, 0, 0)),
                  pl.BlockSpec((tf, D_MODEL), lambda i, j: (j, 0))],
        out_specs=pl.BlockSpec((tm, D_MODEL), lambda i, j: (i, 0)),
        out_shape=jax.ShapeDtypeStruct((t, D_MODEL), F32),
        compiler_params=pltpu.CompilerParams(dimension_semantics=("parallel", "arbitrary"),
                                             vmem_limit_bytes=FFN_VMEM_LIMIT),
        name="ffn",
    )(x2d, r2d, g, wup_t, wdn)


def _band_tables(bq, nsel):
    qi = jnp.arange(bq)[:, None]
    kj = jnp.arange(BAND)[None, :] - WINDOW
    qc, kc = qi // CHUNK, jnp.floor_divide(kj, CHUNK)
    in_band = (kc <= qc) & (kc >= qc - WINDOW // CHUNK)
    valid = [in_band & (kj >= 0)] if nsel == 2 else []
    valid.append(in_band)
    return _t5_bucket(qi - kj), jnp.stack(valid).astype(jnp.int32)


def _sink_tables(sinks, bq):
    s3 = sinks.reshape(N_KV_HEADS, PAIRS, 2)
    rows = PAIRS * bq
    smax = jnp.broadcast_to(jnp.transpose(s3, (0, 2, 1))[:, :, :, None, None],
                            (N_KV_HEADS, 2, PAIRS, bq, LANES)).reshape(N_KV_HEADS, 2, rows, LANES)
    sl = jnp.broadcast_to(jnp.repeat(s3, HEAD_DIM, axis=-1)[:, :, None, :],
                          (N_KV_HEADS, PAIRS, bq, LANES)).reshape(N_KV_HEADS, rows, LANES)
    return smax, sl


def kernel(x_prompt, x_sample, cache_attn_k, cache_attn_v, rel_bias_table, ln_mix_g, w_in,
           q_norm_g, k_norm_g, attn_sinks, sgu_norm_g, sgu_w, sgu_b, out_norm_attn_g,
           out_norm_sgu_g, w_out, ln_ffn_g, w_ffn_up, w_ffn_down):
    bp, sp, _ = x_prompt.shape
    bs, ss, _ = x_sample.shape
    depth = w_in.shape[0]
    assert depth == 1 and ss == CHUNK and cache_attn_k.shape[2] == WINDOW
    l = 0
    xp = x_prompt.reshape(bp * sp, D_MODEL)
    xs = x_sample.reshape(bs * ss, D_MODEL)

    bq, units = 2 * CHUNK, 4
    bias_p, bias_s, win = _prep(rel_bias_table, *_band_tables(bq, 2), w_in[l], bq_s=ss)
    lng = ln_mix_g[l].reshape(1, D_MODEL)
    qg2 = jnp.tile(q_norm_g[l], 2).reshape(1, LANES)
    kg2 = jnp.tile(k_norm_g[l], 2).reshape(1, LANES)
    sg = sgu_norm_g[l].reshape(1, SGU_WIDTH)
    gs = out_norm_sgu_g[l].reshape(1, SGU_WIDTH)
    ga = out_norm_attn_g[l].reshape(1, ATTN_WIDTH)
    lnf = ln_ffn_g[l].reshape(1, D_MODEL)
    sinks = attn_sinks[l].reshape(N_HEADS)

    def sgu_params(c):
        return (sgu_w[l][:, :c, :c],
                jnp.broadcast_to(sgu_b[l][:, :c, None], (SGU_GROUPS, c, LANES)))

    ws_p, bs_p = sgu_params(SGU_CHUNK)
    q, k, v, gated, wup, wdn, wo = _proj(xp, lng, win, qg2, kg2, sg, ws_p, bs_p,
                                         w_ffn_up[l], w_ffn_down[l], w_out[l],
                                         tm=512, chunk=SGU_CHUNK, emit_sv=False, tf=FFN_TF)
    k3 = k.reshape(bp, sp, KV_WIDTH)
    v3 = v.reshape(bp, sp, KV_WIDTH)
    prev = lambda b, i: (b, jnp.maximum(i * units - 1, 0), 0)
    cur = lambda b, i: (b, i, 0)
    x1p, r2p = _attn(q.reshape(bp, sp, ATTN_WIDTH),
                     [(k3, WINDOW), (k3, units * bq)], [(v3, WINDOW), (v3, units * bq)],
                     [prev, cur], [(0, 0, WINDOW), (1, 0, units * bq)],
                     bias_p, *_sink_tables(sinks, bq), ga,
                     x_prompt, gated.reshape(bp, sp, SGU_WIDTH), gs, wo,
                     bq=bq, n_units=units, k_step=bq)
    yp = _ffn(x1p.reshape(bp * sp, D_MODEL), r2p.reshape(bp * sp, LANES), lnf, wup, wdn, tm=1024)

    ws_s, bs_s = sgu_params(ss)
    qs, ks, vs, gated_s, svs = _proj(xs, lng, win, qg2, kg2, sg, ws_s, bs_s,
                                    tm=512, chunk=ss, emit_sv=True)
    seqs = 4
    grp = lambda a, rows: a.reshape(bs // seqs, seqs * rows, a.shape[-1])
    layout = []
    for i in range(seqs):
        layout += [(0, i * WINDOW, WINDOW), (1, i * ss, ss), (-1, 0, BAND - WINDOW - ss)]
    x1s, r2s = _attn(grp(qs, ss),
                     [(grp(cache_attn_k[l].reshape(bs, WINDOW, KV_WIDTH), WINDOW), seqs * WINDOW),
                      (grp(ks, ss), seqs * ss)],
                     [(grp(cache_attn_v[l].reshape(bs, WINDOW, KV_WIDTH), WINDOW), seqs * WINDOW),
                      (grp(vs, ss), seqs * ss)],
                     [cur, cur], layout, bias_s,
                     *_sink_tables(sinks, ss), ga, grp(xs, ss), grp(gated_s, ss), gs, wo,
                     bq=ss, n_units=seqs, k_step=BAND)
    ys = _ffn(x1s.reshape(bs * ss, D_MODEL), r2s.reshape(bs * ss, LANES), lnf, wup, wdn, tm=1024)

    keep = min(WINDOW, sp)
    kv_shape = (N_KV_HEADS, HEAD_DIM)
    return (yp.reshape(bp, sp, D_MODEL),
            ys.reshape(bs, ss, D_MODEL),
            k3[:, -keep:].reshape(1, bp, keep, *kv_shape),
            v3[:, -keep:].reshape(1, bp, keep, *kv_shape),
            ks.reshape(1, bs, ss, *kv_shape),
            vs.reshape(1, bs, ss, *kv_shape),
            svs.reshape(1, bs, ss, SGU_WIDTH))
```

```python
import functools
import math

import jax
import jax.numpy as jnp
from jax import lax
from jax.experimental import pallas as pl
from jax.experimental.pallas import tpu as pltpu

D_MODEL = 2048
CHUNK = 64
ATTN_WIDTH = 1024
SGU_WIDTH = 1024
HEAD_DIM = 64
N_HEADS = ATTN_WIDTH // HEAD_DIM
N_KV_HEADS = 2
GQA_GROUP = N_HEADS // N_KV_HEADS
WINDOW = 128
N_BUCKETS = 32
MAX_DISTANCE = 128
SGU_CHUNK = 128
SGU_GROUPS = 8
SGU_GROUP_CH = SGU_WIDTH // SGU_GROUPS
D_FF = 4 * D_MODEL
KV_WIDTH = N_KV_HEADS * HEAD_DIM
IN_WIDTH = ATTN_WIDTH + 2 * KV_WIDTH + 2 * SGU_WIDTH
EPS = 1e-6
NEG_INF = -1e30

LANES = 128
VMEM_LIMIT = 56 * 1024 * 1024
FFN_VMEM_LIMIT = 62 * 1024 * 1024
FFN_TF = 1024
FFN_SUB = 1024

BF16 = jnp.bfloat16
F32 = jnp.float32

_KV0 = ATTN_WIDTH
_U0 = ATTN_WIDTH + 2 * KV_WIDTH
_V0 = _U0 + SGU_WIDTH


def _rms_rows(x, gain):
    ms = jnp.mean(x * x, axis=-1, keepdims=True)
    return x * lax.rsqrt(ms + EPS) * gain


def _head_pair_rms(blk, gain2, lane_lo):
    sq = blk * blk
    lo = jnp.sum(jnp.where(lane_lo, sq, 0.0), axis=-1, keepdims=True)
    hi = jnp.sum(jnp.where(lane_lo, 0.0, sq), axis=-1, keepdims=True)
    inv = 1.0 / HEAD_DIM
    r = jnp.where(lane_lo, lax.rsqrt(lo * inv + EPS), lax.rsqrt(hi * inv + EPS))
    return blk * r * gain2


def _cast_weights(wup_ref, wdn_ref, wo_ref, wupb_ref, wdnb_ref, wob_ref):
    tf = wupb_ref.shape[2]
    for j in range(wupb_ref.shape[0]):
        wupb_ref[j] = wup_ref[:, j * tf:(j + 1) * tf].astype(BF16)
    wdnb_ref[...] = wdn_ref[...].astype(BF16)
    wob_ref[...] = wo_ref[...].astype(BF16)


def _proj_kernel(chunk, emit_sv, cast_w, x_ref, lng_ref, win_ref, qg_ref, kg_ref, sg_ref,
                 ws_ref, bs_ref, *rest):
    if cast_w:
        _cast_weights(*rest[:3], *rest[-3:])
        rest = rest[3:-3]
    q_ref, k_ref, v_ref, gated_ref = rest[:4]
    tm = x_ref.shape[0]
    h = _rms_rows(x_ref[...], lng_ref[...]).astype(BF16)
    lane_lo = lax.broadcasted_iota(jnp.int32, (tm, LANES), 1) < HEAD_DIM
    qg2 = jnp.concatenate([qg_ref[...], qg_ref[...]], axis=1)
    kg2 = jnp.concatenate([kg_ref[...], kg_ref[...]], axis=1)

    av = jax.nn.gelu(jnp.dot(h, win_ref[:, _V0:IN_WIDTH], preferred_element_type=F32))
    sv = _rms_rows(av, sg_ref[...])
    if emit_sv:
        rest[4][...] = sv
    svb = sv.astype(BF16)
    u = jax.nn.gelu(jnp.dot(h, win_ref[:, _U0:_V0], preferred_element_type=F32))

    zq = jnp.dot(h, win_ref[:, 0:ATTN_WIDTH], preferred_element_type=F32)
    for c in range(ATTN_WIDTH // LANES):
        sl = slice(c * LANES, (c + 1) * LANES)
        q_ref[:, sl] = (_head_pair_rms(zq[:, sl], qg2, lane_lo)
                        * (HEAD_DIM ** -0.5)).astype(BF16)

    row_c = lax.broadcasted_iota(jnp.int32, (chunk, chunk), 0) // CHUNK
    col_c = lax.broadcasted_iota(jnp.int32, (chunk, chunk), 1) // CHUNK
    n_chunks = tm // chunk
    for g in range(SGU_GROUPS):
        cs = slice(g * SGU_GROUP_CH, (g + 1) * SGU_GROUP_CH)
        w = jnp.where(col_c <= row_c, ws_ref[g], 0.0).astype(BF16)
        rhs = jnp.concatenate(
            [svb[n * chunk:(n + 1) * chunk, cs] for n in range(n_chunks)], axis=1)
        sp = jnp.dot(w, rhs, preferred_element_type=F32)
        for n in range(n_chunks):
            rs = slice(n * chunk, (n + 1) * chunk)
            gated_ref[rs, cs] = u[rs, cs] * (sp[:, n * LANES:(n + 1) * LANES] + bs_ref[g])

    zkv = jnp.dot(h, win_ref[:, _KV0:_U0], preferred_element_type=F32)
    k_ref[...] = _head_pair_rms(zkv[:, 0:KV_WIDTH], kg2, lane_lo)
    v_ref[...] = zkv[:, KV_WIDTH:2 * KV_WIDTH]


def _proj(x2d, lng, win, qg2, kg2, sg, ws, bsb, wup=None, wdn=None, wo=None, *, tm, chunk,
          emit_sv, tf=None):
    t = x2d.shape[0]
    const = lambda shape: pl.BlockSpec(shape, lambda i: (0,) * len(shape),
                                       pipeline_mode=pl.Buffered(1))
    row = lambda w: pl.BlockSpec((tm, w), lambda i: (i, 0))
    out_shape = [jax.ShapeDtypeStruct((t, ATTN_WIDTH), BF16),
                 jax.ShapeDtypeStruct((t, KV_WIDTH), F32),
                 jax.ShapeDtypeStruct((t, KV_WIDTH), F32),
                 jax.ShapeDtypeStruct((t, SGU_WIDTH), F32)]
    out_specs = [row(ATTN_WIDTH), row(KV_WIDTH), row(KV_WIDTH), row(SGU_WIDTH)]
    if emit_sv:
        out_shape.append(jax.ShapeDtypeStruct((t, SGU_WIDTH), F32))
        out_specs.append(row(SGU_WIDTH))
    in_specs = [row(D_MODEL), const((1, D_MODEL)), const((D_MODEL, IN_WIDTH)),
                const((1, HEAD_DIM)), const((1, HEAD_DIM)), const((1, SGU_WIDTH)),
                const((SGU_GROUPS, chunk, chunk)), const((SGU_GROUPS, chunk, LANES))]
    args = [x2d, lng, win, qg2, kg2, sg, ws, bsb]
    if wup is not None:
        n_steps = t // tm
        up_rows, dn_rows, wo_rows = D_MODEL // n_steps, D_FF // n_steps, wo.shape[0] // n_steps
        in_specs += [pl.BlockSpec((up_rows, D_FF), lambda i: (i, 0)),
                     pl.BlockSpec((dn_rows, D_MODEL), lambda i: (i, 0)),
                     pl.BlockSpec((wo_rows, D_MODEL), lambda i: (i, 0))]
        out_specs += [pl.BlockSpec((D_FF // tf, up_rows, tf), lambda i: (0, i, 0)),
                      pl.BlockSpec((dn_rows, D_MODEL), lambda i: (i, 0)),
                      pl.BlockSpec((wo_rows, D_MODEL), lambda i: (i, 0))]
        out_shape += [jax.ShapeDtypeStruct((D_FF // tf, D_MODEL, tf), BF16),
                      jax.ShapeDtypeStruct((D_FF, D_MODEL), BF16),
                      jax.ShapeDtypeStruct(wo.shape, BF16)]
        args += [wup, wdn, wo]
    return pl.pallas_call(
        functools.partial(_proj_kernel, chunk, emit_sv, wup is not None),
        grid=(t // tm,),
        in_specs=in_specs,
        out_specs=out_specs,
        out_shape=out_shape,
        compiler_params=pltpu.CompilerParams(dimension_semantics=("parallel",),
                                             vmem_limit_bytes=VMEM_LIMIT),
        name="proj_c%d" % chunk,
    )(*args)


BAND = 2 * LANES
PAIRS = GQA_GROUP // 2


def _prep_kernel(bq_s, tab_ref, sink_ref, bkt_ref, valid_ref, win_ref, bias_p_ref, bias_s_ref,
                 smax_p_ref, sl_p_ref, smax_s_ref, sl_s_ref, winb_ref):
    winb_ref[...] = win_ref[...].astype(BF16)
    bkt = bkt_ref[...]
    h0 = 2 * pl.program_id(0)
    for smax_ref, sl_ref in ((smax_p_ref, sl_p_ref), (smax_s_ref, sl_s_ref)):
        lane_lo = lax.broadcasted_iota(jnp.int32, sl_ref.shape, 1) < HEAD_DIM
        sl_ref[...] = jnp.where(lane_lo, sink_ref[h0], sink_ref[h0 + 1])
        for half in range(2):
            smax_ref[half] = jnp.full(sl_ref.shape, sink_ref[h0 + half], F32)
    for half in range(2):
        h = h0 + half
        acc = jnp.zeros(bkt.shape, F32)
        for b in range(N_BUCKETS):
            acc = jnp.where(bkt == b, tab_ref[b, h], acc)
        cs = slice(half * BAND, (half + 1) * BAND)
        for s in range(valid_ref.shape[0]):
            bias_p_ref[s, :, cs] = jnp.where(valid_ref[s] != 0, acc, NEG_INF)
        bias_s_ref[0, :, cs] = jnp.where(valid_ref[valid_ref.shape[0] - 1, 0:bq_s, :] != 0,
                                         acc[0:bq_s], NEG_INF)


def _prep(table, sinks, bucket, valid, win, *, bq_s):
    nsel, bq, _ = valid.shape
    n_steps = N_HEADS // 2
    slab = pl.BlockSpec((win.shape[0] // n_steps, win.shape[1]), lambda s: (s, 0))
    pair = lambda n, rows: pl.BlockSpec((n, None, rows, 2 * BAND),
                                        lambda s: (0, s // PAIRS, s % PAIRS, 0))
    smax = lambda rows: pl.BlockSpec((None, 2, rows, LANES), lambda s: (s // PAIRS, 0, s % PAIRS, 0))
    sl = lambda rows: pl.BlockSpec((None, rows, LANES), lambda s: (s // PAIRS, s % PAIRS, 0))
    tables = lambda rows: [jax.ShapeDtypeStruct((N_KV_HEADS, 2, PAIRS * rows, LANES), F32),
                           jax.ShapeDtypeStruct((N_KV_HEADS, PAIRS * rows, LANES), F32)]
    return pl.pallas_call(
        functools.partial(_prep_kernel, bq_s),
        grid=(n_steps,),
        in_specs=[pl.BlockSpec(memory_space=pltpu.SMEM),
                  pl.BlockSpec(memory_space=pltpu.SMEM),
                  pl.BlockSpec(bucket.shape, lambda s: (0, 0)),
                  pl.BlockSpec(valid.shape, lambda s: (0, 0, 0)),
                  slab],
        out_specs=[pair(nsel, bq), pair(1, bq_s), smax(bq), sl(bq), smax(bq_s), sl(bq_s), slab],
        out_shape=([jax.ShapeDtypeStruct((nsel, N_KV_HEADS, PAIRS * bq, 2 * BAND), F32),
                    jax.ShapeDtypeStruct((1, N_KV_HEADS, PAIRS * bq_s, 2 * BAND), F32)]
                   + tables(bq) + tables(bq_s) + [jax.ShapeDtypeStruct(win.shape, BF16)]),
        compiler_params=pltpu.CompilerParams(dimension_semantics=("parallel",)),
        name="prep",
    )(table, sinks, bucket, valid, win)


def _t5_bucket(n):
    half = N_BUCKETS // 2
    max_exact = half // 2
    offset = jnp.where(n < 0, half, 0)
    a = jnp.abs(n)
    af = jnp.maximum(a, 1).astype(F32)
    large = max_exact + (jnp.log(af / max_exact) / math.log(MAX_DISTANCE / max_exact)
                         * (half - max_exact)).astype(jnp.int32)
    large = jnp.minimum(large, half - 1)
    return offset + jnp.where(a < max_exact, a, large)


def _attn_kernel(bq, n_units, k_step, nsel, n_kparts, k_layout, *refs):
    q_ref = refs[0]
    k_refs = refs[1:1 + n_kparts]
    v_refs = refs[1 + n_kparts:1 + 2 * n_kparts]
    (bias_ref, smax_ref, sl_ref, ga_ref, x_ref, gated_ref, gs_ref, wo_ref,
     o_ref, r_ref, kk_scr, vv_scr, ao_scr) = refs[1 + 2 * n_kparts:]
    rows = PAIRS * bq

    def split_heads(parts, scr):
        x = jnp.concatenate(
            [parts[p][r0:r0 + n, :] if p >= 0 else jnp.zeros((n, KV_WIDTH), F32)
             for p, r0, n in k_layout], axis=0)
        lo = lax.broadcasted_iota(jnp.int32, x.shape, 1) < HEAD_DIM
        xr = pltpu.roll(x, HEAD_DIM, 1)
        scr[0, 0] = jnp.where(lo, x, 0.0).astype(BF16)
        scr[0, 1] = jnp.where(lo, 0.0, xr).astype(BF16)
        scr[1, 0] = jnp.where(lo, xr, 0.0).astype(BF16)
        scr[1, 1] = jnp.where(lo, 0.0, x).astype(BF16)

    split_heads(k_refs, kk_scr)
    split_heads(v_refs, vv_scr)
    r_i = lax.broadcasted_iota(jnp.int32, (2 * BAND, LANES), 0)
    l_i = lax.broadcasted_iota(jnp.int32, (2 * BAND, LANES), 1)
    ones_blk = jnp.where((r_i < BAND) == (l_i < HEAD_DIM), 1.0, 0.0).astype(BF16)
    lane_lo = lax.broadcasted_iota(jnp.int32, (rows, LANES), 1) < HEAD_DIM
    first_block = pl.program_id(1) == 0

    for u in range(n_units):
        q0, k0 = u * bq, u * k_step
        sel = jnp.where(first_block, 0, 1) if (nsel > 1 and u == 0) else nsel - 1
        for g in range(N_KV_HEADS):
            cols = [slice((PAIRS * g + p) * LANES, (PAIRS * g + p + 1) * LANES)
                    for p in range(PAIRS)]
            qs = jnp.concatenate([q_ref[q0:q0 + bq, c] for c in cols], axis=0)
            k2 = jnp.concatenate([kk_scr[g, 0, k0:k0 + BAND, :],
                                  kk_scr[g, 1, k0:k0 + BAND, :]], axis=0)
            s = lax.dot_general(qs, k2, (((1,), (1,)), ((), ())),
                                preferred_element_type=F32) + bias_ref[sel, g]
            sa, sb = s[:, :BAND], s[:, BAND:]
            ma = jnp.max(jnp.maximum(jnp.maximum(sa[:, :LANES], sa[:, LANES:]), smax_ref[g, 0]),
                         axis=-1, keepdims=True)
            mb = jnp.max(jnp.maximum(jnp.maximum(sb[:, :LANES], sb[:, LANES:]), smax_ref[g, 1]),
                         axis=-1, keepdims=True)
            p = jnp.concatenate([jnp.exp(sa - ma), jnp.exp(sb - mb)], axis=1).astype(BF16)
            v2 = jnp.concatenate([vv_scr[g, 0, k0:k0 + BAND, :],
                                  vv_scr[g, 1, k0:k0 + BAND, :]], axis=0)
            ol = jnp.dot(p, jnp.concatenate([v2, ones_blk], axis=1), preferred_element_type=F32)
            denom = ol[:, LANES:] + jnp.exp(sl_ref[g] - jnp.where(lane_lo, ma, mb))
            out = ol[:, :LANES] / denom
            for p_i, c in enumerate(cols):
                ao_scr[q0:q0 + bq, c] = out[p_i * bq:(p_i + 1) * bq]

    mix_a = _rms_rows(ao_scr[...], ga_ref[...]).astype(BF16)
    mix_s = _rms_rows(gated_ref[...], gs_ref[...]).astype(BF16)
    acc = jnp.dot(mix_s, wo_ref[ATTN_WIDTH:, :], preferred_element_type=F32)
    acc = acc + jnp.dot(mix_a, wo_ref[0:ATTN_WIDTH, :], preferred_element_type=F32)
    x1 = x_ref[...] + acc
    o_ref[...] = x1
    rinv = lax.rsqrt(jnp.mean(x1 * x1, axis=-1, keepdims=True) + EPS)
    r_ref[...] = jnp.broadcast_to(rinv * rinv, r_ref.shape)


def _attn(q, k_parts, v_parts, k_maps, k_layout, bias, smax, sl, ga, x, gated, gs, wo, *, bq,
          n_units, k_step):
    nb, s, _ = q.shape
    nsel = bias.shape[0]
    tq = n_units * bq
    k_rows = [rows for _, rows in k_parts]
    kr = sum(n for _, _, n in k_layout)
    cur = lambda b, i: (b, i, 0)
    row = lambda w: pl.BlockSpec((None, tq, w), cur)
    full = lambda a: pl.BlockSpec(a.shape, lambda b, i: (0,) * a.ndim,
                                  pipeline_mode=pl.Buffered(1))
    kv_specs = [pl.BlockSpec((None, r, KV_WIDTH), m) for r, m in zip(k_rows, k_maps)]
    return pl.pallas_call(
        functools.partial(_attn_kernel, bq, n_units, k_step, nsel, len(k_parts), tuple(k_layout)),
        grid=(nb, s // tq),
        in_specs=([row(ATTN_WIDTH)] + kv_specs + kv_specs
                  + [full(bias), full(smax), full(sl), full(ga),
                     row(D_MODEL), row(SGU_WIDTH), full(gs), full(wo)]),
        out_specs=[row(D_MODEL), row(LANES)],
        out_shape=[jax.ShapeDtypeStruct((nb, s, D_MODEL), F32),
                   jax.ShapeDtypeStruct((nb, s, LANES), F32)],
        scratch_shapes=[pltpu.VMEM((N_KV_HEADS, 2, kr, KV_WIDTH), BF16),
                        pltpu.VMEM((N_KV_HEADS, 2, kr, KV_WIDTH), BF16),
                        pltpu.VMEM((tq, ATTN_WIDTH), F32)],
        compiler_params=pltpu.CompilerParams(dimension_semantics=("parallel", "arbitrary"),
                                             vmem_limit_bytes=VMEM_LIMIT),
        name="attn_q%d" % bq,
    )(q, *[a for a, _ in k_parts], *[a for a, _ in v_parts], bias, smax, sl, ga, x, gated,
      gs, wo)


def _ffn_kernel(x_ref, r_ref, g_ref, wup_ref, wdn_ref, y_ref):
    tf = wup_ref.shape[1]
    xg = (x_ref[...] * g_ref[...]).astype(BF16)
    r2 = jnp.tile(r_ref[...], (1, FFN_SUB // LANES))
    for s in range(tf // FFN_SUB):
        cs = slice(s * FFN_SUB, (s + 1) * FFN_SUB)
        z = jnp.dot(xg, wup_ref[:, cs], preferred_element_type=F32)
        a = (jnp.square(jnp.maximum(z, 0.0)) * r2).astype(BF16)
        if s == 0:
            base = jnp.where(pl.program_id(1) == 0, x_ref[...], y_ref[...])
        else:
            base = y_ref[...]
        y_ref[...] = base + jnp.dot(a, wdn_ref[cs, :], preferred_element_type=F32)


def _ffn(x2d, r2d, g, wup_t, wdn, *, tm):
    t = x2d.shape[0]
    n_f, _, tf = wup_t.shape
    return pl.pallas_call(
        _ffn_kernel,
        grid=(t // tm, n_f),
        in_specs=[pl.BlockSpec((tm, D_MODEL), lambda i, j: (i, 0)),
                  pl.BlockSpec((tm, LANES), lambda i, j: (i, 0)),
                  pl.BlockSpec((1, D_MODEL), lambda i, j: (0, 0)),
                  pl.BlockSpec((None, D_MODEL, tf), lambda i, j: (j, 0, 0)),
                  pl.BlockSpec((tf, D_MODEL), lambda i, j: (j, 0))],
        out_specs=pl.BlockSpec((tm, D_MODEL), lambda i, j: (i, 0)),
        out_shape=jax.ShapeDtypeStruct((t, D_MODEL), F32),
        compiler_params=pltpu.CompilerParams(dimension_semantics=("parallel", "arbitrary"),
                                             vmem_limit_bytes=FFN_VMEM_LIMIT),
        name="ffn",
    )(x2d, r2d, g, wup_t, wdn)


def _band_tables(bq, nsel):
    qi = jnp.arange(bq)[:, None]
    kj = jnp.arange(BAND)[None, :] - WINDOW
    qc, kc = qi // CHUNK, jnp.floor_divide(kj, CHUNK)
    in_band = (kc <= qc) & (kc >= qc - WINDOW // CHUNK)
    valid = [in_band & (kj >= 0)] if nsel == 2 else []
    valid.append(in_band)
    return _t5_bucket(qi - kj), jnp.stack(valid).astype(jnp.int32)


def kernel(x_prompt, x_sample, cache_attn_k, cache_attn_v, rel_bias_table, ln_mix_g, w_in,
           q_norm_g, k_norm_g, attn_sinks, sgu_norm_g, sgu_w, sgu_b, out_norm_attn_g,
           out_norm_sgu_g, w_out, ln_ffn_g, w_ffn_up, w_ffn_down):
    bp, sp, _ = x_prompt.shape
    bs, ss, _ = x_sample.shape
    depth = w_in.shape[0]
    assert depth == 1 and ss == CHUNK and cache_attn_k.shape[2] == WINDOW
    l = 0
    xp = x_prompt.reshape(bp * sp, D_MODEL)
    xs = x_sample.reshape(bs * ss, D_MODEL)

    bq, units = 2 * CHUNK, 4
    sinks = attn_sinks[l].reshape(N_HEADS)
    bias_p, bias_s, smax_p, sl_p, smax_s, sl_s, win = _prep(
        rel_bias_table, sinks, *_band_tables(bq, 2), w_in[l], bq_s=ss)
    lng = ln_mix_g[l].reshape(1, D_MODEL)
    qg2 = q_norm_g[l].reshape(1, HEAD_DIM)
    kg2 = k_norm_g[l].reshape(1, HEAD_DIM)
    sg = sgu_norm_g[l].reshape(1, SGU_WIDTH)
    gs = out_norm_sgu_g[l].reshape(1, SGU_WIDTH)
    ga = out_norm_attn_g[l].reshape(1, ATTN_WIDTH)
    lnf = ln_ffn_g[l].reshape(1, D_MODEL)

    def sgu_params(c):
        return (sgu_w[l][:, :c, :c],
                jnp.broadcast_to(sgu_b[l][:, :c, None], (SGU_GROUPS, c, LANES)))

    ws_p, bs_p = sgu_params(SGU_CHUNK)
    q, k, v, gated, wup, wdn, wo = _proj(xp, lng, win, qg2, kg2, sg, ws_p, bs_p,
                                         w_ffn_up[l], w_ffn_down[l], w_out[l],
                                         tm=512, chunk=SGU_CHUNK, emit_sv=False, tf=FFN_TF)
    k3 = k.reshape(bp, sp, KV_WIDTH)
    v3 = v.reshape(bp, sp, KV_WIDTH)
    prev = lambda b, i: (b, jnp.maximum(i * units - 1, 0), 0)
    cur = lambda b, i: (b, i, 0)
    x1p, r2p = _attn(q.reshape(bp, sp, ATTN_WIDTH),
                     [(k3, WINDOW), (k3, units * bq)], [(v3, WINDOW), (v3, units * bq)],
                     [prev, cur], [(0, 0, WINDOW), (1, 0, units * bq)],
                     bias_p, smax_p, sl_p, ga,
                     x_prompt, gated.reshape(bp, sp, SGU_WIDTH), gs, wo,
                     bq=bq, n_units=units, k_step=bq)
    yp = _ffn(x1p.reshape(bp * sp, D_MODEL), r2p.reshape(bp * sp, LANES), lnf, wup, wdn, tm=1024)

    ws_s, bs_s = sgu_params(ss)
    qs, ks, vs, gated_s, svs = _proj(xs, lng, win, qg2, kg2, sg, ws_s, bs_s,
                                    tm=512, chunk=ss, emit_sv=True)
    seqs = 4
    grp = lambda a, rows: a.reshape(bs // seqs, seqs * rows, a.shape[-1])
    layout = []
    for i in range(seqs):
        layout += [(0, i * WINDOW, WINDOW), (1, i * ss, ss), (-1, 0, BAND - WINDOW - ss)]
    x1s, r2s = _attn(grp(qs, ss),
                     [(grp(cache_attn_k[l].reshape(bs, WINDOW, KV_WIDTH), WINDOW), seqs * WINDOW),
                      (grp(ks, ss), seqs * ss)],
                     [(grp(cache_attn_v[l].reshape(bs, WINDOW, KV_WIDTH), WINDOW), seqs * WINDOW),
                      (grp(vs, ss), seqs * ss)],
                     [cur, cur], layout, bias_s,
                     smax_s, sl_s, ga, grp(xs, ss), grp(gated_s, ss), gs, wo,
                     bq=ss, n_units=seqs, k_step=BAND)
    ys = _ffn(x1s.reshape(bs * ss, D_MODEL), r2s.reshape(bs * ss, LANES), lnf, wup, wdn, tm=1024)

    keep = min(WINDOW, sp)
    kv_shape = (N_KV_HEADS, HEAD_DIM)
    return (yp.reshape(bp, sp, D_MODEL),
            ys.reshape(bs, ss, D_MODEL),
            k3[:, -keep:].reshape(1, bp, keep, *kv_shape),
            v3[:, -keep:].reshape(1, bp, keep, *kv_shape),
            ks.reshape(1, bs, ss, *kv_shape),
            vs.reshape(1, bs, ss, *kv_shape),
            svs.reshape(1, bs, ss, SGU_WIDTH))
```

```python
import functools
import math

import jax
import jax.numpy as jnp
from jax import lax
from jax.experimental import pallas as pl
from jax.experimental.pallas import tpu as pltpu

D_MODEL = 2048
CHUNK = 64
ATTN_WIDTH = 1024
SGU_WIDTH = 1024
HEAD_DIM = 64
N_HEADS = ATTN_WIDTH // HEAD_DIM
N_KV_HEADS = 2
GQA_GROUP = N_HEADS // N_KV_HEADS
WINDOW = 128
N_BUCKETS = 32
MAX_DISTANCE = 128
SGU_CHUNK = 128
SGU_GROUPS = 8
SGU_GROUP_CH = SGU_WIDTH // SGU_GROUPS
D_FF = 4 * D_MODEL
KV_WIDTH = N_KV_HEADS * HEAD_DIM
IN_WIDTH = ATTN_WIDTH + 2 * KV_WIDTH + 2 * SGU_WIDTH
EPS = 1e-6
NEG_INF = -1e30

LANES = 128
VMEM_LIMIT = 56 * 1024 * 1024
FFN_VMEM_LIMIT = 62 * 1024 * 1024
FFN_TF = 1024
FFN_SUB = 1024

BF16 = jnp.bfloat16
F32 = jnp.float32

_KV0 = ATTN_WIDTH
_U0 = ATTN_WIDTH + 2 * KV_WIDTH
_V0 = _U0 + SGU_WIDTH


def _rms_rows(x, gain):
    ms = jnp.mean(x * x, axis=-1, keepdims=True)
    return x * lax.rsqrt(ms + EPS) * gain


def _head_pair_rms(blk, gain2, lane_lo):
    sq = blk * blk
    lo = jnp.sum(jnp.where(lane_lo, sq, 0.0), axis=-1, keepdims=True)
    hi = jnp.sum(jnp.where(lane_lo, 0.0, sq), axis=-1, keepdims=True)
    inv = 1.0 / HEAD_DIM
    r = jnp.where(lane_lo, lax.rsqrt(lo * inv + EPS), lax.rsqrt(hi * inv + EPS))
    return blk * r * gain2


def _cast_weights(wup_ref, wdn_ref, wo_ref, wupb_ref, wdnb_ref, wob_ref):
    tf = wupb_ref.shape[2]
    for j in range(wupb_ref.shape[0]):
        wupb_ref[j] = wup_ref[:, j * tf:(j + 1) * tf].astype(BF16)
    wdnb_ref[...] = wdn_ref[...].astype(BF16)
    wob_ref[...] = wo_ref[...].astype(BF16)


def _proj_kernel(chunk, emit_sv, cast_w, x_ref, lng_ref, win_ref, qg_ref, kg_ref, sg_ref,
                 ws_ref, bs_ref, *rest):
    if cast_w:
        _cast_weights(*rest[:3], *rest[-3:])
        rest = rest[3:-3]
    q_ref, k_ref, v_ref, gated_ref = rest[:4]
    tm = x_ref.shape[0]
    h = _rms_rows(x_ref[...], lng_ref[...]).astype(BF16)
    lane_lo = lax.broadcasted_iota(jnp.int32, (tm, LANES), 1) < HEAD_DIM
    qg2 = jnp.concatenate([qg_ref[...], qg_ref[...]], axis=1)
    kg2 = jnp.concatenate([kg_ref[...], kg_ref[...]], axis=1)

    av = jax.nn.gelu(jnp.dot(h, win_ref[:, _V0:IN_WIDTH], preferred_element_type=F32))
    sv = _rms_rows(av, sg_ref[...])
    if emit_sv:
        rest[4][...] = sv
    svb = sv.astype(BF16)
    u = jax.nn.gelu(jnp.dot(h, win_ref[:, _U0:_V0], preferred_element_type=F32))

    zq = jnp.dot(h, win_ref[:, 0:ATTN_WIDTH], preferred_element_type=F32)
    for c in range(ATTN_WIDTH // LANES):
        sl = slice(c * LANES, (c + 1) * LANES)
        q_ref[:, sl] = (_head_pair_rms(zq[:, sl], qg2, lane_lo)
                        * (HEAD_DIM ** -0.5)).astype(BF16)

    row_c = lax.broadcasted_iota(jnp.int32, (chunk, chunk), 0) // CHUNK
    col_c = lax.broadcasted_iota(jnp.int32, (chunk, chunk), 1) // CHUNK
    n_chunks = tm // chunk
    for g in range(SGU_GROUPS):
        cs = slice(g * SGU_GROUP_CH, (g + 1) * SGU_GROUP_CH)
        w = jnp.where(col_c <= row_c, ws_ref[g, 0:chunk, 0:chunk], 0.0).astype(BF16)
        rhs = jnp.concatenate(
            [svb[n * chunk:(n + 1) * chunk, cs] for n in range(n_chunks)], axis=1)
        sp = jnp.dot(w, rhs, preferred_element_type=F32)
        for n in range(n_chunks):
            rs = slice(n * chunk, (n + 1) * chunk)
            gated_ref[rs, cs] = u[rs, cs] * (sp[:, n * LANES:(n + 1) * LANES]
                                             + bs_ref[g, 0:chunk, :])

    zkv = jnp.dot(h, win_ref[:, _KV0:_U0], preferred_element_type=F32)
    k_ref[...] = _head_pair_rms(zkv[:, 0:KV_WIDTH], kg2, lane_lo)
    v_ref[...] = zkv[:, KV_WIDTH:2 * KV_WIDTH]


def _proj(x2d, lng, win, qg2, kg2, sg, ws, bsb, wup=None, wdn=None, wo=None, *, tm, chunk,
          emit_sv, tf=None):
    t = x2d.shape[0]
    const = lambda shape: pl.BlockSpec(shape, lambda i: (0,) * len(shape),
                                       pipeline_mode=pl.Buffered(1))
    row = lambda w: pl.BlockSpec((tm, w), lambda i: (i, 0))
    out_shape = [jax.ShapeDtypeStruct((t, ATTN_WIDTH), BF16),
                 jax.ShapeDtypeStruct((t, KV_WIDTH), F32),
                 jax.ShapeDtypeStruct((t, KV_WIDTH), F32),
                 jax.ShapeDtypeStruct((t, SGU_WIDTH), F32)]
    out_specs = [row(ATTN_WIDTH), row(KV_WIDTH), row(KV_WIDTH), row(SGU_WIDTH)]
    if emit_sv:
        out_shape.append(jax.ShapeDtypeStruct((t, SGU_WIDTH), F32))
        out_specs.append(row(SGU_WIDTH))
    in_specs = [row(D_MODEL), const((1, D_MODEL)), const((D_MODEL, IN_WIDTH)),
                const((1, HEAD_DIM)), const((1, HEAD_DIM)), const((1, SGU_WIDTH)),
                const(ws.shape), const(bsb.shape)]
    args = [x2d, lng, win, qg2, kg2, sg, ws, bsb]
    if wup is not None:
        n_steps = t // tm
        up_rows, dn_rows, wo_rows = D_MODEL // n_steps, D_FF // n_steps, wo.shape[0] // n_steps
        in_specs += [pl.BlockSpec((up_rows, D_FF), lambda i: (i, 0)),
                     pl.BlockSpec((dn_rows, D_MODEL), lambda i: (i, 0)),
                     pl.BlockSpec((wo_rows, D_MODEL), lambda i: (i, 0))]
        out_specs += [pl.BlockSpec((D_FF // tf, up_rows, tf), lambda i: (0, i, 0)),
                      pl.BlockSpec((dn_rows, D_MODEL), lambda i: (i, 0)),
                      pl.BlockSpec((wo_rows, D_MODEL), lambda i: (i, 0))]
        out_shape += [jax.ShapeDtypeStruct((D_FF // tf, D_MODEL, tf), BF16),
                      jax.ShapeDtypeStruct((D_FF, D_MODEL), BF16),
                      jax.ShapeDtypeStruct(wo.shape, BF16)]
        args += [wup, wdn, wo]
    return pl.pallas_call(
        functools.partial(_proj_kernel, chunk, emit_sv, wup is not None),
        grid=(t // tm,),
        in_specs=in_specs,
        out_specs=out_specs,
        out_shape=out_shape,
        compiler_params=pltpu.CompilerParams(dimension_semantics=("parallel",),
                                             vmem_limit_bytes=VMEM_LIMIT),
        name="proj_c%d" % chunk,
    )(*args)


BAND = 2 * LANES
PAIRS = GQA_GROUP // 2


def _prep_kernel(bq_s, tab_ref, sink_ref, bkt_ref, valid_ref, sgub_ref, win_ref, bias_p_ref,
                 bias_s_ref, smax_p_ref, sl_p_ref, smax_s_ref, sl_s_ref, bsb_ref, winb_ref):
    winb_ref[...] = win_ref[...].astype(BF16)
    c = bsb_ref.shape[0]
    eye = (lax.broadcasted_iota(jnp.int32, (c, c), 0) == lax.broadcasted_iota(jnp.int32, (c, c), 1))
    b_row = sgub_ref[pl.ds(pl.program_id(0), 1), :]
    bsb_ref[...] = jnp.broadcast_to(
        jnp.sum(jnp.where(eye, b_row, 0.0), axis=1, keepdims=True), bsb_ref.shape)
    bkt = bkt_ref[...]
    h0 = 2 * pl.program_id(0)
    for smax_ref, sl_ref in ((smax_p_ref, sl_p_ref), (smax_s_ref, sl_s_ref)):
        lane_lo = lax.broadcasted_iota(jnp.int32, sl_ref.shape, 1) < HEAD_DIM
        sl_ref[...] = jnp.where(lane_lo, sink_ref[h0], sink_ref[h0 + 1])
        for half in range(2):
            smax_ref[half] = jnp.full(sl_ref.shape, sink_ref[h0 + half], F32)
    for half in range(2):
        h = h0 + half
        acc = jnp.zeros(bkt.shape, F32)
        for b in range(N_BUCKETS):
            acc = jnp.where(bkt == b, tab_ref[b, h], acc)
        cs = slice(half * BAND, (half + 1) * BAND)
        for s in range(valid_ref.shape[0]):
            bias_p_ref[s, :, cs] = jnp.where(valid_ref[s] != 0, acc, NEG_INF)
        bias_s_ref[0, :, cs] = jnp.where(valid_ref[valid_ref.shape[0] - 1, 0:bq_s, :] != 0,
                                         acc[0:bq_s], NEG_INF)


def _prep(table, sinks, bucket, valid, sgu_b, win, *, bq_s):
    nsel, bq, _ = valid.shape
    n_steps = N_HEADS // 2
    assert n_steps == SGU_GROUPS
    slab =pl.BlockSpec((win.shape[0] // n_steps, win.shape[1]), lambda s: (s, 0))
    pair = lambda n, rows: pl.BlockSpec((n, None, rows, 2 * BAND),
                                        lambda s: (0, s // PAIRS, s % PAIRS, 0))
    smax = lambda rows: pl.BlockSpec((None, 2, rows, LANES), lambda s: (s // PAIRS, 0, s % PAIRS, 0))
    sl = lambda rows: pl.BlockSpec((None, rows, LANES), lambda s: (s // PAIRS, s % PAIRS, 0))
    tables = lambda rows: [jax.ShapeDtypeStruct((N_KV_HEADS, 2, PAIRS * rows, LANES), F32),
                           jax.ShapeDtypeStruct((N_KV_HEADS, PAIRS * rows, LANES), F32)]
    return pl.pallas_call(
        functools.partial(_prep_kernel, bq_s),
        grid=(n_steps,),
        in_specs=[pl.BlockSpec(memory_space=pltpu.SMEM),
                  pl.BlockSpec(memory_space=pltpu.SMEM),
                  pl.BlockSpec(bucket.shape, lambda s: (0, 0)),
                  pl.BlockSpec(valid.shape, lambda s: (0, 0, 0)),
                  pl.BlockSpec(sgu_b.shape, lambda s: (0, 0)),
                  slab],
        out_specs=[pair(nsel, bq), pair(1, bq_s), smax(bq), sl(bq), smax(bq_s), sl(bq_s),
                   pl.BlockSpec((None, SGU_CHUNK, LANES), lambda s: (s, 0, 0)), slab],
        out_shape=([jax.ShapeDtypeStruct((nsel, N_KV_HEADS, PAIRS * bq, 2 * BAND), F32),
                    jax.ShapeDtypeStruct((1, N_KV_HEADS, PAIRS * bq_s, 2 * BAND), F32)]
                   + tables(bq) + tables(bq_s)
                   + [jax.ShapeDtypeStruct((SGU_GROUPS, SGU_CHUNK, LANES), F32),
                      jax.ShapeDtypeStruct(win.shape, BF16)]),
        compiler_params=pltpu.CompilerParams(dimension_semantics=("parallel",)),
        name="prep",
    )(table, sinks, bucket, valid, sgu_b, win)


def _t5_bucket(n):
    half = N_BUCKETS // 2
    max_exact = half // 2
    offset = jnp.where(n < 0, half, 0)
    a = jnp.abs(n)
    af = jnp.maximum(a, 1).astype(F32)
    large = max_exact + (jnp.log(af / max_exact) / math.log(MAX_DISTANCE / max_exact)
                         * (half - max_exact)).astype(jnp.int32)
    large = jnp.minimum(large, half - 1)
    return offset + jnp.where(a < max_exact, a, large)


def _attn_kernel(bq, n_units, k_step, nsel, n_kparts, k_layout, *refs):
    q_ref = refs[0]
    k_refs = refs[1:1 + n_kparts]
    v_refs = refs[1 + n_kparts:1 + 2 * n_kparts]
    (bias_ref, smax_ref, sl_ref, ga_ref, x_ref, gated_ref, gs_ref, wo_ref,
     o_ref, r_ref, kk_scr, vv_scr, ao_scr) = refs[1 + 2 * n_kparts:]
    rows = PAIRS * bq

    def split_heads(parts, scr):
        x = jnp.concatenate(
            [parts[p][r0:r0 + n, :] if p >= 0 else jnp.zeros((n, KV_WIDTH), F32)
             for p, r0, n in k_layout], axis=0)
        lo = lax.broadcasted_iota(jnp.int32, x.shape, 1) < HEAD_DIM
        xr = pltpu.roll(x, HEAD_DIM, 1)
        scr[0, 0] = jnp.where(lo, x, 0.0).astype(BF16)
        scr[0, 1] = jnp.where(lo, 0.0, xr).astype(BF16)
        scr[1, 0] = jnp.where(lo, xr, 0.0).astype(BF16)
        scr[1, 1] = jnp.where(lo, 0.0, x).astype(BF16)

    split_heads(k_refs, kk_scr)
    split_heads(v_refs, vv_scr)
    r_i = lax.broadcasted_iota(jnp.int32, (2 * BAND, LANES), 0)
    l_i = lax.broadcasted_iota(jnp.int32, (2 * BAND, LANES), 1)
    ones_blk = jnp.where((r_i < BAND) == (l_i < HEAD_DIM), 1.0, 0.0).astype(BF16)
    lane_lo = lax.broadcasted_iota(jnp.int32, (rows, LANES), 1) < HEAD_DIM
    first_block = pl.program_id(1) == 0

    for u in range(n_units):
        q0, k0 = u * bq, u * k_step
        sel = jnp.where(first_block, 0, 1) if (nsel > 1 and u == 0) else nsel - 1
        for g in range(N_KV_HEADS):
            cols = [slice((PAIRS * g + p) * LANES, (PAIRS * g + p + 1) * LANES)
                    for p in range(PAIRS)]
            qs = jnp.concatenate([q_ref[q0:q0 + bq, c] for c in cols], axis=0)
            k2 = jnp.concatenate([kk_scr[g, 0, k0:k0 + BAND, :],
                                  kk_scr[g, 1, k0:k0 + BAND, :]], axis=0)
            s = lax.dot_general(qs, k2, (((1,), (1,)), ((), ())),
                                preferred_element_type=F32) + bias_ref[sel, g]
            sa, sb = s[:, :BAND], s[:, BAND:]
            ma = jnp.max(jnp.maximum(jnp.maximum(sa[:, :LANES], sa[:, LANES:]), smax_ref[g, 0]),
                         axis=-1, keepdims=True)
            mb = jnp.max(jnp.maximum(jnp.maximum(sb[:, :LANES], sb[:, LANES:]), smax_ref[g, 1]),
                         axis=-1, keepdims=True)
            p = jnp.concatenate([jnp.exp(sa - ma), jnp.exp(sb - mb)], axis=1).astype(BF16)
            v2 = jnp.concatenate([vv_scr[g, 0, k0:k0 + BAND, :],
                                  vv_scr[g, 1, k0:k0 + BAND, :]], axis=0)
            ol = jnp.dot(p, jnp.concatenate([v2, ones_blk], axis=1), preferred_element_type=F32)
            denom = ol[:, LANES:] + jnp.exp(sl_ref[g] - jnp.where(lane_lo, ma, mb))
            out = ol[:, :LANES] / denom
            for p_i, c in enumerate(cols):
                ao_scr[q0:q0 + bq, c] = out[p_i * bq:(p_i + 1) * bq]

    mix_a = _rms_rows(ao_scr[...], ga_ref[...]).astype(BF16)
    mix_s = _rms_rows(gated_ref[...], gs_ref[...]).astype(BF16)
    acc = jnp.dot(mix_s, wo_ref[ATTN_WIDTH:, :], preferred_element_type=F32)
    acc = acc + jnp.dot(mix_a, wo_ref[0:ATTN_WIDTH, :], preferred_element_type=F32)
    x1 = x_ref[...] + acc
    o_ref[...] = x1
    rinv = lax.rsqrt(jnp.mean(x1 * x1, axis=-1, keepdims=True) + EPS)
    r_ref[...] = jnp.broadcast_to(rinv * rinv, r_ref.shape)


def _attn(q, k_parts, v_parts, k_maps, k_layout, bias, smax, sl, ga, x, gated, gs, wo, *, bq,
          n_units, k_step):
    nb, s, _ = q.shape
    nsel = bias.shape[0]
    tq = n_units * bq
    k_rows = [rows for _, rows in k_parts]
    kr = sum(n for _, _, n in k_layout)
    cur = lambda b, i: (b, i, 0)
    row = lambda w: pl.BlockSpec((None, tq, w), cur)
    full = lambda a: pl.BlockSpec(a.shape, lambda b, i: (0,) * a.ndim,
                                  pipeline_mode=pl.Buffered(1))
    kv_specs = [pl.BlockSpec((None, r, KV_WIDTH), m) for r, m in zip(k_rows, k_maps)]
    return pl.pallas_call(
        functools.partial(_attn_kernel, bq, n_units, k_step, nsel, len(k_parts), tuple(k_layout)),
        grid=(nb, s // tq),
        in_specs=([row(ATTN_WIDTH)] + kv_specs + kv_specs
                  + [full(bias), full(smax), full(sl), full(ga),
                     row(D_MODEL), row(SGU_WIDTH), full(gs), full(wo)]),
        out_specs=[row(D_MODEL), row(LANES)],
        out_shape=[jax.ShapeDtypeStruct((nb, s, D_MODEL), F32),
                   jax.ShapeDtypeStruct((nb, s, LANES), F32)],
        scratch_shapes=[pltpu.VMEM((N_KV_HEADS, 2, kr, KV_WIDTH), BF16),
                        pltpu.VMEM((N_KV_HEADS, 2, kr, KV_WIDTH), BF16),
                        pltpu.VMEM((tq, ATTN_WIDTH), F32)],
        compiler_params=pltpu.CompilerParams(dimension_semantics=("parallel", "arbitrary"),
                                             vmem_limit_bytes=VMEM_LIMIT),
        name="attn_q%d" % bq,
    )(q, *[a for a, _ in k_parts], *[a for a, _ in v_parts], bias, smax, sl, ga, x, gated,
      gs, wo)


def _ffn_kernel(x_ref, r_ref, g_ref, wup_ref, wdn_ref, y_ref):
    tf = wup_ref.shape[1]
    xg = (x_ref[...] * g_ref[...]).astype(BF16)
    r2 = jnp.tile(r_ref[...], (1, FFN_SUB // LANES))
    for s in range(tf // FFN_SUB):
        cs = slice(s * FFN_SUB, (s + 1) * FFN_SUB)
        z = jnp.dot(xg, wup_ref[:, cs], preferred_element_type=F32)
        a = (jnp.square(jnp.maximum(z, 0.0)) * r2).astype(BF16)
        if s == 0:
            base = jnp.where(pl.program_id(1) == 0, x_ref[...], y_ref[...])
        else:
            base = y_ref[...]
        y_ref[...] = base + jnp.dot(a, wdn_ref[cs, :], preferred_element_type=F32)


def _ffn(x2d, r2d, g, wup_t, wdn, *, tm):
    t = x2d.shape[0]
    n_f, _, tf = wup_t.shape
    return pl.pallas_call(
        _ffn_kernel,
        grid=(t // tm, n_f),
        in_specs=[pl.BlockSpec((tm, D_MODEL), lambda i, j: (i, 0)),
                  pl.BlockSpec((tm, LANES), lambda i, j: (i, 0)),
                  pl.BlockSpec((1, D_MODEL), lambda i, j: (0, 0)),
                  pl.BlockSpec((None, D_MODEL, tf), lambda i, j: (j, 0, 0)),
                  pl.BlockSpec((tf, D_MODEL), lambda i, j: (j, 0))],
        out_specs=pl.BlockSpec((tm, D_MODEL), lambda i, j: (i, 0)),
        out_shape=jax.ShapeDtypeStruct((t, D_MODEL), F32),
        compiler_params=pltpu.CompilerParams(dimension_semantics=("parallel", "arbitrary"),
                                             vmem_limit_bytes=FFN_VMEM_LIMIT),
        name="ffn",
    )(x2d, r2d, g, wup_t, wdn)


def _band_tables(bq, nsel):
    qi = jnp.arange(bq)[:, None]
    kj = jnp.arange(BAND)[None, :] - WINDOW
    qc, kc = qi // CHUNK, jnp.floor_divide(kj, CHUNK)
    in_band = (kc <= qc) & (kc >= qc - WINDOW // CHUNK)
    valid = [in_band & (kj >= 0)] if nsel == 2 else []
    valid.append(in_band)
    return _t5_bucket(qi - kj), jnp.stack(valid).astype(jnp.int32)


def kernel(x_prompt, x_sample, cache_attn_k, cache_attn_v, rel_bias_table, ln_mix_g, w_in,
           q_norm_g, k_norm_g, attn_sinks, sgu_norm_g, sgu_w, sgu_b, out_norm_attn_g,
           out_norm_sgu_g, w_out, ln_ffn_g, w_ffn_up, w_ffn_down):
    bp, sp, _ = x_prompt.shape
    bs, ss, _ = x_sample.shape
    depth = w_in.shape[0]
    assert depth == 1 and ss == CHUNK and cache_attn_k.shape[2] == WINDOW
    l = 0
    xp = x_prompt.reshape(bp * sp, D_MODEL)
    xs = x_sample.reshape(bs * ss, D_MODEL)

    bq, units = 2 * CHUNK, 4
    sinks = attn_sinks[l].reshape(N_HEADS)
    bias_p, bias_s, smax_p, sl_p, smax_s, sl_s, bsb, win = _prep(
        rel_bias_table, sinks, *_band_tables(bq, 2), sgu_b[l], w_in[l], bq_s=ss)
    lng = ln_mix_g[l].reshape(1, D_MODEL)
    qg2 = q_norm_g[l].reshape(1, HEAD_DIM)
    kg2 = k_norm_g[l].reshape(1, HEAD_DIM)
    sg = sgu_norm_g[l].reshape(1, SGU_WIDTH)
    gs = out_norm_sgu_g[l].reshape(1, SGU_WIDTH)
    ga = out_norm_attn_g[l].reshape(1, ATTN_WIDTH)
    lnf = ln_ffn_g[l].reshape(1, D_MODEL)

    ws = sgu_w[l]

    q, k, v, gated, wup, wdn, wo = _proj(xp, lng, win, qg2, kg2, sg, ws, bsb,
                                         w_ffn_up[l], w_ffn_down[l], w_out[l],
                                         tm=512, chunk=SGU_CHUNK, emit_sv=False, tf=FFN_TF)
    k3 = k.reshape(bp, sp, KV_WIDTH)
    v3 = v.reshape(bp, sp, KV_WIDTH)
    prev = lambda b, i: (b, jnp.maximum(i * units - 1, 0), 0)
    cur = lambda b, i: (b, i, 0)
    x1p, r2p = _attn(q.reshape(bp, sp, ATTN_WIDTH),
                     [(k3, WINDOW), (k3, units * bq)], [(v3, WINDOW), (v3, units * bq)],
                     [prev, cur], [(0, 0, WINDOW), (1, 0, units * bq)],
                     bias_p, smax_p, sl_p, ga,
                     x_prompt, gated.reshape(bp, sp, SGU_WIDTH), gs, wo,
                     bq=bq, n_units=units, k_step=bq)
    yp = _ffn(x1p.reshape(bp * sp, D_MODEL), r2p.reshape(bp * sp, LANES), lnf, wup, wdn, tm=1024)

    qs, ks, vs, gated_s, svs = _proj(xs, lng, win, qg2, kg2, sg, ws, bsb,
                                    tm=512, chunk=ss, emit_sv=True)
    seqs = 4
    grp = lambda a, rows: a.reshape(bs // seqs, seqs * rows, a.shape[-1])
    layout = []
    for i in range(seqs):
        layout += [(0, i * WINDOW, WINDOW), (1, i * ss, ss), (-1, 0, BAND - WINDOW - ss)]
    x1s, r2s = _attn(grp(qs, ss),
                     [(grp(cache_attn_k[l].reshape(bs, WINDOW, KV_WIDTH), WINDOW), seqs * WINDOW),
                      (grp(ks, ss), seqs * ss)],
                     [(grp(cache_attn_v[l].reshape(bs, WINDOW, KV_WIDTH), WINDOW), seqs * WINDOW),
                      (grp(vs, ss), seqs * ss)],
                     [cur, cur], layout, bias_s,
                     smax_s, sl_s, ga, grp(xs, ss), grp(gated_s, ss), gs, wo,
                     bq=ss, n_units=seqs, k_step=BAND)
    ys = _ffn(x1s.reshape(bs * ss, D_MODEL), r2s.reshape(bs * ss, LANES), lnf, wup, wdn, tm=1024)

    keep = min(WINDOW, sp)
    kv_shape = (N_KV_HEADS, HEAD_DIM)
    return (yp.reshape(bp, sp, D_MODEL),
            ys.reshape(bs, ss, D_MODEL),
            k3[:, -keep:].reshape(1, bp, keep, *kv_shape),
            v3[:, -keep:].reshape(1, bp, keep, *kv_shape),
            ks.reshape(1, bs, ss, *kv_shape),
            vs.reshape(1, bs, ss, *kv_shape),
            svs.reshape(1, bs, ss, SGU_WIDTH))
```

```python
import functools
import math

import jax
import jax.numpy as jnp
from jax import lax
from jax.experimental import pallas as pl
from jax.experimental.pallas import tpu as pltpu

D_MODEL = 2048
CHUNK = 64
ATTN_WIDTH = 1024
SGU_WIDTH = 1024
HEAD_DIM = 64
N_HEADS = ATTN_WIDTH // HEAD_DIM
N_KV_HEADS = 2
GQA_GROUP = N_HEADS // N_KV_HEADS
WINDOW = 128
N_BUCKETS = 32
MAX_DISTANCE = 128
SGU_CHUNK = 128
SGU_GROUPS = 8
SGU_GROUP_CH = SGU_WIDTH // SGU_GROUPS
D_FF = 4 * D_MODEL
KV_WIDTH = N_KV_HEADS * HEAD_DIM
IN_WIDTH = ATTN_WIDTH + 2 * KV_WIDTH + 2 * SGU_WIDTH
EPS = 1e-6
NEG_INF = -1e30

LANES = 128
VMEM_LIMIT = 56 * 1024 * 1024
FFN_VMEM_LIMIT = 62 * 1024 * 1024
FFN_TF = 1024
FFN_SUB = 1024

BF16 = jnp.bfloat16
F32 = jnp.float32

_KV0 = ATTN_WIDTH
_U0 = ATTN_WIDTH + 2 * KV_WIDTH
_V0 = _U0 + SGU_WIDTH


def _rms_rows(x, gain):
    ms = jnp.mean(x * x, axis=-1, keepdims=True)
    return x * lax.rsqrt(ms + EPS) * gain


def _head_pair_rms(blk, gain2, lane_lo):
    sq = blk * blk
    lo = jnp.sum(jnp.where(lane_lo, sq, 0.0), axis=-1, keepdims=True)
    hi = jnp.sum(jnp.where(lane_lo, 0.0, sq), axis=-1, keepdims=True)
    inv = 1.0 / HEAD_DIM
    r = jnp.where(lane_lo, lax.rsqrt(lo * inv + EPS), lax.rsqrt(hi * inv + EPS))
    return blk * r * gain2


def _cast_weights(wup_ref, wdn_ref, wo_ref, wupb_ref, wdnb_ref, wob_ref):
    tf = wupb_ref.shape[2]
    for j in range(wupb_ref.shape[0]):
        wupb_ref[j] = wup_ref[:, j * tf:(j + 1) * tf].astype(BF16)
    wdnb_ref[...] = wdn_ref[...].astype(BF16)
    wob_ref[...] = wo_ref[...].astype(BF16)


def _proj_kernel(chunk, emit_sv, cast_w, x_ref, lng_ref, win_ref, qg_ref, kg_ref, sg_ref,
                 ws_ref, bs_ref, *rest):
    if cast_w:
        _cast_weights(*rest[:3], *rest[-3:])
        rest = rest[3:-3]
    q_ref, k_ref, v_ref, gated_ref = rest[:4]
    tm = x_ref.shape[0]
    h = _rms_rows(x_ref[...], lng_ref[...]).astype(BF16)
    lane_lo = lax.broadcasted_iota(jnp.int32, (tm, LANES), 1) < HEAD_DIM
    qg2 = jnp.concatenate([qg_ref[...], qg_ref[...]], axis=1)
    kg2 = jnp.concatenate([kg_ref[...], kg_ref[...]], axis=1)

    av = jax.nn.gelu(jnp.dot(h, win_ref[:, _V0:IN_WIDTH], preferred_element_type=F32))
    sv = _rms_rows(av, sg_ref[...])
    if emit_sv:
        rest[4][...] = sv
    svb = sv.astype(BF16)
    u = jax.nn.gelu(jnp.dot(h, win_ref[:, _U0:_V0], preferred_element_type=F32))

    zq = jnp.dot(h, win_ref[:, 0:ATTN_WIDTH], preferred_element_type=F32)
    for c in range(ATTN_WIDTH // LANES):
        sl = slice(c * LANES, (c + 1) * LANES)
        q_ref[:, sl] = (_head_pair_rms(zq[:, sl], qg2, lane_lo)
                        * (HEAD_DIM ** -0.5)).astype(BF16)

    row_c = lax.broadcasted_iota(jnp.int32, (chunk, chunk), 0) // CHUNK
    col_c = lax.broadcasted_iota(jnp.int32, (chunk, chunk), 1) // CHUNK
    n_chunks = tm // chunk
    for g in range(SGU_GROUPS):
        cs = slice(g * SGU_GROUP_CH, (g + 1) * SGU_GROUP_CH)
        w = jnp.where(col_c <= row_c, ws_ref[g, 0:chunk, 0:chunk], 0.0).astype(BF16)
        rhs = jnp.concatenate(
            [svb[n * chunk:(n + 1) * chunk, cs] for n in range(n_chunks)], axis=1)
        sp = jnp.dot(w, rhs, preferred_element_type=F32)
        for n in range(n_chunks):
            rs = slice(n * chunk, (n + 1) * chunk)
            gated_ref[rs, cs] = u[rs, cs] * (sp[:, n * SGU_GROUP_CH:(n + 1) * SGU_GROUP_CH]
                                             + bs_ref[g, 0:chunk, :])

    zkv = jnp.dot(h, win_ref[:, _KV0:_U0], preferred_element_type=F32)
    k_ref[...] = _head_pair_rms(zkv[:, 0:KV_WIDTH], kg2, lane_lo)
    v_ref[...] = zkv[:, KV_WIDTH:2 * KV_WIDTH]


def _proj(x2d, lng, win, qg2, kg2, sg, ws, bsb, wup=None, wdn=None, wo=None, *, tm, chunk,
          emit_sv, tf=None):
    t = x2d.shape[0]
    const = lambda shape: pl.BlockSpec(shape, lambda i: (0,) * len(shape),
                                       pipeline_mode=pl.Buffered(1))
    row = lambda w: pl.BlockSpec((tm, w), lambda i: (i, 0))
    out_shape = [jax.ShapeDtypeStruct((t, ATTN_WIDTH), BF16),
                 jax.ShapeDtypeStruct((t, KV_WIDTH), F32),
                 jax.ShapeDtypeStruct((t, KV_WIDTH), F32),
                 jax.ShapeDtypeStruct((t, SGU_WIDTH), F32)]
    out_specs = [row(ATTN_WIDTH), row(KV_WIDTH), row(KV_WIDTH), row(SGU_WIDTH)]
    if emit_sv:
        out_shape.append(jax.ShapeDtypeStruct((t, SGU_WIDTH), F32))
        out_specs.append(row(SGU_WIDTH))
    in_specs = [row(D_MODEL), const((1, D_MODEL)), const((D_MODEL, IN_WIDTH)),
                const((1, HEAD_DIM)), const((1, HEAD_DIM)), const((1, SGU_WIDTH)),
                const(ws.shape), const(bsb.shape)]
    args = [x2d, lng, win, qg2, kg2, sg, ws, bsb]
    if wup is not None:
        n_steps = t // tm
        up_rows, dn_rows, wo_rows = D_MODEL // n_steps, D_FF // n_steps, wo.shape[0] // n_steps
        in_specs += [pl.BlockSpec((up_rows, D_FF), lambda i: (i, 0)),
                     pl.BlockSpec((dn_rows, D_MODEL), lambda i: (i, 0)),
                     pl.BlockSpec((wo_rows, D_MODEL), lambda i: (i, 0))]
        out_specs += [pl.BlockSpec((D_FF // tf, up_rows, tf), lambda i: (0, i, 0)),
                      pl.BlockSpec((dn_rows, D_MODEL), lambda i: (i, 0)),
                      pl.BlockSpec((wo_rows, D_MODEL), lambda i: (i, 0))]
        out_shape += [jax.ShapeDtypeStruct((D_FF // tf, D_MODEL, tf), BF16),
                      jax.ShapeDtypeStruct((D_FF, D_MODEL), BF16),
                      jax.ShapeDtypeStruct(wo.shape, BF16)]
        args += [wup, wdn, wo]
    return pl.pallas_call(
        functools.partial(_proj_kernel, chunk, emit_sv, wup is not None),
        grid=(t // tm,),
        in_specs=in_specs,
        out_specs=out_specs,
        out_shape=out_shape,
        compiler_params=pltpu.CompilerParams(dimension_semantics=("parallel",),
                                             vmem_limit_bytes=VMEM_LIMIT),
        name="proj_c%d" % chunk,
    )(*args)


BAND = 2 * LANES
PAIRS = GQA_GROUP // 2


def _prep_kernel(bq_s, tab_ref, sink_ref, bkt_ref, valid_ref, sgub_ref, win_ref, bias_p_ref,
                 bias_s_ref, smax_p_ref, sl_p_ref, smax_s_ref, sl_s_ref, bsb_ref, winb_ref):
    winb_ref[...] = win_ref[...].astype(BF16)
    c = bsb_ref.shape[0]
    eye = (lax.broadcasted_iota(jnp.int32, (c, c), 0) == lax.broadcasted_iota(jnp.int32, (c, c), 1))
    b_row = sgub_ref[pl.ds(pl.program_id(0), 1), :]
    bsb_ref[...] = jnp.broadcast_to(
        jnp.sum(jnp.where(eye, b_row, 0.0), axis=1, keepdims=True), bsb_ref.shape)
    bkt = bkt_ref[...]
    h0 = 2 * pl.program_id(0)
    for smax_ref, sl_ref in ((smax_p_ref, sl_p_ref), (smax_s_ref, sl_s_ref)):
        lane_lo = lax.broadcasted_iota(jnp.int32, sl_ref.shape, 1) < HEAD_DIM
        sl_ref[...] = jnp.where(lane_lo, sink_ref[h0], sink_ref[h0 + 1])
        for half in range(2):
            smax_ref[half] = jnp.full(sl_ref.shape, sink_ref[h0 + half], F32)
    for half in range(2):
        h = h0 + half
        row = jnp.broadcast_to(tab_ref[pl.ds(h, 1), :], (bkt.shape[0], LANES))
        acc = jnp.concatenate(
            [jnp.take_along_axis(row, bkt[:, c * LANES:(c + 1) * LANES], axis=1)
             for c in range(bkt.shape[1] // LANES)], axis=1)
        cs = slice(half * BAND, (half + 1) * BAND)
        for s in range(valid_ref.shape[0]):
            bias_p_ref[s, :, cs] = jnp.where(valid_ref[s] != 0, acc, NEG_INF)
        bias_s_ref[0, :, cs] = jnp.where(valid_ref[valid_ref.shape[0] - 1, 0:bq_s, :] != 0,
                                         acc[0:bq_s], NEG_INF)


def _prep(table, sinks, bucket, valid, sgu_b, win, *, bq_s):
    nsel, bq, _ = valid.shape
    n_steps = N_HEADS // 2
    assert n_steps == SGU_GROUPS
    slab = pl.BlockSpec((win.shape[0] // n_steps, win.shape[1]), lambda s: (s, 0))
    pair = lambda n, rows: pl.BlockSpec((n, None, rows, 2 * BAND),
                                        lambda s: (0, s // PAIRS, s % PAIRS, 0))
    smax = lambda rows: pl.BlockSpec((None, 2, rows, LANES), lambda s: (s // PAIRS, 0, s % PAIRS, 0))
    sl = lambda rows: pl.BlockSpec((None, rows, LANES), lambda s: (s // PAIRS, s % PAIRS, 0))
    tables = lambda rows: [jax.ShapeDtypeStruct((N_KV_HEADS, 2, PAIRS * rows, LANES), F32),
                           jax.ShapeDtypeStruct((N_KV_HEADS, PAIRS * rows, LANES), F32)]
    return pl.pallas_call(
        functools.partial(_prep_kernel, bq_s),
        grid=(n_steps,),
        in_specs=[pl.BlockSpec(table.shape, lambda s: (0, 0)),
                  pl.BlockSpec(memory_space=pltpu.SMEM),
                  pl.BlockSpec(bucket.shape, lambda s: (0, 0)),
                  pl.BlockSpec(valid.shape, lambda s: (0, 0, 0)),
                  pl.BlockSpec(sgu_b.shape, lambda s: (0, 0)),
                  slab],
        out_specs=[pair(nsel, bq), pair(1, bq_s), smax(bq), sl(bq), smax(bq_s), sl(bq_s),
                   pl.BlockSpec((None, SGU_CHUNK, LANES), lambda s: (s, 0, 0)), slab],
        out_shape=([jax.ShapeDtypeStruct((nsel, N_KV_HEADS, PAIRS * bq, 2 * BAND), F32),
                    jax.ShapeDtypeStruct((1, N_KV_HEADS, PAIRS * bq_s, 2 * BAND), F32)]
                   + tables(bq) + tables(bq_s)
                   + [jax.ShapeDtypeStruct((SGU_GROUPS, SGU_CHUNK, LANES), F32),
                      jax.ShapeDtypeStruct(win.shape, BF16)]),
        compiler_params=pltpu.CompilerParams(dimension_semantics=("parallel",)),
        name="prep",
    )(table, sinks, bucket, valid, sgu_b, win)


def _t5_bucket(n):
    half = N_BUCKETS // 2
    max_exact = half // 2
    offset = jnp.where(n < 0, half, 0)
    a = jnp.abs(n)
    af = jnp.maximum(a, 1).astype(F32)
    large = max_exact + (jnp.log(af / max_exact) / math.log(MAX_DISTANCE / max_exact)
                         * (half - max_exact)).astype(jnp.int32)
    large = jnp.minimum(large, half - 1)
    return offset + jnp.where(a < max_exact, a, large)


def _attn_kernel(bq, n_units, k_step, nsel, n_kparts, k_layout, *refs):
    q_ref = refs[0]
    k_refs = refs[1:1 + n_kparts]
    v_refs = refs[1 + n_kparts:1 + 2 * n_kparts]
    (bias_ref, smax_ref, sl_ref, ga_ref, x_ref, gated_ref, gs_ref, wo_ref,
     o_ref, r_ref, kk_scr, vv_scr, ao_scr) = refs[1 + 2 * n_kparts:]
    rows = PAIRS * bq

    def split_heads(parts, scr):
        x = jnp.concatenate(
            [parts[p][r0:r0 + n, :] if p >= 0 else jnp.zeros((n, KV_WIDTH), F32)
             for p, r0, n in k_layout], axis=0)
        lo = lax.broadcasted_iota(jnp.int32, x.shape, 1) < HEAD_DIM
        xr = pltpu.roll(x, HEAD_DIM, 1)
        scr[0, 0] = jnp.where(lo, x, 0.0).astype(BF16)
        scr[0, 1] = jnp.where(lo, 0.0, xr).astype(BF16)
        scr[1, 0] = jnp.where(lo, xr, 0.0).astype(BF16)
        scr[1, 1] = jnp.where(lo, 0.0, x).astype(BF16)

    split_heads(k_refs, kk_scr)
    split_heads(v_refs, vv_scr)
    r_i = lax.broadcasted_iota(jnp.int32, (2 * BAND, LANES), 0)
    l_i = lax.broadcasted_iota(jnp.int32, (2 * BAND, LANES), 1)
    ones_blk = jnp.where((r_i < BAND) == (l_i < HEAD_DIM), 1.0, 0.0).astype(BF16)
    lane_lo = lax.broadcasted_iota(jnp.int32, (rows, LANES), 1) < HEAD_DIM
    first_block = pl.program_id(1) == 0

    for u in range(n_units):
        q0, k0 = u * bq, u * k_step
        sel = jnp.where(first_block, 0, 1) if (nsel > 1 and u == 0) else nsel - 1
        for g in range(N_KV_HEADS):
            cols = [slice((PAIRS * g + p) * LANES, (PAIRS * g + p + 1) * LANES)
                    for p in range(PAIRS)]
            qs = jnp.concatenate([q_ref[q0:q0 + bq, c] for c in cols], axis=0)
            k2 = jnp.concatenate([kk_scr[g, 0, k0:k0 + BAND, :],
                                  kk_scr[g, 1, k0:k0 + BAND, :]], axis=0)
            s = lax.dot_general(qs, k2, (((1,), (1,)), ((), ())),
                                preferred_element_type=F32) + bias_ref[sel, g]
            sa, sb = s[:, :BAND], s[:, BAND:]
            ma = jnp.max(jnp.maximum(jnp.maximum(sa[:, :LANES], sa[:, LANES:]), smax_ref[g, 0]),
                         axis=-1, keepdims=True)
            mb = jnp.max(jnp.maximum(jnp.maximum(sb[:, :LANES], sb[:, LANES:]), smax_ref[g, 1]),
                         axis=-1, keepdims=True)
            p = jnp.concatenate([jnp.exp(sa - ma), jnp.exp(sb - mb)], axis=1).astype(BF16)
            v2 = jnp.concatenate([vv_scr[g, 0, k0:k0 + BAND, :],
                                  vv_scr[g, 1, k0:k0 + BAND, :]], axis=0)
            ol = jnp.dot(p, jnp.concatenate([v2, ones_blk], axis=1), preferred_element_type=F32)
            denom = ol[:, LANES:] + jnp.exp(sl_ref[g] - jnp.where(lane_lo, ma, mb))
            out = ol[:, :LANES] / denom
            for p_i, c in enumerate(cols):
                ao_scr[q0:q0 + bq, c] = out[p_i * bq:(p_i + 1) * bq]

    mix_a = _rms_rows(ao_scr[...], ga_ref[...]).astype(BF16)
    mix_s = _rms_rows(gated_ref[...], gs_ref[...]).astype(BF16)
    acc = jnp.dot(mix_s, wo_ref[ATTN_WIDTH:, :], preferred_element_type=F32)
    acc = acc + jnp.dot(mix_a, wo_ref[0:ATTN_WIDTH, :], preferred_element_type=F32)
    x1 = x_ref[...] + acc
    o_ref[...] = x1
    rinv = lax.rsqrt(jnp.mean(x1 * x1, axis=-1, keepdims=True) + EPS)
    r_ref[...] = jnp.broadcast_to(rinv * rinv, r_ref.shape)


def _attn(q, k_parts, v_parts, k_maps, k_layout, bias, smax, sl, ga, x, gated, gs, wo, *, bq,
          n_units, k_step):
    nb, s, _ = q.shape
    nsel = bias.shape[0]
    tq = n_units * bq
    k_rows = [rows for _, rows in k_parts]
    kr = sum(n for _, _, n in k_layout)
    cur = lambda b, i: (b, i, 0)
    row = lambda w: pl.BlockSpec((None, tq, w), cur)
    full = lambda a: pl.BlockSpec(a.shape, lambda b, i: (0,) * a.ndim,
                                  pipeline_mode=pl.Buffered(1))
    kv_specs = [pl.BlockSpec((None, r, KV_WIDTH), m) for r, m in zip(k_rows, k_maps)]
    return pl.pallas_call(
        functools.partial(_attn_kernel, bq, n_units, k_step, nsel, len(k_parts), tuple(k_layout)),
        grid=(nb, s // tq),
        in_specs=([row(ATTN_WIDTH)] + kv_specs + kv_specs
                  + [full(bias), full(smax), full(sl), full(ga),
                     row(D_MODEL), row(SGU_WIDTH), full(gs), full(wo)]),
        out_specs=[row(D_MODEL), row(LANES)],
        out_shape=[jax.ShapeDtypeStruct((nb, s, D_MODEL), F32),
                   jax.ShapeDtypeStruct((nb, s, LANES), F32)],
        scratch_shapes=[pltpu.VMEM((N_KV_HEADS, 2, kr, KV_WIDTH), BF16),
                        pltpu.VMEM((N_KV_HEADS, 2, kr, KV_WIDTH), BF16),
                        pltpu.VMEM((tq, ATTN_WIDTH), F32)],
        compiler_params=pltpu.CompilerParams(dimension_semantics=("parallel", "arbitrary"),
                                             vmem_limit_bytes=VMEM_LIMIT),
        name="attn_q%d" % bq,
    )(q, *[a for a, _ in k_parts], *[a for a, _ in v_parts], bias, smax, sl, ga, x, gated,
      gs, wo)


def _ffn_kernel(x_ref, r_ref, g_ref, wup_ref, wdn_ref, y_ref):
    tf = wup_ref.shape[1]
    xg = (x_ref[...] * g_ref[...]).astype(BF16)
    r2 = jnp.tile(r_ref[...], (1, FFN_SUB // LANES))
    for s in range(tf // FFN_SUB):
        cs = slice(s * FFN_SUB, (s + 1) * FFN_SUB)
        z = jnp.dot(xg, wup_ref[:, cs], preferred_element_type=F32)
        a = (jnp.square(jnp.maximum(z, 0.0)) * r2).astype(BF16)
        if s == 0:
            base = jnp.where(pl.program_id(1) == 0, x_ref[...], y_ref[...])
        else:
            base = y_ref[...]
        y_ref[...] = base + jnp.dot(a, wdn_ref[cs, :], preferred_element_type=F32)


def _ffn(x2d, r2d, g, wup_t, wdn, *, tm):
    t = x2d.shape[0]
    n_f, _, tf = wup_t.shape
    return pl.pallas_call(
        _ffn_kernel,
        grid=(t // tm, n_f),
        in_specs=[pl.BlockSpec((tm, D_MODEL), lambda i, j: (i, 0)),
                  pl.BlockSpec((tm, LANES), lambda i, j: (i, 0)),
                  pl.BlockSpec((1, D_MODEL), lambda i, j: (0, 0)),
                  pl.BlockSpec((None, D_MODEL, tf), lambda i, j: (j, 0, 0)),
                  pl.BlockSpec((tf, D_MODEL), lambda i, j: (j, 0))],
        out_specs=pl.BlockSpec((tm, D_MODEL), lambda i, j: (i, 0)),
        out_shape=jax.ShapeDtypeStruct((t, D_MODEL), F32),
        compiler_params=pltpu.CompilerParams(dimension_semantics=("parallel", "arbitrary"),
                                             vmem_limit_bytes=FFN_VMEM_LIMIT),
        name="ffn",
    )(x2d, r2d, g, wup_t, wdn)


def _band_tables(bq, nsel):
    qi = jnp.arange(bq)[:, None]
    kj = jnp.arange(BAND)[None, :] - WINDOW
    qc, kc = qi // CHUNK, jnp.floor_divide(kj, CHUNK)
    in_band = (kc <= qc) & (kc >= qc - WINDOW // CHUNK)
    valid = [in_band & (kj >= 0)] if nsel == 2 else []
    valid.append(in_band)
    return _t5_bucket(qi - kj), jnp.stack(valid).astype(jnp.int32)


def kernel(x_prompt, x_sample, cache_attn_k, cache_attn_v, rel_bias_table, ln_mix_g, w_in,
           q_norm_g, k_norm_g, attn_sinks, sgu_norm_g, sgu_w, sgu_b, out_norm_attn_g,
           out_norm_sgu_g, w_out, ln_ffn_g, w_ffn_up, w_ffn_down):
    bp, sp, _ = x_prompt.shape
    bs, ss, _ = x_sample.shape
    depth = w_in.shape[0]
    assert depth == 1 and ss == CHUNK and cache_attn_k.shape[2] == WINDOW
    l = 0
    xp = x_prompt.reshape(bp * sp, D_MODEL)
    xs = x_sample.reshape(bs * ss, D_MODEL)

    bq, units = 2 * CHUNK, 4
    sinks = attn_sinks[l].reshape(N_HEADS)
    table_t = jnp.pad(rel_bias_table.T, ((0, 0), (0, LANES - N_BUCKETS)))
    bias_p, bias_s, smax_p, sl_p, smax_s, sl_s, bsb, win = _prep(
        table_t, sinks, *_band_tables(bq, 2), sgu_b[l], w_in[l], bq_s=ss)
    lng = ln_mix_g[l].reshape(1, D_MODEL)
    qg2 = q_norm_g[l].reshape(1, HEAD_DIM)
    kg2 = k_norm_g[l].reshape(1, HEAD_DIM)
    sg = sgu_norm_g[l].reshape(1, SGU_WIDTH)
    gs = out_norm_sgu_g[l].reshape(1, SGU_WIDTH)
    ga = out_norm_attn_g[l].reshape(1, ATTN_WIDTH)
    lnf = ln_ffn_g[l].reshape(1, D_MODEL)

    ws = sgu_w[l]

    q, k, v, gated, wup, wdn, wo = _proj(xp, lng, win, qg2, kg2, sg, ws, bsb,
                                         w_ffn_up[l], w_ffn_down[l], w_out[l],
                                         tm=512, chunk=SGU_CHUNK, emit_sv=False, tf=FFN_TF)
    k3 = k.reshape(bp, sp, KV_WIDTH)
    v3 = v.reshape(bp, sp, KV_WIDTH)
    prev = lambda b, i: (b, jnp.maximum(i * units - 1, 0), 0)
    cur = lambda b, i: (b, i, 0)
    x1p, r2p = _attn(q.reshape(bp, sp, ATTN_WIDTH),
                     [(k3, WINDOW), (k3, units * bq)], [(v3, WINDOW), (v3, units * bq)],
                     [prev, cur], [(0, 0, WINDOW), (1, 0, units * bq)],
                     bias_p, smax_p, sl_p, ga,
                     x_prompt, gated.reshape(bp, sp, SGU_WIDTH), gs, wo,
                     bq=bq, n_units=units, k_step=bq)
    yp = _ffn(x1p.reshape(bp * sp, D_MODEL), r2p.reshape(bp * sp, LANES), lnf, wup, wdn, tm=1024)

    qs, ks, vs, gated_s, svs = _proj(xs, lng, win, qg2, kg2, sg, ws, bsb,
                                    tm=512, chunk=ss, emit_sv=True)
    seqs = 4
    grp = lambda a, rows: a.reshape(bs // seqs, seqs * rows, a.shape[-1])
    layout = []
    for i in range(seqs):
        layout += [(0, i * WINDOW, WINDOW), (1, i * ss, ss), (-1, 0, BAND - WINDOW - ss)]
    x1s, r2s = _attn(grp(qs, ss),
                     [(grp(cache_attn_k[l].reshape(bs, WINDOW, KV_WIDTH), WINDOW), seqs * WINDOW),
                      (grp(ks, ss), seqs * ss)],
                     [(grp(cache_attn_v[l].reshape(bs, WINDOW, KV_WIDTH), WINDOW), seqs * WINDOW),
                      (grp(vs, ss), seqs * ss)],
                     [cur, cur], layout, bias_s,
                     smax_s, sl_s, ga, grp(xs, ss), grp(gated_s, ss), gs, wo,
                     bq=ss, n_units=seqs, k_step=BAND)
    ys = _ffn(x1s.reshape(bs * ss, D_MODEL), r2s.reshape(bs * ss, LANES), lnf, wup, wdn, tm=1024)

    keep = min(WINDOW, sp)
    kv_shape = (N_KV_HEADS, HEAD_DIM)
    return (yp.reshape(bp, sp, D_MODEL),
            ys.reshape(bs, ss, D_MODEL),
            k3[:, -keep:].reshape(1, bp, keep, *kv_shape),
            v3[:, -keep:].reshape(1, bp, keep, *kv_shape),
            ks.reshape(1, bs, ss, *kv_shape),
            vs.reshape(1, bs, ss, *kv_shape),
            svs.reshape(1, bs, ss, SGU_WIDTH))
```

```python
import functools
import math

import jax
import jax.numpy as jnp
from jax import lax
from jax.experimental import pallas as pl
from jax.experimental.pallas import tpu as pltpu

D_MODEL = 2048
CHUNK = 64
ATTN_WIDTH = 1024
SGU_WIDTH = 1024
HEAD_DIM = 64
N_HEADS = ATTN_WIDTH // HEAD_DIM
N_KV_HEADS = 2
GQA_GROUP = N_HEADS // N_KV_HEADS
WINDOW = 128
N_BUCKETS = 32
MAX_DISTANCE = 128
SGU_CHUNK = 128
SGU_GROUPS = 8
SGU_GROUP_CH = SGU_WIDTH // SGU_GROUPS
D_FF = 4 * D_MODEL
KV_WIDTH = N_KV_HEADS * HEAD_DIM
IN_WIDTH = ATTN_WIDTH + 2 * KV_WIDTH + 2 * SGU_WIDTH
EPS = 1e-6
NEG_INF = -1e30

LANES = 128
VMEM_LIMIT = 56 * 1024 * 1024
FFN_VMEM_LIMIT = 62 * 1024 * 1024
PROJ_TM = 512
ATTN_UNITS = 4
SAMPLE_SEQS = 8
FFN_TM = 1024
FFN_TF = 1024
FFN_SUB = 1024

BF16 = jnp.bfloat16
F32 = jnp.float32

_KV0 = ATTN_WIDTH
_U0 = ATTN_WIDTH + 2 * KV_WIDTH
_V0 = _U0 + SGU_WIDTH


def _rms_rows(x, gain):
    ms = jnp.mean(x * x, axis=-1, keepdims=True)
    return x * lax.rsqrt(ms + EPS) * gain


def _head_pair_rms(blk, gain2, lane_lo):
    sq = blk * blk
    lo = jnp.sum(jnp.where(lane_lo, sq, 0.0), axis=-1, keepdims=True)
    hi = jnp.sum(jnp.where(lane_lo, 0.0, sq), axis=-1, keepdims=True)
    inv = 1.0 / HEAD_DIM
    r = jnp.where(lane_lo, lax.rsqrt(lo * inv + EPS), lax.rsqrt(hi * inv + EPS))
    return blk * r * gain2


def _cast_weights(wup_ref, wdn_ref, wo_ref, wupb_ref, wdnb_ref, wob_ref):
    tf = wupb_ref.shape[2]
    for j in range(wupb_ref.shape[0]):
        wupb_ref[j] = wup_ref[:, j * tf:(j + 1) * tf].astype(BF16)
    wdnb_ref[...] = wdn_ref[...].astype(BF16)
    wob_ref[...] = wo_ref[...].astype(BF16)


def _proj_kernel(chunk, emit_sv, cast_w, x_ref, lng_ref, win_ref, qg_ref, kg_ref, sg_ref,
                 ws_ref, bs_ref, *rest):
    if cast_w:
        _cast_weights(*rest[:3], *rest[-3:])
        rest = rest[3:-3]
    q_ref, k_ref, v_ref, gated_ref = rest[:4]
    tm = x_ref.shape[0]
    h = _rms_rows(x_ref[...], lng_ref[...]).astype(BF16)
    lane_lo = lax.broadcasted_iota(jnp.int32, (tm, LANES), 1) < HEAD_DIM
    qg2 = jnp.concatenate([qg_ref[...], qg_ref[...]], axis=1)
    kg2 = jnp.concatenate([kg_ref[...], kg_ref[...]], axis=1)

    av = jax.nn.gelu(jnp.dot(h, win_ref[:, _V0:IN_WIDTH], preferred_element_type=F32))
    sv = _rms_rows(av, sg_ref[...])
    if emit_sv:
        rest[4][...] = sv
    svb = sv.astype(BF16)
    u = jax.nn.gelu(jnp.dot(h, win_ref[:, _U0:_V0], preferred_element_type=F32))

    zq = jnp.dot(h, win_ref[:, 0:ATTN_WIDTH], preferred_element_type=F32)
    for c in range(ATTN_WIDTH // LANES):
        sl = slice(c * LANES, (c + 1) * LANES)
        q_ref[:, sl] = (_head_pair_rms(zq[:, sl], qg2, lane_lo)
                        * (HEAD_DIM ** -0.5)).astype(BF16)

    row_c = lax.broadcasted_iota(jnp.int32, (chunk, chunk), 0) // CHUNK
    col_c = lax.broadcasted_iota(jnp.int32, (chunk, chunk), 1) // CHUNK
    n_chunks = tm // chunk
    for g in range(SGU_GROUPS):
        cs = slice(g * SGU_GROUP_CH, (g + 1) * SGU_GROUP_CH)
        w = jnp.where(col_c <= row_c, ws_ref[g, 0:chunk, 0:chunk], 0.0).astype(BF16)
        rhs = jnp.concatenate(
            [svb[n * chunk:(n + 1) * chunk, cs] for n in range(n_chunks)], axis=1)
        sp = jnp.dot(w, rhs, preferred_element_type=F32)
        for n in range(n_chunks):
            rs = slice(n * chunk, (n + 1) * chunk)
            gated_ref[rs, cs] = u[rs, cs] * (sp[:, n * SGU_GROUP_CH:(n + 1) * SGU_GROUP_CH]
                                             + bs_ref[g, 0:chunk, :])

    zkv = jnp.dot(h, win_ref[:, _KV0:_U0], preferred_element_type=F32)
    k_ref[...] = _head_pair_rms(zkv[:, 0:KV_WIDTH], kg2, lane_lo)
    v_ref[...] = zkv[:, KV_WIDTH:2 * KV_WIDTH]


def _proj(x2d, lng, win, qg2, kg2, sg, ws, bsb, wup=None, wdn=None, wo=None, *, tm, chunk,
          emit_sv, tf=None):
    t = x2d.shape[0]
    const = lambda shape: pl.BlockSpec(shape, lambda i: (0,) * len(shape),
                                       pipeline_mode=pl.Buffered(1))
    row = lambda w: pl.BlockSpec((tm, w), lambda i: (i, 0))
    out_shape = [jax.ShapeDtypeStruct((t, ATTN_WIDTH), BF16),
                 jax.ShapeDtypeStruct((t, KV_WIDTH), F32),
                 jax.ShapeDtypeStruct((t, KV_WIDTH), F32),
                 jax.ShapeDtypeStruct((t, SGU_WIDTH), F32)]
    out_specs = [row(ATTN_WIDTH), row(KV_WIDTH), row(KV_WIDTH), row(SGU_WIDTH)]
    if emit_sv:
        out_shape.append(jax.ShapeDtypeStruct((t, SGU_WIDTH), F32))
        out_specs.append(row(SGU_WIDTH))
    in_specs = [row(D_MODEL), const((1, D_MODEL)), const((D_MODEL, IN_WIDTH)),
                const((1, HEAD_DIM)), const((1, HEAD_DIM)), const((1, SGU_WIDTH)),
                const(ws.shape), const(bsb.shape)]
    args = [x2d, lng, win, qg2, kg2, sg, ws, bsb]
    if wup is not None:
        n_steps = t // tm
        up_rows, dn_rows, wo_rows = D_MODEL // n_steps, D_FF // n_steps, wo.shape[0] // n_steps
        in_specs += [pl.BlockSpec((up_rows, D_FF), lambda i: (i, 0)),
                     pl.BlockSpec((dn_rows, D_MODEL), lambda i: (i, 0)),
                     pl.BlockSpec((wo_rows, D_MODEL), lambda i: (i, 0))]
        out_specs += [pl.BlockSpec((D_FF // tf, up_rows, tf), lambda i: (0, i, 0)),
                      pl.BlockSpec((dn_rows, D_MODEL), lambda i: (i, 0)),
                      pl.BlockSpec((wo_rows, D_MODEL), lambda i: (i, 0))]
        out_shape += [jax.ShapeDtypeStruct((D_FF // tf, D_MODEL, tf), BF16),
                      jax.ShapeDtypeStruct((D_FF, D_MODEL), BF16),
                      jax.ShapeDtypeStruct(wo.shape, BF16)]
        args += [wup, wdn, wo]
    return pl.pallas_call(
        functools.partial(_proj_kernel, chunk, emit_sv, wup is not None),
        grid=(t // tm,),
        in_specs=in_specs,
        out_specs=out_specs,
        out_shape=out_shape,
        compiler_params=pltpu.CompilerParams(dimension_semantics=("parallel",),
                                             vmem_limit_bytes=VMEM_LIMIT),
        name="proj_c%d" % chunk,
    )(*args)


BAND = 2 * LANES
PAIRS = GQA_GROUP // 2


def _prep_kernel(bq_s, tab_ref, sink_ref, bkt_ref, valid_ref, sgub_ref, win_ref, bias_p_ref,
                 bias_s_ref, smax_p_ref, sl_p_ref, smax_s_ref, sl_s_ref, bsb_ref, winb_ref):
    winb_ref[...] = win_ref[...].astype(BF16)
    c = bsb_ref.shape[0]
    eye = (lax.broadcasted_iota(jnp.int32, (c, c), 0) == lax.broadcasted_iota(jnp.int32, (c, c), 1))
    b_row = sgub_ref[pl.ds(pl.program_id(0), 1), :]
    bsb_ref[...] = jnp.broadcast_to(
        jnp.sum(jnp.where(eye, b_row, 0.0), axis=1, keepdims=True), bsb_ref.shape)
    bkt = bkt_ref[...]
    h0 = 2 * pl.program_id(0)
    for smax_ref, sl_ref in ((smax_p_ref, sl_p_ref), (smax_s_ref, sl_s_ref)):
        lane_lo = lax.broadcasted_iota(jnp.int32, sl_ref.shape, 1) < HEAD_DIM
        sl_ref[...] = jnp.where(lane_lo, sink_ref[h0], sink_ref[h0 + 1])
        for half in range(2):
            smax_ref[half] = jnp.full(sl_ref.shape, sink_ref[h0 + half], F32)
    for half in range(2):
        h = h0 + half
        row = jnp.broadcast_to(tab_ref[pl.ds(h, 1), :], (bkt.shape[0], LANES))
        acc = jnp.concatenate(
            [jnp.take_along_axis(row, bkt[:, c * LANES:(c + 1) * LANES], axis=1)
             for c in range(bkt.shape[1] // LANES)], axis=1)
        cs = slice(half * BAND, (half + 1) * BAND)
        for s in range(valid_ref.shape[0]):
            bias_p_ref[s, :, cs] = jnp.where(valid_ref[s] != 0, acc, NEG_INF)
        bias_s_ref[0, :, cs] = jnp.where(valid_ref[valid_ref.shape[0] - 1, 0:bq_s, :] != 0,
                                         acc[0:bq_s], NEG_INF)


def _prep(table, sinks, bucket, valid, sgu_b, win, *, bq_s):
    nsel, bq, _ = valid.shape
    n_steps = N_HEADS // 2
    assert n_steps == SGU_GROUPS
    slab = pl.BlockSpec((win.shape[0] // n_steps, win.shape[1]), lambda s: (s, 0))
    pair = lambda n, rows: pl.BlockSpec((n, None, rows, 2 * BAND),
                                        lambda s: (0, s // PAIRS, s % PAIRS, 0))
    smax = lambda rows: pl.BlockSpec((None, 2, rows, LANES), lambda s: (s // PAIRS, 0, s % PAIRS, 0))
    sl = lambda rows: pl.BlockSpec((None, rows, LANES), lambda s: (s // PAIRS, s % PAIRS, 0))
    tables = lambda rows: [jax.ShapeDtypeStruct((N_KV_HEADS, 2, PAIRS * rows, LANES), F32),
                           jax.ShapeDtypeStruct((N_KV_HEADS, PAIRS * rows, LANES), F32)]
    return pl.pallas_call(
        functools.partial(_prep_kernel, bq_s),
        grid=(n_steps,),
        in_specs=[pl.BlockSpec(table.shape, lambda s: (0, 0)),
                  pl.BlockSpec(memory_space=pltpu.SMEM),
                  pl.BlockSpec(bucket.shape, lambda s: (0, 0)),
                  pl.BlockSpec(valid.shape, lambda s: (0, 0, 0)),
                  pl.BlockSpec(sgu_b.shape, lambda s: (0, 0)),
                  slab],
        out_specs=[pair(nsel, bq), pair(1, bq_s), smax(bq), sl(bq), smax(bq_s), sl(bq_s),
                   pl.BlockSpec((None, SGU_CHUNK, LANES), lambda s: (s, 0, 0)), slab],
        out_shape=([jax.ShapeDtypeStruct((nsel, N_KV_HEADS, PAIRS * bq, 2 * BAND), F32),
                    jax.ShapeDtypeStruct((1, N_KV_HEADS, PAIRS * bq_s, 2 * BAND), F32)]
                   + tables(bq) + tables(bq_s)
                   + [jax.ShapeDtypeStruct((SGU_GROUPS, SGU_CHUNK, LANES), F32),
                      jax.ShapeDtypeStruct(win.shape, BF16)]),
        compiler_params=pltpu.CompilerParams(dimension_semantics=("parallel",)),
        name="prep",
    )(table, sinks, bucket, valid, sgu_b, win)


def _t5_bucket(n):
    half = N_BUCKETS // 2
    max_exact = half // 2
    offset = jnp.where(n < 0, half, 0)
    a = jnp.abs(n)
    af = jnp.maximum(a, 1).astype(F32)
    large = max_exact + (jnp.log(af / max_exact) / math.log(MAX_DISTANCE / max_exact)
                         * (half - max_exact)).astype(jnp.int32)
    large = jnp.minimum(large, half - 1)
    return offset + jnp.where(a < max_exact, a, large)


def _attn_kernel(bq, n_units, k_step, nsel, n_kparts, k_layout, *refs):
    q_ref = refs[0]
    k_refs = refs[1:1 + n_kparts]
    v_refs = refs[1 + n_kparts:1 + 2 * n_kparts]
    (bias_ref, smax_ref, sl_ref, ga_ref, x_ref, gated_ref, gs_ref, wo_ref,
     o_ref, r_ref, kk_scr, vv_scr, ao_scr) = refs[1 + 2 * n_kparts:]
    rows = PAIRS * bq

    def split_heads(parts, scr):
        x = jnp.concatenate(
            [parts[p][r0:r0 + n, :] if p >= 0 else jnp.zeros((n, KV_WIDTH), F32)
             for p, r0, n in k_layout], axis=0)
        lo = lax.broadcasted_iota(jnp.int32, x.shape, 1) < HEAD_DIM
        xr = pltpu.roll(x, HEAD_DIM, 1)
        scr[0, 0] = jnp.where(lo, x, 0.0).astype(BF16)
        scr[0, 1] = jnp.where(lo, 0.0, xr).astype(BF16)
        scr[1, 0] = jnp.where(lo, xr, 0.0).astype(BF16)
        scr[1, 1] = jnp.where(lo, 0.0, x).astype(BF16)

    split_heads(k_refs, kk_scr)
    split_heads(v_refs, vv_scr)
    r_i = lax.broadcasted_iota(jnp.int32, (2 * BAND, LANES), 0)
    l_i = lax.broadcasted_iota(jnp.int32, (2 * BAND, LANES), 1)
    ones_blk = jnp.where((r_i < BAND) == (l_i < HEAD_DIM), 1.0, 0.0).astype(BF16)
    lane_lo = lax.broadcasted_iota(jnp.int32, (rows, LANES), 1) < HEAD_DIM
    first_block = pl.program_id(1) == 0

    for u in range(n_units):
        q0, k0 = u * bq, u * k_step
        sel = jnp.where(first_block, 0, 1) if (nsel > 1 and u == 0) else nsel - 1
        for g in range(N_KV_HEADS):
            cols = [slice((PAIRS * g + p) * LANES, (PAIRS * g + p + 1) * LANES)
                    for p in range(PAIRS)]
            qs = jnp.concatenate([q_ref[q0:q0 + bq, c] for c in cols], axis=0)
            k2 = jnp.concatenate([kk_scr[g, 0, k0:k0 + BAND, :],
                                  kk_scr[g, 1, k0:k0 + BAND, :]], axis=0)
            s = lax.dot_general(qs, k2, (((1,), (1,)), ((), ())),
                                preferred_element_type=F32) + bias_ref[sel, g]
            sa, sb = s[:, :BAND], s[:, BAND:]
            ma = jnp.max(jnp.maximum(jnp.maximum(sa[:, :LANES], sa[:, LANES:]), smax_ref[g, 0]),
                         axis=-1, keepdims=True)
            mb = jnp.max(jnp.maximum(jnp.maximum(sb[:, :LANES], sb[:, LANES:]), smax_ref[g, 1]),
                         axis=-1, keepdims=True)
            p = jnp.concatenate([jnp.exp(sa - ma), jnp.exp(sb - mb)], axis=1).astype(BF16)
            v2 = jnp.concatenate([vv_scr[g, 0, k0:k0 + BAND, :],
                                  vv_scr[g, 1, k0:k0 + BAND, :]], axis=0)
            ol = jnp.dot(p, jnp.concatenate([v2, ones_blk], axis=1), preferred_element_type=F32)
            denom = ol[:, LANES:] + jnp.exp(sl_ref[g] - jnp.where(lane_lo, ma, mb))
            out = ol[:, :LANES] / denom
            for p_i, c in enumerate(cols):
                ao_scr[q0:q0 + bq, c] = out[p_i * bq:(p_i + 1) * bq]

    mix_a = _rms_rows(ao_scr[...], ga_ref[...]).astype(BF16)
    mix_s = _rms_rows(gated_ref[...], gs_ref[...]).astype(BF16)
    acc = jnp.dot(mix_s, wo_ref[ATTN_WIDTH:, :], preferred_element_type=F32)
    acc = acc + jnp.dot(mix_a, wo_ref[0:ATTN_WIDTH, :], preferred_element_type=F32)
    x1 = x_ref[...] + acc
    o_ref[...] = x1
    rinv = lax.rsqrt(jnp.mean(x1 * x1, axis=-1, keepdims=True) + EPS)
    r_ref[...] = jnp.broadcast_to(rinv * rinv, r_ref.shape)


def _attn(q, k_parts, v_parts, k_maps, k_layout, bias, smax, sl, ga, x, gated, gs, wo, *, bq,
          n_units, k_step):
    nb, s, _ = q.shape
    nsel = bias.shape[0]
    tq = n_units * bq
    k_rows = [rows for _, rows in k_parts]
    kr = sum(n for _, _, n in k_layout)
    cur = lambda b, i: (b, i, 0)
    row = lambda w: pl.BlockSpec((None, tq, w), cur)
    full = lambda a: pl.BlockSpec(a.shape, lambda b, i: (0,) * a.ndim,
                                  pipeline_mode=pl.Buffered(1))
    kv_specs = [pl.BlockSpec((None, r, KV_WIDTH), m) for r, m in zip(k_rows, k_maps)]
    return pl.pallas_call(
        functools.partial(_attn_kernel, bq, n_units, k_step, nsel, len(k_parts), tuple(k_layout)),
        grid=(nb, s // tq),
        in_specs=([row(ATTN_WIDTH)] + kv_specs + kv_specs
                  + [full(bias), full(smax), full(sl), full(ga),
                     row(D_MODEL), row(SGU_WIDTH), full(gs), full(wo)]),
        out_specs=[row(D_MODEL), row(LANES)],
        out_shape=[jax.ShapeDtypeStruct((nb, s, D_MODEL), F32),
                   jax.ShapeDtypeStruct((nb, s, LANES), F32)],
        scratch_shapes=[pltpu.VMEM((N_KV_HEADS, 2, kr, KV_WIDTH), BF16),
                        pltpu.VMEM((N_KV_HEADS, 2, kr, KV_WIDTH), BF16),
                        pltpu.VMEM((tq, ATTN_WIDTH), F32)],
        compiler_params=pltpu.CompilerParams(dimension_semantics=("parallel", "arbitrary"),
                                             vmem_limit_bytes=VMEM_LIMIT),
        name="attn_q%d" % bq,
    )(q, *[a for a, _ in k_parts], *[a for a, _ in v_parts], bias, smax, sl, ga, x, gated,
      gs, wo)


def _ffn_kernel(x_ref, r_ref, g_ref, wup_ref, wdn_ref, y_ref):
    tf = wup_ref.shape[1]
    xg = (x_ref[...] * g_ref[...]).astype(BF16)
    r2 = jnp.tile(r_ref[...], (1, FFN_SUB // LANES))
    for s in range(tf // FFN_SUB):
        cs = slice(s * FFN_SUB, (s + 1) * FFN_SUB)
        z = jnp.dot(xg, wup_ref[:, cs], preferred_element_type=F32)
        a = (jnp.square(jnp.maximum(z, 0.0)) * r2).astype(BF16)
        if s == 0:
            base = jnp.where(pl.program_id(1) == 0, x_ref[...], y_ref[...])
        else:
            base = y_ref[...]
        y_ref[...] = base + jnp.dot(a, wdn_ref[cs, :], preferred_element_type=F32)


def _ffn(x2d, r2d, g, wup_t, wdn, *, tm):
    t = x2d.shape[0]
    n_f, _, tf = wup_t.shape
    return pl.pallas_call(
        _ffn_kernel,
        grid=(t // tm, n_f),
        in_specs=[pl.BlockSpec((tm, D_MODEL), lambda i, j: (i, 0)),
                  pl.BlockSpec((tm, LANES), lambda i, j: (i, 0)),
                  pl.BlockSpec((1, D_MODEL), lambda i, j: (0, 0)),
                  pl.BlockSpec((None, D_MODEL, tf), lambda i, j: (j, 0, 0)),
                  pl.BlockSpec((tf, D_MODEL), lambda i, j: (j, 0))],
        out_specs=pl.BlockSpec((tm, D_MODEL), lambda i, j: (i, 0)),
        out_shape=jax.ShapeDtypeStruct((t, D_MODEL), F32),
        compiler_params=pltpu.CompilerParams(dimension_semantics=("parallel", "arbitrary"),
                                             vmem_limit_bytes=FFN_VMEM_LIMIT),
        name="ffn",
    )(x2d, r2d, g, wup_t, wdn)


def _band_tables(bq, nsel):
    qi = jnp.arange(bq)[:, None]
    kj = jnp.arange(BAND)[None, :] - WINDOW
    qc, kc = qi // CHUNK, jnp.floor_divide(kj, CHUNK)
    in_band = (kc <= qc) & (kc >= qc - WINDOW // CHUNK)
    valid = [in_band & (kj >= 0)] if nsel == 2 else []
    valid.append(in_band)
    return _t5_bucket(qi - kj), jnp.stack(valid).astype(jnp.int32)


def kernel(x_prompt, x_sample, cache_attn_k, cache_attn_v, rel_bias_table, ln_mix_g, w_in,
           q_norm_g, k_norm_g, attn_sinks, sgu_norm_g, sgu_w, sgu_b, out_norm_attn_g,
           out_norm_sgu_g, w_out, ln_ffn_g, w_ffn_up, w_ffn_down):
    bp, sp, _ = x_prompt.shape
    bs, ss, _ = x_sample.shape
    depth = w_in.shape[0]
    assert depth == 1 and ss == CHUNK and cache_attn_k.shape[2] == WINDOW
    l = 0
    xp = x_prompt.reshape(bp * sp, D_MODEL)
    xs = x_sample.reshape(bs * ss, D_MODEL)

    bq, units = 2 * CHUNK, ATTN_UNITS
    sinks = attn_sinks[l].reshape(N_HEADS)
    table_t = jnp.pad(rel_bias_table.T, ((0, 0), (0, LANES - N_BUCKETS)))
    bias_p, bias_s, smax_p, sl_p, smax_s, sl_s, bsb, win = _prep(
        table_t, sinks, *_band_tables(bq, 2), sgu_b[l], w_in[l], bq_s=ss)
    lng = ln_mix_g[l].reshape(1, D_MODEL)
    qg2 = q_norm_g[l].reshape(1, HEAD_DIM)
    kg2 = k_norm_g[l].reshape(1, HEAD_DIM)
    sg = sgu_norm_g[l].reshape(1, SGU_WIDTH)
    gs = out_norm_sgu_g[l].reshape(1, SGU_WIDTH)
    ga = out_norm_attn_g[l].reshape(1, ATTN_WIDTH)
    lnf = ln_ffn_g[l].reshape(1, D_MODEL)

    ws = sgu_w[l]

    q, k, v, gated, wup, wdn, wo = _proj(xp, lng, win, qg2, kg2, sg, ws, bsb,
                                         w_ffn_up[l], w_ffn_down[l], w_out[l],
                                         tm=PROJ_TM, chunk=SGU_CHUNK, emit_sv=False, tf=FFN_TF)
    k3 = k.reshape(bp, sp, KV_WIDTH)
    v3 = v.reshape(bp, sp, KV_WIDTH)
    prev = lambda b, i: (b, jnp.maximum(i * units - 1, 0), 0)
    cur = lambda b, i: (b, i, 0)
    x1p, r2p = _attn(q.reshape(bp, sp, ATTN_WIDTH),
                     [(k3, WINDOW), (k3, units * bq)], [(v3, WINDOW), (v3, units * bq)],
                     [prev, cur], [(0, 0, WINDOW), (1, 0, units * bq)],
                     bias_p, smax_p, sl_p, ga,
                     x_prompt, gated.reshape(bp, sp, SGU_WIDTH), gs, wo,
                     bq=bq, n_units=units, k_step=bq)
    yp = _ffn(x1p.reshape(bp * sp, D_MODEL), r2p.reshape(bp * sp, LANES), lnf, wup, wdn, tm=FFN_TM)

    qs, ks, vs, gated_s, svs = _proj(xs, lng, win, qg2, kg2, sg, ws, bsb,
                                    tm=PROJ_TM, chunk=ss, emit_sv=True)
    seqs = SAMPLE_SEQS
    grp = lambda a, rows: a.reshape(bs // seqs, seqs * rows, a.shape[-1])
    layout = []
    for i in range(seqs):
        layout += [(0, i * WINDOW, WINDOW), (1, i * ss, ss), (-1, 0, BAND - WINDOW - ss)]
    x1s, r2s = _attn(grp(qs, ss),
                     [(grp(cache_attn_k[l].reshape(bs, WINDOW, KV_WIDTH), WINDOW), seqs * WINDOW),
                      (grp(ks, ss), seqs * ss)],
                     [(grp(cache_attn_v[l].reshape(bs, WINDOW, KV_WIDTH), WINDOW), seqs * WINDOW),
                      (grp(vs, ss), seqs * ss)],
                     [cur, cur], layout, bias_s,
                     smax_s, sl_s, ga, grp(xs, ss), grp(gated_s, ss), gs, wo,
                     bq=ss, n_units=seqs, k_step=BAND)
    ys = _ffn(x1s.reshape(bs * ss, D_MODEL), r2s.reshape(bs * ss, LANES), lnf, wup, wdn, tm=FFN_TM)

    keep = min(WINDOW, sp)
    kv_shape = (N_KV_HEADS, HEAD_DIM)
    return (yp.reshape(bp, sp, D_MODEL),
            ys.reshape(bs, ss, D_MODEL),
            k3[:, -keep:].reshape(1, bp, keep, *kv_shape),
            v3[:, -keep:].reshape(1, bp, keep, *kv_shape),
            ks.reshape(1, bs, ss, *kv_shape),
            vs.reshape(1, bs, ss, *kv_shape),
            svs.reshape(1, bs, ss, SGU_WIDTH))
```

```python
import functools
import math

import jax
import jax.numpy as jnp
from jax import lax
from jax.experimental import pallas as pl
from jax.experimental.pallas import tpu as pltpu

D_MODEL = 2048
CHUNK = 64
ATTN_WIDTH = 1024
SGU_WIDTH = 1024
HEAD_DIM = 64
N_HEADS = ATTN_WIDTH // HEAD_DIM
N_KV_HEADS = 2
GQA_GROUP = N_HEADS // N_KV_HEADS
WINDOW = 128
N_BUCKETS = 32
MAX_DISTANCE = 128
SGU_CHUNK = 128
SGU_GROUPS = 8
SGU_GROUP_CH = SGU_WIDTH // SGU_GROUPS
D_FF = 4 * D_MODEL
KV_WIDTH = N_KV_HEADS * HEAD_DIM
IN_WIDTH = ATTN_WIDTH + 2 * KV_WIDTH + 2 * SGU_WIDTH
EPS = 1e-6
NEG_INF = -1e30

LANES = 128
VMEM_LIMIT = 56 * 1024 * 1024
FFN_VMEM_LIMIT = 62 * 1024 * 1024
PROJ_TM = 512
ATTN_UNITS = 4
SAMPLE_SEQS = 4
FFN_TM = 1024
FFN_TF = 1024
FFN_SUB = 1024

BF16 = jnp.bfloat16
F32 = jnp.float32

_KV0 = ATTN_WIDTH
_U0 = ATTN_WIDTH + 2 * KV_WIDTH
_V0 = _U0 + SGU_WIDTH


def _rms_rows(x, gain):
    ms = jnp.mean(x * x, axis=-1, keepdims=True)
    return x * lax.rsqrt(ms + EPS) * gain


def _head_pair_rms(blk, gain2, lane_lo):
    sq = blk * blk
    lo = jnp.sum(jnp.where(lane_lo, sq, 0.0), axis=-1, keepdims=True)
    hi = jnp.sum(jnp.where(lane_lo, 0.0, sq), axis=-1, keepdims=True)
    inv = 1.0 / HEAD_DIM
    r = jnp.where(lane_lo, lax.rsqrt(lo * inv + EPS), lax.rsqrt(hi * inv + EPS))
    return blk * r * gain2


def _cast_weights(wup_ref, wdn_ref, wo_ref, wupb_ref, wdnb_ref, wob_ref):
    tf = wupb_ref.shape[2]
    for j in range(wupb_ref.shape[0]):
        wupb_ref[j] = wup_ref[:, j * tf:(j + 1) * tf].astype(BF16)
    wdnb_ref[...] = wdn_ref[...].astype(BF16)
    wob_ref[...] = wo_ref[...].astype(BF16)


def _proj_kernel(chunk, emit_sv, cast_w, x_ref, lng_ref, win_ref, qg_ref, kg_ref, sg_ref,
                 ws_ref, bs_ref, *rest):
    if cast_w:
        _cast_weights(*rest[:3], *rest[-3:])
        rest = rest[3:-3]
    q_ref, k_ref, v_ref, gated_ref = rest[:4]
    tm = x_ref.shape[0]
    h = _rms_rows(x_ref[...], lng_ref[...]).astype(BF16)
    lane_lo = lax.broadcasted_iota(jnp.int32, (tm, LANES), 1) < HEAD_DIM
    qg2 = jnp.concatenate([qg_ref[...], qg_ref[...]], axis=1)
    kg2 = jnp.concatenate([kg_ref[...], kg_ref[...]], axis=1)

    av = jax.nn.gelu(jnp.dot(h, win_ref[:, _V0:IN_WIDTH], preferred_element_type=F32))
    sv = _rms_rows(av, sg_ref[...])
    if emit_sv:
        rest[4][...] = sv
    svb = sv.astype(BF16)
    u = jax.nn.gelu(jnp.dot(h, win_ref[:, _U0:_V0], preferred_element_type=F32))

    zq = jnp.dot(h, win_ref[:, 0:ATTN_WIDTH], preferred_element_type=F32)
    for c in range(ATTN_WIDTH // LANES):
        sl = slice(c * LANES, (c + 1) * LANES)
        q_ref[:, sl] = (_head_pair_rms(zq[:, sl], qg2, lane_lo)
                        * (HEAD_DIM ** -0.5)).astype(BF16)

    zkv = jnp.dot(h, win_ref[:, _KV0:_U0], preferred_element_type=F32)
    k_ref[...] = _head_pair_rms(zkv[:, 0:KV_WIDTH], kg2, lane_lo)
    v_ref[...] = zkv[:, KV_WIDTH:2 * KV_WIDTH]

    row_c = lax.broadcasted_iota(jnp.int32, (chunk, chunk), 0) // CHUNK
    col_c = lax.broadcasted_iota(jnp.int32, (chunk, chunk), 1) // CHUNK
    n_chunks = tm // chunk
    for g in range(SGU_GROUPS):
        cs = slice(g * SGU_GROUP_CH, (g + 1) * SGU_GROUP_CH)
        w = jnp.where(col_c <= row_c, ws_ref[g, 0:chunk, 0:chunk], 0.0).astype(BF16)
        rhs = jnp.concatenate(
            [svb[n * chunk:(n + 1) * chunk, cs] for n in range(n_chunks)], axis=1)
        sp = jnp.dot(w, rhs, preferred_element_type=F32)
        for n in range(n_chunks):
            rs = slice(n * chunk, (n + 1) * chunk)
            gated_ref[rs, cs] = u[rs, cs] * (sp[:, n * SGU_GROUP_CH:(n + 1) * SGU_GROUP_CH]
                                             + bs_ref[g, 0:chunk, :])


def _proj(x2d, lng, win, qg2, kg2, sg, ws, bsb, wup=None, wdn=None, wo=None, *, tm, chunk,
          emit_sv, tf=None):
    t = x2d.shape[0]
    const = lambda shape: pl.BlockSpec(shape, lambda i: (0,) * len(shape),
                                       pipeline_mode=pl.Buffered(1))
    row = lambda w: pl.BlockSpec((tm, w), lambda i: (i, 0))
    out_shape = [jax.ShapeDtypeStruct((t, ATTN_WIDTH), BF16),
                 jax.ShapeDtypeStruct((t, KV_WIDTH), F32),
                 jax.ShapeDtypeStruct((t, KV_WIDTH), F32),
                 jax.ShapeDtypeStruct((t, SGU_WIDTH), F32)]
    out_specs = [row(ATTN_WIDTH), row(KV_WIDTH), row(KV_WIDTH), row(SGU_WIDTH)]
    if emit_sv:
        out_shape.append(jax.ShapeDtypeStruct((t, SGU_WIDTH), F32))
        out_specs.append(row(SGU_WIDTH))
    in_specs = [row(D_MODEL), const((1, D_MODEL)), const((D_MODEL, IN_WIDTH)),
                const((1, HEAD_DIM)), const((1, HEAD_DIM)), const((1, SGU_WIDTH)),
                const(ws.shape), const(bsb.shape)]
    args = [x2d, lng, win, qg2, kg2, sg, ws, bsb]
    if wup is not None:
        n_steps = t // tm
        up_rows, dn_rows, wo_rows = D_MODEL // n_steps, D_FF // n_steps, wo.shape[0] // n_steps
        in_specs += [pl.BlockSpec((up_rows, D_FF), lambda i: (i, 0)),
                     pl.BlockSpec((dn_rows, D_MODEL), lambda i: (i, 0)),
                     pl.BlockSpec((wo_rows, D_MODEL), lambda i: (i, 0))]
        out_specs += [pl.BlockSpec((D_FF // tf, up_rows, tf), lambda i: (0, i, 0)),
                      pl.BlockSpec((dn_rows, D_MODEL), lambda i: (i, 0)),
                      pl.BlockSpec((wo_rows, D_MODEL), lambda i: (i, 0))]
        out_shape += [jax.ShapeDtypeStruct((D_FF // tf, D_MODEL, tf), BF16),
                      jax.ShapeDtypeStruct((D_FF, D_MODEL), BF16),
                      jax.ShapeDtypeStruct(wo.shape, BF16)]
        args += [wup, wdn, wo]
    return pl.pallas_call(
        functools.partial(_proj_kernel, chunk, emit_sv, wup is not None),
        grid=(t // tm,),
        in_specs=in_specs,
        out_specs=out_specs,
        out_shape=out_shape,
        compiler_params=pltpu.CompilerParams(dimension_semantics=("parallel",),
                                             vmem_limit_bytes=VMEM_LIMIT),
        name="proj_c%d" % chunk,
    )(*args)


BAND = 2 * LANES
PAIRS = GQA_GROUP // 2


def _prep_kernel(bq_s, tab_ref, sink_ref, bkt_ref, valid_ref, sgub_ref, win_ref, bias_p_ref,
                 bias_s_ref, smax_p_ref, sl_p_ref, smax_s_ref, sl_s_ref, bsb_ref, winb_ref):
    winb_ref[...] = win_ref[...].astype(BF16)
    c = bsb_ref.shape[0]
    eye = (lax.broadcasted_iota(jnp.int32, (c, c), 0) == lax.broadcasted_iota(jnp.int32, (c, c), 1))
    b_row = sgub_ref[pl.ds(pl.program_id(0), 1), :]
    bsb_ref[...] = jnp.broadcast_to(
        jnp.sum(jnp.where(eye, b_row, 0.0), axis=1, keepdims=True), bsb_ref.shape)
    bkt = bkt_ref[...]
    h0 = 2 * pl.program_id(0)
    for smax_ref, sl_ref in ((smax_p_ref, sl_p_ref), (smax_s_ref, sl_s_ref)):
        lane_lo = lax.broadcasted_iota(jnp.int32, sl_ref.shape, 1) < HEAD_DIM
        sl_ref[...] = jnp.where(lane_lo, sink_ref[h0], sink_ref[h0 + 1])
        for half in range(2):
            smax_ref[half] = jnp.full(sl_ref.shape, sink_ref[h0 + half], F32)
    for half in range(2):
        h = h0 + half
        row = jnp.broadcast_to(tab_ref[pl.ds(h, 1), :], (bkt.shape[0], LANES))
        acc = jnp.concatenate(
            [jnp.take_along_axis(row, bkt[:, c * LANES:(c + 1) * LANES], axis=1)
             for c in range(bkt.shape[1] // LANES)], axis=1)
        cs = slice(half * BAND, (half + 1) * BAND)
        for s in range(valid_ref.shape[0]):
            bias_p_ref[s, :, cs] = jnp.where(valid_ref[s] != 0, acc, NEG_INF)
        bias_s_ref[0, :, cs] = jnp.where(valid_ref[valid_ref.shape[0] - 1, 0:bq_s, :] != 0,
                                         acc[0:bq_s], NEG_INF)


def _prep(table, sinks, bucket, valid, sgu_b, win, *, bq_s):
    nsel, bq, _ = valid.shape
    n_steps = N_HEADS // 2
    assert n_steps == SGU_GROUPS
    slab = pl.BlockSpec((win.shape[0] // n_steps, win.shape[1]), lambda s: (s, 0))
    pair = lambda n, rows: pl.BlockSpec((n, None, rows, 2 * BAND),
                                        lambda s: (0, s // PAIRS, s % PAIRS, 0))
    smax = lambda rows: pl.BlockSpec((None, 2, rows, LANES), lambda s: (s // PAIRS, 0, s % PAIRS, 0))
    sl = lambda rows: pl.BlockSpec((None, rows, LANES), lambda s: (s // PAIRS, s % PAIRS, 0))
    tables = lambda rows: [jax.ShapeDtypeStruct((N_KV_HEADS, 2, PAIRS * rows, LANES), F32),
                           jax.ShapeDtypeStruct((N_KV_HEADS, PAIRS * rows, LANES), F32)]
    return pl.pallas_call(
        functools.partial(_prep_kernel, bq_s),
        grid=(n_steps,),
        in_specs=[pl.BlockSpec(table.shape, lambda s: (0, 0)),
                  pl.BlockSpec(memory_space=pltpu.SMEM),
                  pl.BlockSpec(bucket.shape, lambda s: (0, 0)),
                  pl.BlockSpec(valid.shape, lambda s: (0, 0, 0)),
                  pl.BlockSpec(sgu_b.shape, lambda s: (0, 0)),
                  slab],
        out_specs=[pair(nsel, bq), pair(1, bq_s), smax(bq), sl(bq), smax(bq_s), sl(bq_s),
                   pl.BlockSpec((None, SGU_CHUNK, LANES), lambda s: (s, 0, 0)), slab],
        out_shape=([jax.ShapeDtypeStruct((nsel, N_KV_HEADS, PAIRS * bq, 2 * BAND), F32),
                    jax.ShapeDtypeStruct((1, N_KV_HEADS, PAIRS * bq_s, 2 * BAND), F32)]
                   + tables(bq) + tables(bq_s)
                   + [jax.ShapeDtypeStruct((SGU_GROUPS, SGU_CHUNK, LANES), F32),
                      jax.ShapeDtypeStruct(win.shape, BF16)]),
        compiler_params=pltpu.CompilerParams(dimension_semantics=("parallel",)),
        name="prep",
    )(table, sinks, bucket, valid, sgu_b, win)


def _t5_bucket(n):
    half = N_BUCKETS // 2
    max_exact = half // 2
    offset = jnp.where(n < 0, half, 0)
    a = jnp.abs(n)
    af = jnp.maximum(a, 1).astype(F32)
    large = max_exact + (jnp.log(af / max_exact) / math.log(MAX_DISTANCE / max_exact)
                         * (half - max_exact)).astype(jnp.int32)
    large = jnp.minimum(large, half - 1)
    return offset + jnp.where(a < max_exact, a, large)


def _attn_kernel(bq, n_units, k_step, nsel, n_kparts, k_layout, *refs):
    q_ref = refs[0]
    k_refs = refs[1:1 + n_kparts]
    v_refs = refs[1 + n_kparts:1 + 2 * n_kparts]
    (bias_ref, smax_ref, sl_ref, ga_ref, x_ref, gated_ref, gs_ref, wo_ref,
     o_ref, r_ref, kk_scr, vv_scr, ao_scr) = refs[1 + 2 * n_kparts:]
    rows = PAIRS * bq

    def split_heads(parts, scr):
        x = jnp.concatenate(
            [parts[p][r0:r0 + n, :] if p >= 0 else jnp.zeros((n, KV_WIDTH), F32)
             for p, r0, n in k_layout], axis=0)
        lo = lax.broadcasted_iota(jnp.int32, x.shape, 1) < HEAD_DIM
        xr = pltpu.roll(x, HEAD_DIM, 1)
        scr[0, 0] = jnp.where(lo, x, 0.0).astype(BF16)
        scr[0, 1] = jnp.where(lo, 0.0, xr).astype(BF16)
        scr[1, 0] = jnp.where(lo, xr, 0.0).astype(BF16)
        scr[1, 1] = jnp.where(lo, 0.0, x).astype(BF16)

    split_heads(k_refs, kk_scr)
    split_heads(v_refs, vv_scr)
    r_i = lax.broadcasted_iota(jnp.int32, (2 * BAND, LANES), 0)
    l_i = lax.broadcasted_iota(jnp.int32, (2 * BAND, LANES), 1)
    ones_blk = jnp.where((r_i < BAND) == (l_i < HEAD_DIM), 1.0, 0.0).astype(BF16)
    lane_lo = lax.broadcasted_iota(jnp.int32, (rows, LANES), 1) < HEAD_DIM
    first_block = pl.program_id(1) == 0

    for u in range(n_units):
        q0, k0 = u * bq, u * k_step
        sel = jnp.where(first_block, 0, 1) if (nsel > 1 and u == 0) else nsel - 1
        for g in range(N_KV_HEADS):
            cols = [slice((PAIRS * g + p) * LANES, (PAIRS * g + p + 1) * LANES)
                    for p in range(PAIRS)]
            qs = jnp.concatenate([q_ref[q0:q0 + bq, c] for c in cols], axis=0)
            k2 = jnp.concatenate([kk_scr[g, 0, k0:k0 + BAND, :],
                                  kk_scr[g, 1, k0:k0 + BAND, :]], axis=0)
            s = lax.dot_general(qs, k2, (((1,), (1,)), ((), ())),
                                preferred_element_type=F32) + bias_ref[sel, g]
            sa, sb = s[:, :BAND], s[:, BAND:]
            ma = jnp.max(jnp.maximum(jnp.maximum(sa[:, :LANES], sa[:, LANES:]), smax_ref[g, 0]),
                         axis=-1, keepdims=True)
            mb = jnp.max(jnp.maximum(jnp.maximum(sb[:, :LANES], sb[:, LANES:]), smax_ref[g, 1]),
                         axis=-1, keepdims=True)
            p = jnp.concatenate([jnp.exp(sa - ma), jnp.exp(sb - mb)], axis=1).astype(BF16)
            v2 = jnp.concatenate([vv_scr[g, 0, k0:k0 + BAND, :],
                                  vv_scr[g, 1, k0:k0 + BAND, :]], axis=0)
            ol = jnp.dot(p, jnp.concatenate([v2, ones_blk], axis=1), preferred_element_type=F32)
            denom = ol[:, LANES:] + jnp.exp(sl_ref[g] - jnp.where(lane_lo, ma, mb))
            out = ol[:, :LANES] / denom
            for p_i, c in enumerate(cols):
                ao_scr[q0:q0 + bq, c] = out[p_i * bq:(p_i + 1) * bq]

    mix = jnp.concatenate([_rms_rows(ao_scr[...], ga_ref[...]).astype(BF16),
                           _rms_rows(gated_ref[...], gs_ref[...]).astype(BF16)], axis=1)
    x1 = x_ref[...] + jnp.dot(mix, wo_ref[...], preferred_element_type=F32)
    o_ref[...] = x1
    rinv = lax.rsqrt(jnp.mean(x1 * x1, axis=-1, keepdims=True) + EPS)
    r_ref[...] = jnp.broadcast_to(rinv * rinv, r_ref.shape)


def _attn(q, k_parts, v_parts, k_maps, k_layout, bias, smax, sl, ga, x, gated, gs, wo, *, bq,
          n_units, k_step):
    nb, s, _ = q.shape
    nsel = bias.shape[0]
    tq = n_units * bq
    k_rows = [rows for _, rows in k_parts]
    kr = sum(n for _, _, n in k_layout)
    cur = lambda b, i: (b, i, 0)
    row = lambda w: pl.BlockSpec((None, tq, w), cur)
    full = lambda a: pl.BlockSpec(a.shape, lambda b, i: (0,) * a.ndim,
                                  pipeline_mode=pl.Buffered(1))
    kv_specs = [pl.BlockSpec((None, r, KV_WIDTH), m) for r, m in zip(k_rows, k_maps)]
    return pl.pallas_call(
        functools.partial(_attn_kernel, bq, n_units, k_step, nsel, len(k_parts), tuple(k_layout)),
        grid=(nb, s // tq),
        in_specs=([row(ATTN_WIDTH)] + kv_specs + kv_specs
                  + [full(bias), full(smax), full(sl), full(ga),
                     row(D_MODEL), row(SGU_WIDTH), full(gs), full(wo)]),
        out_specs=[row(D_MODEL), row(LANES)],
        out_shape=[jax.ShapeDtypeStruct((nb, s, D_MODEL), F32),
                   jax.ShapeDtypeStruct((nb, s, LANES), F32)],
        scratch_shapes=[pltpu.VMEM((N_KV_HEADS, 2, kr, KV_WIDTH), BF16),
                        pltpu.VMEM((N_KV_HEADS, 2, kr, KV_WIDTH), BF16),
                        pltpu.VMEM((tq, ATTN_WIDTH), F32)],
        compiler_params=pltpu.CompilerParams(dimension_semantics=("parallel", "arbitrary"),
                                             vmem_limit_bytes=VMEM_LIMIT),
        name="attn_q%d" % bq,
    )(q, *[a for a, _ in k_parts], *[a for a, _ in v_parts], bias, smax, sl, ga, x, gated,
      gs, wo)


def _ffn_kernel(x_ref, r_ref, g_ref, wup_ref, wdn_ref, y_ref):
    tf = wup_ref.shape[1]
    xg = (x_ref[...] * g_ref[...]).astype(BF16)
    r2 = jnp.tile(r_ref[...], (1, FFN_SUB // LANES))
    for s in range(tf // FFN_SUB):
        cs = slice(s * FFN_SUB, (s + 1) * FFN_SUB)
        z = jnp.dot(xg, wup_ref[:, cs], preferred_element_type=F32)
        a = (jnp.square(jnp.maximum(z, 0.0)) * r2).astype(BF16)
        if s == 0:
            base = jnp.where(pl.program_id(1) == 0, x_ref[...], y_ref[...])
        else:
            base = y_ref[...]
        y_ref[...] = base + jnp.dot(a, wdn_ref[cs, :], preferred_element_type=F32)


def _ffn(x2d, r2d, g, wup_t, wdn, *, tm):
    t = x2d.shape[0]
    n_f, _, tf = wup_t.shape
    return pl.pallas_call(
        _ffn_kernel,
        grid=(t // tm, n_f),
        in_specs=[pl.BlockSpec((tm, D_MODEL), lambda i, j: (i, 0)),
                  pl.BlockSpec((tm, LANES), lambda i, j: (i, 0)),
                  pl.BlockSpec((1, D_MODEL), lambda i, j: (0, 0)),
                  pl.BlockSpec((None, D_MODEL, tf), lambda i, j: (j, 0, 0)),
                  pl.BlockSpec((tf, D_MODEL), lambda i, j: (j, 0))],
        out_specs=pl.BlockSpec((tm, D_MODEL), lambda i, j: (i, 0)),
        out_shape=jax.ShapeDtypeStruct((t, D_MODEL), F32),
        compiler_params=pltpu.CompilerParams(dimension_semantics=("parallel", "arbitrary"),
                                             vmem_limit_bytes=FFN_VMEM_LIMIT),
        name="ffn",
    )(x2d, r2d, g, wup_t, wdn)


def _band_tables(bq, nsel):
    qi = jnp.arange(bq)[:, None]
    kj = jnp.arange(BAND)[None, :] - WINDOW
    qc, kc = qi // CHUNK, jnp.floor_divide(kj, CHUNK)
    in_band = (kc <= qc) & (kc >= qc - WINDOW // CHUNK)
    valid = [in_band & (kj >= 0)] if nsel == 2 else []
    valid.append(in_band)
    return _t5_bucket(qi - kj), jnp.stack(valid).astype(jnp.int32)


def kernel(x_prompt, x_sample, cache_attn_k, cache_attn_v, rel_bias_table, ln_mix_g, w_in,
           q_norm_g, k_norm_g, attn_sinks, sgu_norm_g, sgu_w, sgu_b, out_norm_attn_g,
           out_norm_sgu_g, w_out, ln_ffn_g, w_ffn_up, w_ffn_down):
    bp, sp, _ = x_prompt.shape
    bs, ss, _ = x_sample.shape
    depth = w_in.shape[0]
    assert depth == 1 and ss == CHUNK and cache_attn_k.shape[2] == WINDOW
    l = 0
    xp = x_prompt.reshape(bp * sp, D_MODEL)
    xs = x_sample.reshape(bs * ss, D_MODEL)

    bq, units = 2 * CHUNK, ATTN_UNITS
    sinks = attn_sinks[l].reshape(N_HEADS)
    table_t = jnp.pad(rel_bias_table.T, ((0, 0), (0, LANES - N_BUCKETS)))
    bias_p, bias_s, smax_p, sl_p, smax_s, sl_s, bsb, win = _prep(
        table_t, sinks, *_band_tables(bq, 2), sgu_b[l], w_in[l], bq_s=ss)
    lng = ln_mix_g[l].reshape(1, D_MODEL)
    qg2 = q_norm_g[l].reshape(1, HEAD_DIM)
    kg2 = k_norm_g[l].reshape(1, HEAD_DIM)
    sg = sgu_norm_g[l].reshape(1, SGU_WIDTH)
    gs = out_norm_sgu_g[l].reshape(1, SGU_WIDTH)
    ga = out_norm_attn_g[l].reshape(1, ATTN_WIDTH)
    lnf = ln_ffn_g[l].reshape(1, D_MODEL)

    ws = sgu_w[l]

    q, k, v, gated, wup, wdn, wo = _proj(xp, lng, win, qg2, kg2, sg, ws, bsb,
                                         w_ffn_up[l], w_ffn_down[l], w_out[l],
                                         tm=PROJ_TM, chunk=SGU_CHUNK, emit_sv=False, tf=FFN_TF)
    k3 = k.reshape(bp, sp, KV_WIDTH)
    v3 = v.reshape(bp, sp, KV_WIDTH)
    prev = lambda b, i: (b, jnp.maximum(i * units - 1, 0), 0)
    cur = lambda b, i: (b, i, 0)
    x1p, r2p = _attn(q.reshape(bp, sp, ATTN_WIDTH),
                     [(k3, WINDOW), (k3, units * bq)], [(v3, WINDOW), (v3, units * bq)],
                     [prev, cur], [(0, 0, WINDOW), (1, 0, units * bq)],
                     bias_p, smax_p, sl_p, ga,
                     x_prompt, gated.reshape(bp, sp, SGU_WIDTH), gs, wo,
                     bq=bq, n_units=units, k_step=bq)
    yp = _ffn(x1p.reshape(bp * sp, D_MODEL), r2p.reshape(bp * sp, LANES), lnf, wup, wdn, tm=FFN_TM)

    qs, ks, vs, gated_s, svs = _proj(xs, lng, win, qg2, kg2, sg, ws, bsb,
                                    tm=PROJ_TM, chunk=ss, emit_sv=True)
    seqs = SAMPLE_SEQS
    grp = lambda a, rows: a.reshape(bs // seqs, seqs * rows, a.shape[-1])
    layout = []
    for i in range(seqs):
        layout += [(0, i * WINDOW, WINDOW), (1, i * ss, ss), (-1, 0, BAND - WINDOW - ss)]
    x1s, r2s = _attn(grp(qs, ss),
                     [(grp(cache_attn_k[l].reshape(bs, WINDOW, KV_WIDTH), WINDOW), seqs * WINDOW),
                      (grp(ks, ss), seqs * ss)],
                     [(grp(cache_attn_v[l].reshape(bs, WINDOW, KV_WIDTH), WINDOW), seqs * WINDOW),
                      (grp(vs, ss), seqs * ss)],
                     [cur, cur], layout, bias_s,
                     smax_s, sl_s, ga, grp(xs, ss), grp(gated_s, ss), gs, wo,
                     bq=ss, n_units=seqs, k_step=BAND)
    ys = _ffn(x1s.reshape(bs * ss, D_MODEL), r2s.reshape(bs * ss, LANES), lnf, wup, wdn, tm=FFN_TM)

    keep = min(WINDOW, sp)
    kv_shape = (N_KV_HEADS, HEAD_DIM)
    return (yp.reshape(bp, sp, D_MODEL),
            ys.reshape(bs, ss, D_MODEL),
            k3[:, -keep:].reshape(1, bp, keep, *kv_shape),
            v3[:, -keep:].reshape(1, bp, keep, *kv_shape),
            ks.reshape(1, bs, ss, *kv_shape),
            vs.reshape(1, bs, ss, *kv_shape),
            svs.reshape(1, bs, ss, SGU_WIDTH))
```

```python
import functools
import math

import jax
import jax.numpy as jnp
from jax import lax
from jax.experimental import pallas as pl
from jax.experimental.pallas import tpu as pltpu

D_MODEL = 2048
CHUNK = 64
ATTN_WIDTH = 1024
SGU_WIDTH = 1024
HEAD_DIM = 64
N_HEADS = ATTN_WIDTH // HEAD_DIM
N_KV_HEADS = 2
GQA_GROUP = N_HEADS // N_KV_HEADS
WINDOW = 128
N_BUCKETS = 32
MAX_DISTANCE = 128
SGU_CHUNK = 128
SGU_GROUPS = 8
SGU_GROUP_CH = SGU_WIDTH // SGU_GROUPS
D_FF = 4 * D_MODEL
KV_WIDTH = N_KV_HEADS * HEAD_DIM
IN_WIDTH = ATTN_WIDTH + 2 * KV_WIDTH + 2 * SGU_WIDTH
EPS = 1e-6
NEG_INF = -1e30

LANES = 128
PREP_VMEM_LIMIT = 16 * 1024 * 1024
PROJ_VMEM_LIMIT = 49 * 1024 * 1024
ATTN_VMEM_LIMIT = 43 * 1024 * 1024
FFN_VMEM_LIMIT = 56 * 1024 * 1024
PROJ_TM = 512
ATTN_UNITS = 4
SAMPLE_SEQS = 4
FFN_TM = 1024
FFN_TF = 1024
FFN_SUB = 1024

BF16 = jnp.bfloat16
F32 = jnp.float32

_KV0 = ATTN_WIDTH
_U0 = ATTN_WIDTH + 2 * KV_WIDTH
_V0 = _U0 + SGU_WIDTH


def _rms_rows(x, gain):
    ms = jnp.mean(x * x, axis=-1, keepdims=True)
    return x * lax.rsqrt(ms + EPS) * gain


def _head_pair_rms(blk, gain2, lane_lo):
    sq = blk * blk
    lo = jnp.sum(jnp.where(lane_lo, sq, 0.0), axis=-1, keepdims=True)
    hi = jnp.sum(jnp.where(lane_lo, 0.0, sq), axis=-1, keepdims=True)
    inv = 1.0 / HEAD_DIM
    r = jnp.where(lane_lo, lax.rsqrt(lo * inv + EPS), lax.rsqrt(hi * inv + EPS))
    return blk * r * gain2


def _cast_weights(wup_ref, wdn_ref, wo_ref, wupb_ref, wdnb_ref, wob_ref):
    tf = wupb_ref.shape[2]
    for j in range(wupb_ref.shape[0]):
        wupb_ref[j] = wup_ref[:, j * tf:(j + 1) * tf].astype(BF16)
    wdnb_ref[...] = wdn_ref[...].astype(BF16)
    wob_ref[...] = wo_ref[...].astype(BF16)


def _proj_kernel(chunk, emit_sv, cast_w, x_ref, lng_ref, win_ref, qg_ref, kg_ref, sg_ref,
                 ws_ref, bs_ref, *rest):
    if cast_w:
        _cast_weights(*rest[:3], *rest[-3:])
        rest = rest[3:-3]
    q_ref, k_ref, v_ref, gated_ref = rest[:4]
    tm = x_ref.shape[0]
    h = _rms_rows(x_ref[...], lng_ref[...]).astype(BF16)
    lane_lo = lax.broadcasted_iota(jnp.int32, (tm, LANES), 1) < HEAD_DIM
    qg2 = jnp.concatenate([qg_ref[...], qg_ref[...]], axis=1)
    kg2 = jnp.concatenate([kg_ref[...], kg_ref[...]], axis=1)

    av = jax.nn.gelu(jnp.dot(h, win_ref[:, _V0:IN_WIDTH], preferred_element_type=F32))
    sv = _rms_rows(av, sg_ref[...])
    if emit_sv:
        rest[4][...] = sv
    svb = sv.astype(BF16)
    u = jax.nn.gelu(jnp.dot(h, win_ref[:, _U0:_V0], preferred_element_type=F32))

    zq = jnp.dot(h, win_ref[:, 0:ATTN_WIDTH], preferred_element_type=F32)
    for c in range(ATTN_WIDTH // LANES):
        sl = slice(c * LANES, (c + 1) * LANES)
        q_ref[:, sl] = (_head_pair_rms(zq[:, sl], qg2, lane_lo)
                        * (HEAD_DIM ** -0.5)).astype(BF16)

    zkv = jnp.dot(h, win_ref[:, _KV0:_U0], preferred_element_type=F32)
    k_ref[...] = _head_pair_rms(zkv[:, 0:KV_WIDTH], kg2, lane_lo)
    v_ref[...] = zkv[:, KV_WIDTH:2 * KV_WIDTH]

    row_c = lax.broadcasted_iota(jnp.int32, (chunk, chunk), 0) // CHUNK
    col_c = lax.broadcasted_iota(jnp.int32, (chunk, chunk), 1) // CHUNK
    n_chunks = tm // chunk
    for g in range(SGU_GROUPS):
        cs = slice(g * SGU_GROUP_CH, (g + 1) * SGU_GROUP_CH)
        w = jnp.where(col_c <= row_c, ws_ref[g, 0:chunk, 0:chunk], 0.0).astype(BF16)
        rhs = jnp.concatenate(
            [svb[n * chunk:(n + 1) * chunk, cs] for n in range(n_chunks)], axis=1)
        sp = jnp.dot(w, rhs, preferred_element_type=F32)
        for n in range(n_chunks):
            rs = slice(n * chunk, (n + 1) * chunk)
            gated_ref[rs, cs] = u[rs, cs] * (sp[:, n * SGU_GROUP_CH:(n + 1) * SGU_GROUP_CH]
                                             + bs_ref[g, 0:chunk, :])


def _proj(x2d, lng, win, qg2, kg2, sg, ws, bsb, wup=None, wdn=None, wo=None, *, tm, chunk,
          emit_sv, tf=None):
    t = x2d.shape[0]
    const = lambda shape: pl.BlockSpec(shape, lambda i: (0,) * len(shape),
                                       pipeline_mode=pl.Buffered(1))
    row = lambda w: pl.BlockSpec((tm, w), lambda i: (i, 0))
    out_shape = [jax.ShapeDtypeStruct((t, ATTN_WIDTH), BF16),
                 jax.ShapeDtypeStruct((t, KV_WIDTH), F32),
                 jax.ShapeDtypeStruct((t, KV_WIDTH), F32),
                 jax.ShapeDtypeStruct((t, SGU_WIDTH), F32)]
    out_specs = [row(ATTN_WIDTH), row(KV_WIDTH), row(KV_WIDTH), row(SGU_WIDTH)]
    if emit_sv:
        out_shape.append(jax.ShapeDtypeStruct((t, SGU_WIDTH), F32))
        out_specs.append(row(SGU_WIDTH))
    in_specs = [row(D_MODEL), const((1, D_MODEL)), const((D_MODEL, IN_WIDTH)),
                const((1, HEAD_DIM)), const((1, HEAD_DIM)), const((1, SGU_WIDTH)),
                const(ws.shape), const(bsb.shape)]
    args = [x2d, lng, win, qg2, kg2, sg, ws, bsb]
    if wup is not None:
        n_steps = t // tm
        up_rows, dn_rows, wo_rows = D_MODEL // n_steps, D_FF // n_steps, wo.shape[0] // n_steps
        in_specs += [pl.BlockSpec((up_rows, D_FF), lambda i: (i, 0)),
                     pl.BlockSpec((dn_rows, D_MODEL), lambda i: (i, 0)),
                     pl.BlockSpec((wo_rows, D_MODEL), lambda i: (i, 0))]
        out_specs += [pl.BlockSpec((D_FF // tf, up_rows, tf), lambda i: (0, i, 0)),
                      pl.BlockSpec((dn_rows, D_MODEL), lambda i: (i, 0)),
                      pl.BlockSpec((wo_rows, D_MODEL), lambda i: (i, 0))]
        out_shape += [jax.ShapeDtypeStruct((D_FF // tf, D_MODEL, tf), BF16),
                      jax.ShapeDtypeStruct((D_FF, D_MODEL), BF16),
                      jax.ShapeDtypeStruct(wo.shape, BF16)]
        args += [wup, wdn, wo]
    return pl.pallas_call(
        functools.partial(_proj_kernel, chunk, emit_sv, wup is not None),
        grid=(t // tm,),
        in_specs=in_specs,
        out_specs=out_specs,
        out_shape=out_shape,
        compiler_params=pltpu.CompilerParams(dimension_semantics=("parallel",),
                                             vmem_limit_bytes=PROJ_VMEM_LIMIT),
        name="proj_c%d" % chunk,
    )(*args)


BAND = 2 * LANES
PAIRS = GQA_GROUP // 2


def _prep_kernel(bq_s, tab_ref, sink_ref, bkt_ref, valid_ref, sgub_ref, win_ref, bias_p_ref,
                 bias_s_ref, smax_p_ref, sl_p_ref, smax_s_ref, sl_s_ref, bsb_ref, winb_ref):
    winb_ref[...] = win_ref[...].astype(BF16)
    c = bsb_ref.shape[0]
    eye = (lax.broadcasted_iota(jnp.int32, (c, c), 0) == lax.broadcasted_iota(jnp.int32, (c, c), 1))
    b_row = sgub_ref[pl.ds(pl.program_id(0), 1), :]
    bsb_ref[...] = jnp.broadcast_to(
        jnp.sum(jnp.where(eye, b_row, 0.0), axis=1, keepdims=True), bsb_ref.shape)
    bkt = bkt_ref[...]
    h0 = 2 * pl.program_id(0)
    for smax_ref, sl_ref in ((smax_p_ref, sl_p_ref), (smax_s_ref, sl_s_ref)):
        lane_lo = lax.broadcasted_iota(jnp.int32, sl_ref.shape, 1) < HEAD_DIM
        sl_ref[...] = jnp.where(lane_lo, sink_ref[h0], sink_ref[h0 + 1])
        for half in range(2):
            smax_ref[half] = jnp.full(sl_ref.shape, sink_ref[h0 + half], F32)
    for half in range(2):
        h = h0 + half
        row = jnp.broadcast_to(tab_ref[pl.ds(h, 1), :], (bkt.shape[0], LANES))
        acc = jnp.concatenate(
            [jnp.take_along_axis(row, bkt[:, c * LANES:(c + 1) * LANES], axis=1)
             for c in range(bkt.shape[1] // LANES)], axis=1)
        cs = slice(half * BAND, (half + 1) * BAND)
        for s in range(valid_ref.shape[0]):
            bias_p_ref[s, :, cs] = jnp.where(valid_ref[s] != 0, acc, NEG_INF)
        bias_s_ref[0, :, cs] = jnp.where(valid_ref[valid_ref.shape[0] - 1, 0:bq_s, :] != 0,
                                         acc[0:bq_s], NEG_INF)


def _prep(table, sinks, bucket, valid, sgu_b, win, *, bq_s):
    nsel, bq, _ = valid.shape
    n_steps = N_HEADS // 2
    assert n_steps == SGU_GROUPS
    slab = pl.BlockSpec((win.shape[0] // n_steps, win.shape[1]), lambda s: (s, 0))
    pair = lambda n, rows: pl.BlockSpec((n, None, rows, 2 * BAND),
                                        lambda s: (0, s // PAIRS, s % PAIRS, 0))
    smax = lambda rows: pl.BlockSpec((None, 2, rows, LANES), lambda s: (s // PAIRS, 0, s % PAIRS, 0))
    sl = lambda rows: pl.BlockSpec((None, rows, LANES), lambda s: (s // PAIRS, s % PAIRS, 0))
    tables = lambda rows: [jax.ShapeDtypeStruct((N_KV_HEADS, 2, PAIRS * rows, LANES), F32),
                           jax.ShapeDtypeStruct((N_KV_HEADS, PAIRS * rows, LANES), F32)]
    return pl.pallas_call(
        functools.partial(_prep_kernel, bq_s),
        grid=(n_steps,),
        in_specs=[pl.BlockSpec(table.shape, lambda s: (0, 0)),
                  pl.BlockSpec(memory_space=pltpu.SMEM),
                  pl.BlockSpec(bucket.shape, lambda s: (0, 0)),
                  pl.BlockSpec(valid.shape, lambda s: (0, 0, 0)),
                  pl.BlockSpec(sgu_b.shape, lambda s: (0, 0)),
                  slab],
        out_specs=[pair(nsel, bq), pair(1, bq_s), smax(bq), sl(bq), smax(bq_s), sl(bq_s),
                   pl.BlockSpec((None, SGU_CHUNK, LANES), lambda s: (s, 0, 0)), slab],
        out_shape=([jax.ShapeDtypeStruct((nsel, N_KV_HEADS, PAIRS * bq, 2 * BAND), F32),
                    jax.ShapeDtypeStruct((1, N_KV_HEADS, PAIRS * bq_s, 2 * BAND), F32)]
                   + tables(bq) + tables(bq_s)
                   + [jax.ShapeDtypeStruct((SGU_GROUPS, SGU_CHUNK, LANES), F32),
                      jax.ShapeDtypeStruct(win.shape, BF16)]),
        compiler_params=pltpu.CompilerParams(dimension_semantics=("parallel",),
                                             vmem_limit_bytes=PREP_VMEM_LIMIT),
        name="prep",
    )(table, sinks, bucket, valid, sgu_b, win)


def _t5_bucket(n):
    half = N_BUCKETS // 2
    max_exact = half // 2
    offset = jnp.where(n < 0, half, 0)
    a = jnp.abs(n)
    af = jnp.maximum(a, 1).astype(F32)
    large = max_exact + (jnp.log(af / max_exact) / math.log(MAX_DISTANCE / max_exact)
                         * (half - max_exact)).astype(jnp.int32)
    large = jnp.minimum(large, half - 1)
    return offset + jnp.where(a < max_exact, a, large)


def _attn_kernel(bq, n_units, k_step, nsel, n_kparts, k_layout, *refs):
    q_ref = refs[0]
    k_refs = refs[1:1 + n_kparts]
    v_refs = refs[1 + n_kparts:1 + 2 * n_kparts]
    (bias_ref, smax_ref, sl_ref, ga_ref, x_ref, gated_ref, gs_ref, wo_ref,
     o_ref, r_ref, kk_scr, vv_scr, ao_scr) = refs[1 + 2 * n_kparts:]
    rows = PAIRS * bq

    def split_heads(parts, scr):
        x = jnp.concatenate(
            [parts[p][r0:r0 + n, :] if p >= 0 else jnp.zeros((n, KV_WIDTH), F32)
             for p, r0, n in k_layout], axis=0)
        lo = lax.broadcasted_iota(jnp.int32, x.shape, 1) < HEAD_DIM
        xr = pltpu.roll(x, HEAD_DIM, 1)
        scr[0, 0] = jnp.where(lo, x, 0.0).astype(BF16)
        scr[0, 1] = jnp.where(lo, 0.0, xr).astype(BF16)
        scr[1, 0] = jnp.where(lo, xr, 0.0).astype(BF16)
        scr[1, 1] = jnp.where(lo, 0.0, x).astype(BF16)

    split_heads(k_refs, kk_scr)
    split_heads(v_refs, vv_scr)
    r_i = lax.broadcasted_iota(jnp.int32, (2 * BAND, LANES), 0)
    l_i = lax.broadcasted_iota(jnp.int32, (2 * BAND, LANES), 1)
    ones_blk = jnp.where((r_i < BAND) == (l_i < HEAD_DIM), 1.0, 0.0).astype(BF16)
    lane_lo = lax.broadcasted_iota(jnp.int32, (rows, LANES), 1) < HEAD_DIM
    first_block = pl.program_id(1) == 0

    for u in range(n_units):
        q0, k0 = u * bq, u * k_step
        sel = jnp.where(first_block, 0, 1) if (nsel > 1 and u == 0) else nsel - 1
        for g in range(N_KV_HEADS):
            cols = [slice((PAIRS * g + p) * LANES, (PAIRS * g + p + 1) * LANES)
                    for p in range(PAIRS)]
            qs = jnp.concatenate([q_ref[q0:q0 + bq, c] for c in cols], axis=0)
            k2 = jnp.concatenate([kk_scr[g, 0, k0:k0 + BAND, :],
                                  kk_scr[g, 1, k0:k0 + BAND, :]], axis=0)
            s = lax.dot_general(qs, k2, (((1,), (1,)), ((), ())),
                                preferred_element_type=F32) + bias_ref[sel, g]
            sa, sb = s[:, :BAND], s[:, BAND:]
            ma = jnp.max(jnp.maximum(jnp.maximum(sa[:, :LANES], sa[:, LANES:]), smax_ref[g, 0]),
                         axis=-1, keepdims=True)
            mb = jnp.max(jnp.maximum(jnp.maximum(sb[:, :LANES], sb[:, LANES:]), smax_ref[g, 1]),
                         axis=-1, keepdims=True)
            p = jnp.concatenate([jnp.exp(sa - ma), jnp.exp(sb - mb)], axis=1).astype(BF16)
            v2 = jnp.concatenate([vv_scr[g, 0, k0:k0 + BAND, :],
                                  vv_scr[g, 1, k0:k0 + BAND, :]], axis=0)
            ol = jnp.dot(p, jnp.concatenate([v2, ones_blk], axis=1), preferred_element_type=F32)
            denom = ol[:, LANES:] + jnp.exp(sl_ref[g] - jnp.where(lane_lo, ma, mb))
            out = ol[:, :LANES] / denom
            for p_i, c in enumerate(cols):
                ao_scr[q0:q0 + bq, c] = out[p_i * bq:(p_i + 1) * bq]

    mix = jnp.concatenate([_rms_rows(ao_scr[...], ga_ref[...]).astype(BF16),
                           _rms_rows(gated_ref[...], gs_ref[...]).astype(BF16)], axis=1)
    x1 = x_ref[...] + jnp.dot(mix, wo_ref[...], preferred_element_type=F32)
    o_ref[...] = x1
    rinv = lax.rsqrt(jnp.mean(x1 * x1, axis=-1, keepdims=True) + EPS)
    r_ref[...] = jnp.broadcast_to(rinv * rinv, r_ref.shape)


def _attn(q, k_parts, v_parts, k_maps, k_layout, bias, smax, sl, ga, x, gated, gs, wo, *, bq,
          n_units, k_step):
    nb, s, _ = q.shape
    nsel = bias.shape[0]
    tq = n_units * bq
    k_rows = [rows for _, rows in k_parts]
    kr = sum(n for _, _, n in k_layout)
    cur = lambda b, i: (b, i, 0)
    row = lambda w: pl.BlockSpec((None, tq, w), cur)
    full = lambda a: pl.BlockSpec(a.shape, lambda b, i: (0,) * a.ndim,
                                  pipeline_mode=pl.Buffered(1))
    kv_specs = [pl.BlockSpec((None, r, KV_WIDTH), m) for r, m in zip(k_rows, k_maps)]
    return pl.pallas_call(
        functools.partial(_attn_kernel, bq, n_units, k_step, nsel, len(k_parts), tuple(k_layout)),
        grid=(nb, s // tq),
        in_specs=([row(ATTN_WIDTH)] + kv_specs + kv_specs
                  + [full(bias), full(smax), full(sl), full(ga),
                     row(D_MODEL), row(SGU_WIDTH), full(gs), full(wo)]),
        out_specs=[row(D_MODEL), row(LANES)],
        out_shape=[jax.ShapeDtypeStruct((nb, s, D_MODEL), F32),
                   jax.ShapeDtypeStruct((nb, s, LANES), F32)],
        scratch_shapes=[pltpu.VMEM((N_KV_HEADS, 2, kr, KV_WIDTH), BF16),
                        pltpu.VMEM((N_KV_HEADS, 2, kr, KV_WIDTH), BF16),
                        pltpu.VMEM((tq, ATTN_WIDTH), F32)],
        compiler_params=pltpu.CompilerParams(dimension_semantics=("parallel", "arbitrary"),
                                             vmem_limit_bytes=ATTN_VMEM_LIMIT),
        name="attn_q%d" % bq,
    )(q, *[a for a, _ in k_parts], *[a for a, _ in v_parts], bias, smax, sl, ga, x, gated,
      gs, wo)


def _ffn_kernel(x_ref, r_ref, g_ref, wup_ref, wdn_ref, y_ref):
    tf = wup_ref.shape[1]
    xg = (x_ref[...] * g_ref[...]).astype(BF16)
    r2 = jnp.tile(r_ref[...], (1, FFN_SUB // LANES))
    for s in range(tf // FFN_SUB):
        cs = slice(s * FFN_SUB, (s + 1) * FFN_SUB)
        z = jnp.dot(xg, wup_ref[:, cs], preferred_element_type=F32)
        a = (jnp.square(jnp.maximum(z, 0.0)) * r2).astype(BF16)
        if s == 0:
            base = jnp.where(pl.program_id(1) == 0, x_ref[...], y_ref[...])
        else:
            base = y_ref[...]
        y_ref[...] = base + jnp.dot(a, wdn_ref[cs, :], preferred_element_type=F32)


def _ffn(x2d, r2d, g, wup_t, wdn, *, tm):
    t = x2d.shape[0]
    n_f, _, tf = wup_t.shape
    return pl.pallas_call(
        _ffn_kernel,
        grid=(t // tm, n_f),
        in_specs=[pl.BlockSpec((tm, D_MODEL), lambda i, j: (i, 0)),
                  pl.BlockSpec((tm, LANES), lambda i, j: (i, 0)),
                  pl.BlockSpec((1, D_MODEL), lambda i, j: (0, 0)),
                  pl.BlockSpec((None, D_MODEL, tf), lambda i, j: (j, 0, 0)),
                  pl.BlockSpec((tf, D_MODEL), lambda i, j: (j, 0))],
        out_specs=pl.BlockSpec((tm, D_MODEL), lambda i, j: (i, 0)),
        out_shape=jax.ShapeDtypeStruct((t, D_MODEL), F32),
        compiler_params=pltpu.CompilerParams(dimension_semantics=("parallel", "arbitrary"),
                                             vmem_limit_bytes=FFN_VMEM_LIMIT),
        name="ffn",
    )(x2d, r2d, g, wup_t, wdn)


def _band_tables(bq, nsel):
    qi = jnp.arange(bq)[:, None]
    kj = jnp.arange(BAND)[None, :] - WINDOW
    qc, kc = qi // CHUNK, jnp.floor_divide(kj, CHUNK)
    in_band = (kc <= qc) & (kc >= qc - WINDOW // CHUNK)
    valid = [in_band & (kj >= 0)] if nsel == 2 else []
    valid.append(in_band)
    return _t5_bucket(qi - kj), jnp.stack(valid).astype(jnp.int32)


def kernel(x_prompt, x_sample, cache_attn_k, cache_attn_v, rel_bias_table, ln_mix_g, w_in,
           q_norm_g, k_norm_g, attn_sinks, sgu_norm_g, sgu_w, sgu_b, out_norm_attn_g,
           out_norm_sgu_g, w_out, ln_ffn_g, w_ffn_up, w_ffn_down):
    bp, sp, _ = x_prompt.shape
    bs, ss, _ = x_sample.shape
    depth = w_in.shape[0]
    assert depth == 1 and ss == CHUNK and cache_attn_k.shape[2] == WINDOW
    l = 0
    xp = x_prompt.reshape(bp * sp, D_MODEL)
    xs = x_sample.reshape(bs * ss, D_MODEL)

    bq, units = 2 * CHUNK, ATTN_UNITS
    sinks = attn_sinks[l].reshape(N_HEADS)
    table_t = jnp.pad(rel_bias_table.T, ((0, 0), (0, LANES - N_BUCKETS)))
    bias_p, bias_s, smax_p, sl_p, smax_s, sl_s, bsb, win = _prep(
        table_t, sinks, *_band_tables(bq, 2), sgu_b[l], w_in[l], bq_s=ss)
    lng = ln_mix_g[l].reshape(1, D_MODEL)
    qg2 = q_norm_g[l].reshape(1, HEAD_DIM)
    kg2 = k_norm_g[l].reshape(1, HEAD_DIM)
    sg = sgu_norm_g[l].reshape(1, SGU_WIDTH)
    gs = out_norm_sgu_g[l].reshape(1, SGU_WIDTH)
    ga = out_norm_attn_g[l].reshape(1, ATTN_WIDTH)
    lnf = ln_ffn_g[l].reshape(1, D_MODEL)

    ws = sgu_w[l]

    q, k, v, gated, wup, wdn, wo = _proj(xp, lng, win, qg2, kg2, sg, ws, bsb,
                                         w_ffn_up[l], w_ffn_down[l], w_out[l],
                                         tm=PROJ_TM, chunk=SGU_CHUNK, emit_sv=False, tf=FFN_TF)
    k3 = k.reshape(bp, sp, KV_WIDTH)
    v3 = v.reshape(bp, sp, KV_WIDTH)
    prev = lambda b, i: (b, jnp.maximum(i * units - 1, 0), 0)
    cur = lambda b, i: (b, i, 0)
    x1p, r2p = _attn(q.reshape(bp, sp, ATTN_WIDTH),
                     [(k3, WINDOW), (k3, units * bq)], [(v3, WINDOW), (v3, units * bq)],
                     [prev, cur], [(0, 0, WINDOW), (1, 0, units * bq)],
                     bias_p, smax_p, sl_p, ga,
                     x_prompt, gated.reshape(bp, sp, SGU_WIDTH), gs, wo,
                     bq=bq, n_units=units, k_step=bq)
    yp = _ffn(x1p.reshape(bp * sp, D_MODEL), r2p.reshape(bp * sp, LANES), lnf, wup, wdn, tm=FFN_TM)

    qs, ks, vs, gated_s, svs = _proj(xs, lng, win, qg2, kg2, sg, ws, bsb,
                                    tm=PROJ_TM, chunk=ss, emit_sv=True)
    seqs = SAMPLE_SEQS
    grp = lambda a, rows: a.reshape(bs // seqs, seqs * rows, a.shape[-1])
    layout = []
    for i in range(seqs):
        layout += [(0, i * WINDOW, WINDOW), (1, i * ss, ss), (-1, 0, BAND - WINDOW - ss)]
    x1s, r2s = _attn(grp(qs, ss),
                     [(grp(cache_attn_k[l].reshape(bs, WINDOW, KV_WIDTH), WINDOW), seqs * WINDOW),
                      (grp(ks, ss), seqs * ss)],
                     [(grp(cache_attn_v[l].reshape(bs, WINDOW, KV_WIDTH), WINDOW), seqs * WINDOW),
                      (grp(vs, ss), seqs * ss)],
                     [cur, cur], layout, bias_s,
                     smax_s, sl_s, ga, grp(xs, ss), grp(gated_s, ss), gs, wo,
                     bq=ss, n_units=seqs, k_step=BAND)
    ys = _ffn(x1s.reshape(bs * ss, D_MODEL), r2s.reshape(bs * ss, LANES), lnf, wup, wdn, tm=FFN_TM)

    keep = min(WINDOW, sp)
    kv_shape = (N_KV_HEADS, HEAD_DIM)
    return (yp.reshape(bp, sp, D_MODEL),
            ys.reshape(bs, ss, D_MODEL),
            k3[:, -keep:].reshape(1, bp, keep, *kv_shape),
            v3[:, -keep:].reshape(1, bp, keep, *kv_shape),
            ks.reshape(1, bs, ss, *kv_shape),
            vs.reshape(1, bs, ss, *kv_shape),
            svs.reshape(1, bs, ss, SGU_WIDTH))
```

```python
import functools
import math

import jax
import jax.numpy as jnp
from jax import lax
from jax.experimental import pallas as pl
from jax.experimental.pallas import tpu as pltpu

D_MODEL = 2048
CHUNK = 64
ATTN_WIDTH = 1024
SGU_WIDTH = 1024
HEAD_DIM = 64
N_HEADS = ATTN_WIDTH // HEAD_DIM
N_KV_HEADS = 2
GQA_GROUP = N_HEADS // N_KV_HEADS
WINDOW = 128
N_BUCKETS = 32
MAX_DISTANCE = 128
SGU_CHUNK = 128
SGU_GROUPS = 8
SGU_GROUP_CH = SGU_WIDTH // SGU_GROUPS
D_FF = 4 * D_MODEL
KV_WIDTH = N_KV_HEADS * HEAD_DIM
IN_WIDTH = ATTN_WIDTH + 2 * KV_WIDTH + 2 * SGU_WIDTH
EPS = 1e-6
NEG_INF = -1e30

LANES = 128
VMEM_LIMIT = 56 * 1024 * 1024
FFN_VMEM_LIMIT = 62 * 1024 * 1024
PROJ_VMEM_LIMIT = 60 * 1024 * 1024
WIN_STAGE_ROWS = 256
PROJ_TM = 512
ATTN_UNITS = 4
SAMPLE_SEQS = 4
FFN_TM = 1024
FFN_TF = 1024
FFN_SUB = 1024

BF16 = jnp.bfloat16
F32 = jnp.float32

_KV0 = ATTN_WIDTH
_U0 = ATTN_WIDTH + 2 * KV_WIDTH
_V0 = _U0 + SGU_WIDTH


def _rms_rows(x, gain):
    ms = jnp.mean(x * x, axis=-1, keepdims=True)
    return x * lax.rsqrt(ms + EPS) * gain


def _head_pair_rms(blk, gain2, lane_lo):
    sq = blk * blk
    lo = jnp.sum(jnp.where(lane_lo, sq, 0.0), axis=-1, keepdims=True)
    hi = jnp.sum(jnp.where(lane_lo, 0.0, sq), axis=-1, keepdims=True)
    inv = 1.0 / HEAD_DIM
    r = jnp.where(lane_lo, lax.rsqrt(lo * inv + EPS), lax.rsqrt(hi * inv + EPS))
    return blk * r * gain2


def _cast_weights(wup_ref, wdn_ref, wo_ref, wupb_ref, wdnb_ref, wob_ref):
    tf = wupb_ref.shape[2]
    for j in range(wupb_ref.shape[0]):
        wupb_ref[j] = wup_ref[:, j * tf:(j + 1) * tf].astype(BF16)
    wdnb_ref[...] = wdn_ref[...].astype(BF16)
    wob_ref[...] = wo_ref[...].astype(BF16)


def _load_win(winf_hbm, winb_hbm, win_scr, stage, sems):
    step, last = pl.program_id(0), pl.num_programs(0) - 1
    rows = stage.shape[1]
    n_chunks = win_scr.shape[0] // rows
    chunk_in = lambda c: pltpu.make_async_copy(
        winf_hbm.at[pl.ds(c * rows, rows), :], stage.at[c % 2], sems.at[c % 2])
    write_out = pltpu.make_async_copy(win_scr, winb_hbm, sems.at[2])

    @pl.when(step == 0)
    def _():
        chunk_in(0).start()
        for c in range(n_chunks):
            if c + 1 < n_chunks:
                chunk_in(c + 1).start()
            chunk_in(c).wait()
            win_scr[c * rows:(c + 1) * rows, :] = stage[c % 2].astype(BF16)
        write_out.start()

    @pl.when(step == last)
    def _():
        write_out.wait()


def _proj_kernel(chunk, emit_sv, cast_w, x_ref, lng_ref, win_ref, qg_ref, kg_ref, sg_ref,
                 ws_ref, bs_ref, *rest):
    if cast_w:
        winb_hbm, win_scr, stage, sems = rest[-4:]
        _cast_weights(*rest[:3], *rest[-7:-4])
        rest = rest[3:-7]
        _load_win(win_ref, winb_hbm, win_scr, stage, sems)
        win_ref = win_scr
    q_ref, k_ref, v_ref, gated_ref = rest[:4]
    tm = x_ref.shape[0]
    h = _rms_rows(x_ref[...], lng_ref[...]).astype(BF16)
    lane_lo = lax.broadcasted_iota(jnp.int32, (tm, LANES), 1) < HEAD_DIM
    qg2 = jnp.concatenate([qg_ref[...], qg_ref[...]], axis=1)
    kg2 = jnp.concatenate([kg_ref[...], kg_ref[...]], axis=1)

    av = jax.nn.gelu(jnp.dot(h, win_ref[:, _V0:IN_WIDTH], preferred_element_type=F32))
    sv = _rms_rows(av, sg_ref[...])
    if emit_sv:
        rest[4][...] = sv
    svb = sv.astype(BF16)
    u = jax.nn.gelu(jnp.dot(h, win_ref[:, _U0:_V0], preferred_element_type=F32))

    zq = jnp.dot(h, win_ref[:, 0:ATTN_WIDTH], preferred_element_type=F32)
    for c in range(ATTN_WIDTH // LANES):
        sl = slice(c * LANES, (c + 1) * LANES)
        q_ref[:, sl] = (_head_pair_rms(zq[:, sl], qg2, lane_lo)
                        * (HEAD_DIM ** -0.5)).astype(BF16)

    zkv = jnp.dot(h, win_ref[:, _KV0:_U0], preferred_element_type=F32)
    k_ref[...] = _head_pair_rms(zkv[:, 0:KV_WIDTH], kg2, lane_lo)
    v_ref[...] = zkv[:, KV_WIDTH:2 * KV_WIDTH]

    row_c = lax.broadcasted_iota(jnp.int32, (chunk, chunk), 0) // CHUNK
    col_c = lax.broadcasted_iota(jnp.int32, (chunk, chunk), 1) // CHUNK
    n_chunks = tm // chunk
    for g in range(SGU_GROUPS):
        cs = slice(g * SGU_GROUP_CH, (g + 1) * SGU_GROUP_CH)
        w = jnp.where(col_c <= row_c, ws_ref[g, 0:chunk, 0:chunk], 0.0).astype(BF16)
        rhs = jnp.concatenate(
            [svb[n * chunk:(n + 1) * chunk, cs] for n in range(n_chunks)], axis=1)
        sp = jnp.dot(w, rhs, preferred_element_type=F32)
        for n in range(n_chunks):
            rs = slice(n * chunk, (n + 1) * chunk)
            gated_ref[rs, cs] = u[rs, cs] * (sp[:, n * SGU_GROUP_CH:(n + 1) * SGU_GROUP_CH]
                                             + bs_ref[g, 0:chunk, :])


def _proj(x2d, lng, win, qg2, kg2, sg, ws, bsb, wup=None, wdn=None, wo=None, *, tm, chunk,
          emit_sv, tf=None):
    t = x2d.shape[0]
    const = lambda shape: pl.BlockSpec(shape, lambda i: (0,) * len(shape),
                                       pipeline_mode=pl.Buffered(1))
    row = lambda w: pl.BlockSpec((tm, w), lambda i: (i, 0))
    out_shape = [jax.ShapeDtypeStruct((t, ATTN_WIDTH), BF16),
                 jax.ShapeDtypeStruct((t, KV_WIDTH), F32),
                 jax.ShapeDtypeStruct((t, KV_WIDTH), F32),
                 jax.ShapeDtypeStruct((t, SGU_WIDTH), F32)]
    out_specs = [row(ATTN_WIDTH), row(KV_WIDTH), row(KV_WIDTH), row(SGU_WIDTH)]
    if emit_sv:
        out_shape.append(jax.ShapeDtypeStruct((t, SGU_WIDTH), F32))
        out_specs.append(row(SGU_WIDTH))
    in_specs = [row(D_MODEL), const((1, D_MODEL)), const((D_MODEL, IN_WIDTH)),
                const((1, HEAD_DIM)), const((1, HEAD_DIM)), const((1, SGU_WIDTH)),
                const(ws.shape), const(bsb.shape)]
    args = [x2d, lng, win, qg2, kg2, sg, ws, bsb]
    scratch = []
    if wup is not None:
        in_specs[2] = pl.BlockSpec(memory_space=pl.ANY)
        scratch = [pltpu.VMEM(win.shape, BF16),
                   pltpu.VMEM((2, WIN_STAGE_ROWS, win.shape[1]), F32),
                   pltpu.SemaphoreType.DMA((3,))]
        n_steps = t // tm
        up_rows, dn_rows, wo_rows = D_MODEL // n_steps, D_FF // n_steps, wo.shape[0] // n_steps
        in_specs += [pl.BlockSpec((up_rows, D_FF), lambda i: (i, 0)),
                     pl.BlockSpec((dn_rows, D_MODEL), lambda i: (i, 0)),
                     pl.BlockSpec((wo_rows, D_MODEL), lambda i: (i, 0))]
        out_specs += [pl.BlockSpec((D_FF // tf, up_rows, tf), lambda i: (0, i, 0)),
                      pl.BlockSpec((dn_rows, D_MODEL), lambda i: (i, 0)),
                      pl.BlockSpec((wo_rows, D_MODEL), lambda i: (i, 0)),
                      pl.BlockSpec(memory_space=pl.ANY)]
        out_shape += [jax.ShapeDtypeStruct((D_FF // tf, D_MODEL, tf), BF16),
                      jax.ShapeDtypeStruct((D_FF, D_MODEL), BF16),
                      jax.ShapeDtypeStruct(wo.shape, BF16),
                      jax.ShapeDtypeStruct(win.shape, BF16)]
        args += [wup, wdn, wo]
    return pl.pallas_call(
        functools.partial(_proj_kernel, chunk, emit_sv, wup is not None),
        grid=(t // tm,),
        in_specs=in_specs,
        out_specs=out_specs,
        out_shape=out_shape,
        scratch_shapes=scratch,
        compiler_params=pltpu.CompilerParams(dimension_semantics=("arbitrary",),
                                             vmem_limit_bytes=PROJ_VMEM_LIMIT),
        name="proj_c%d" % chunk,
    )(*args)


BAND = 2 * LANES
PAIRS = GQA_GROUP // 2


def _prep_kernel(bq_s, tab_ref, sink_ref, bkt_ref, valid_ref, sgub_ref, bias_p_ref,
                 bias_s_ref, smax_p_ref, sl_p_ref, smax_s_ref, sl_s_ref, bsb_ref):
    c = bsb_ref.shape[0]
    eye = (lax.broadcasted_iota(jnp.int32, (c, c), 0) == lax.broadcasted_iota(jnp.int32, (c, c), 1))
    b_row = sgub_ref[pl.ds(pl.program_id(0), 1), :]
    bsb_ref[...] = jnp.broadcast_to(
        jnp.sum(jnp.where(eye, b_row, 0.0), axis=1, keepdims=True), bsb_ref.shape)
    bkt = bkt_ref[...]
    h0 = 2 * pl.program_id(0)
    for smax_ref, sl_ref in ((smax_p_ref, sl_p_ref), (smax_s_ref, sl_s_ref)):
        lane_lo = lax.broadcasted_iota(jnp.int32, sl_ref.shape, 1) < HEAD_DIM
        sl_ref[...] = jnp.where(lane_lo, sink_ref[h0], sink_ref[h0 + 1])
        for half in range(2):
            smax_ref[half] = jnp.full(sl_ref.shape, sink_ref[h0 + half], F32)
    for half in range(2):
        h = h0 + half
        row = jnp.broadcast_to(tab_ref[pl.ds(h, 1), :], (bkt.shape[0], LANES))
        acc = jnp.concatenate(
            [jnp.take_along_axis(row, bkt[:, c * LANES:(c + 1) * LANES], axis=1)
             for c in range(bkt.shape[1] // LANES)], axis=1)
        cs = slice(half * BAND, (half + 1) * BAND)
        for s in range(valid_ref.shape[0]):
            bias_p_ref[s, :, cs] = jnp.where(valid_ref[s] != 0, acc, NEG_INF)
        bias_s_ref[0, :, cs] = jnp.where(valid_ref[valid_ref.shape[0] - 1, 0:bq_s, :] != 0,
                                         acc[0:bq_s], NEG_INF)


def _prep(table, sinks, bucket, valid, sgu_b, *, bq_s):
    nsel, bq, _ = valid.shape
    n_steps = N_HEADS // 2
    assert n_steps == SGU_GROUPS
    pair = lambda n, rows: pl.BlockSpec((n, None, rows, 2 * BAND),
                                        lambda s: (0, s // PAIRS, s % PAIRS, 0))
    smax = lambda rows: pl.BlockSpec((None, 2, rows, LANES), lambda s: (s // PAIRS, 0, s % PAIRS, 0))
    sl = lambda rows: pl.BlockSpec((None, rows, LANES), lambda s: (s // PAIRS, s % PAIRS, 0))
    tables = lambda rows: [jax.ShapeDtypeStruct((N_KV_HEADS, 2, PAIRS * rows, LANES), F32),
                           jax.ShapeDtypeStruct((N_KV_HEADS, PAIRS * rows, LANES), F32)]
    return pl.pallas_call(
        functools.partial(_prep_kernel, bq_s),
        grid=(n_steps,),
        in_specs=[pl.BlockSpec(table.shape, lambda s: (0, 0)),
                  pl.BlockSpec(memory_space=pltpu.SMEM),
                  pl.BlockSpec(bucket.shape, lambda s: (0, 0)),
                  pl.BlockSpec(valid.shape, lambda s: (0, 0, 0)),
                  pl.BlockSpec(sgu_b.shape, lambda s: (0, 0))],
        out_specs=[pair(nsel, bq), pair(1, bq_s), smax(bq), sl(bq), smax(bq_s), sl(bq_s),
                   pl.BlockSpec((None, SGU_CHUNK, LANES), lambda s: (s, 0, 0))],
        out_shape=([jax.ShapeDtypeStruct((nsel, N_KV_HEADS, PAIRS * bq, 2 * BAND), F32),
                    jax.ShapeDtypeStruct((1, N_KV_HEADS, PAIRS * bq_s, 2 * BAND), F32)]
                   + tables(bq) + tables(bq_s)
                   + [jax.ShapeDtypeStruct((SGU_GROUPS, SGU_CHUNK, LANES), F32)]),
        compiler_params=pltpu.CompilerParams(dimension_semantics=("parallel",)),
        name="prep",
    )(table, sinks, bucket, valid, sgu_b)


def _t5_bucket(n):
    half = N_BUCKETS // 2
    max_exact = half // 2
    offset = jnp.where(n < 0, half, 0)
    a = jnp.abs(n)
    af = jnp.maximum(a, 1).astype(F32)
    large = max_exact + (jnp.log(af / max_exact) / math.log(MAX_DISTANCE / max_exact)
                         * (half - max_exact)).astype(jnp.int32)
    large = jnp.minimum(large, half - 1)
    return offset + jnp.where(a < max_exact, a, large)


def _attn_kernel(bq, n_units, k_step, nsel, n_kparts, k_layout, *refs):
    q_ref = refs[0]
    k_refs = refs[1:1 + n_kparts]
    v_refs = refs[1 + n_kparts:1 + 2 * n_kparts]
    (bias_ref, smax_ref, sl_ref, ga_ref, x_ref, gated_ref, gs_ref, wo_ref,
     o_ref, r_ref, kk_scr, vv_scr, ao_scr) = refs[1 + 2 * n_kparts:]
    rows = PAIRS * bq

    def split_heads(parts, scr):
        x = jnp.concatenate(
            [parts[p][r0:r0 + n, :] if p >= 0 else jnp.zeros((n, KV_WIDTH), F32)
             for p, r0, n in k_layout], axis=0)
        lo = lax.broadcasted_iota(jnp.int32, x.shape, 1) < HEAD_DIM
        xr = pltpu.roll(x, HEAD_DIM, 1)
        scr[0, 0] = jnp.where(lo, x, 0.0).astype(BF16)
        scr[0, 1] = jnp.where(lo, 0.0, xr).astype(BF16)
        scr[1, 0] = jnp.where(lo, xr, 0.0).astype(BF16)
        scr[1, 1] = jnp.where(lo, 0.0, x).astype(BF16)

    split_heads(k_refs, kk_scr)
    split_heads(v_refs, vv_scr)
    r_i = lax.broadcasted_iota(jnp.int32, (2 * BAND, LANES), 0)
    l_i = lax.broadcasted_iota(jnp.int32, (2 * BAND, LANES), 1)
    ones_blk = jnp.where((r_i < BAND) == (l_i < HEAD_DIM), 1.0, 0.0).astype(BF16)
    lane_lo = lax.broadcasted_iota(jnp.int32, (rows, LANES), 1) < HEAD_DIM
    first_block = pl.program_id(1) == 0

    for u in range(n_units):
        q0, k0 = u * bq, u * k_step
        sel = jnp.where(first_block, 0, 1) if (nsel > 1 and u == 0) else nsel - 1
        for g in range(N_KV_HEADS):
            cols = [slice((PAIRS * g + p) * LANES, (PAIRS * g + p + 1) * LANES)
                    for p in range(PAIRS)]
            qs = jnp.concatenate([q_ref[q0:q0 + bq, c] for c in cols], axis=0)
            k2 = jnp.concatenate([kk_scr[g, 0, k0:k0 + BAND, :],
                                  kk_scr[g, 1, k0:k0 + BAND, :]], axis=0)
            s = lax.dot_general(qs, k2, (((1,), (1,)), ((), ())),
                                preferred_element_type=F32) + bias_ref[sel, g]
            sa, sb = s[:, :BAND], s[:, BAND:]
            ma = jnp.max(jnp.maximum(jnp.maximum(sa[:, :LANES], sa[:, LANES:]), smax_ref[g, 0]),
                         axis=-1, keepdims=True)
            mb = jnp.max(jnp.maximum(jnp.maximum(sb[:, :LANES], sb[:, LANES:]), smax_ref[g, 1]),
                         axis=-1, keepdims=True)
            p = jnp.concatenate([jnp.exp(sa - ma), jnp.exp(sb - mb)], axis=1).astype(BF16)
            v2 = jnp.concatenate([vv_scr[g, 0, k0:k0 + BAND, :],
                                  vv_scr[g, 1, k0:k0 + BAND, :]], axis=0)
            ol = jnp.dot(p, jnp.concatenate([v2, ones_blk], axis=1), preferred_element_type=F32)
            denom = ol[:, LANES:] + jnp.exp(sl_ref[g] - jnp.where(lane_lo, ma, mb))
            out = ol[:, :LANES] / denom
            for p_i, c in enumerate(cols):
                ao_scr[q0:q0 + bq, c] = out[p_i * bq:(p_i + 1) * bq]

    mix = jnp.concatenate([_rms_rows(ao_scr[...], ga_ref[...]).astype(BF16),
                           _rms_rows(gated_ref[...], gs_ref[...]).astype(BF16)], axis=1)
    x1 = x_ref[...] + jnp.dot(mix, wo_ref[...], preferred_element_type=F32)
    o_ref[...] = x1
    rinv = lax.rsqrt(jnp.mean(x1 * x1, axis=-1, keepdims=True) + EPS)
    r_ref[...] = jnp.broadcast_to(rinv * rinv, r_ref.shape)


def _attn(q, k_parts, v_parts, k_maps, k_layout, bias, smax, sl, ga, x, gated, gs, wo, *, bq,
          n_units, k_step):
    nb, s, _ = q.shape
    nsel = bias.shape[0]
    tq = n_units * bq
    k_rows = [rows for _, rows in k_parts]
    kr = sum(n for _, _, n in k_layout)
    cur = lambda b, i: (b, i, 0)
    row = lambda w: pl.BlockSpec((None, tq, w), cur)
    full = lambda a: pl.BlockSpec(a.shape, lambda b, i: (0,) * a.ndim,
                                  pipeline_mode=pl.Buffered(1))
    kv_specs = [pl.BlockSpec((None, r, KV_WIDTH), m) for r, m in zip(k_rows, k_maps)]
    return pl.pallas_call(
        functools.partial(_attn_kernel, bq, n_units, k_step, nsel, len(k_parts), tuple(k_layout)),
        grid=(nb, s // tq),
        in_specs=([row(ATTN_WIDTH)] + kv_specs + kv_specs
                  + [full(bias), full(smax), full(sl), full(ga),
                     row(D_MODEL), row(SGU_WIDTH), full(gs), full(wo)]),
        out_specs=[row(D_MODEL), row(LANES)],
        out_shape=[jax.ShapeDtypeStruct((nb, s, D_MODEL), F32),
                   jax.ShapeDtypeStruct((nb, s, LANES), F32)],
        scratch_shapes=[pltpu.VMEM((N_KV_HEADS, 2, kr, KV_WIDTH), BF16),
                        pltpu.VMEM((N_KV_HEADS, 2, kr, KV_WIDTH), BF16),
                        pltpu.VMEM((tq, ATTN_WIDTH), F32)],
        compiler_params=pltpu.CompilerParams(dimension_semantics=("parallel", "arbitrary"),
                                             vmem_limit_bytes=VMEM_LIMIT),
        name="attn_q%d" % bq,
    )(q, *[a for a, _ in k_parts], *[a for a, _ in v_parts], bias, smax, sl, ga, x, gated,
      gs, wo)


def _ffn_kernel(x_ref, r_ref, g_ref, wup_ref, wdn_ref, y_ref):
    tf = wup_ref.shape[1]
    xg = (x_ref[...] * g_ref[...]).astype(BF16)
    r2 = jnp.tile(r_ref[...], (1, FFN_SUB // LANES))
    for s in range(tf // FFN_SUB):
        cs = slice(s * FFN_SUB, (s + 1) * FFN_SUB)
        z = jnp.dot(xg, wup_ref[:, cs], preferred_element_type=F32)
        a = (jnp.square(jnp.maximum(z, 0.0)) * r2).astype(BF16)
        if s == 0:
            base = jnp.where(pl.program_id(1) == 0, x_ref[...], y_ref[...])
        else:
            base = y_ref[...]
        y_ref[...] = base + jnp.dot(a, wdn_ref[cs, :], preferred_element_type=F32)


def _ffn(x2d, r2d, g, wup_t, wdn, *, tm):
    t = x2d.shape[0]
    n_f, _, tf = wup_t.shape
    return pl.pallas_call(
        _ffn_kernel,
        grid=(t // tm, n_f),
        in_specs=[pl.BlockSpec((tm, D_MODEL), lambda i, j: (i, 0)),
                  pl.BlockSpec((tm, LANES), lambda i, j: (i, 0)),
                  pl.BlockSpec((1, D_MODEL), lambda i, j: (0, 0)),
                  pl.BlockSpec((None, D_MODEL, tf), lambda i, j: (j, 0, 0)),
                  pl.BlockSpec((tf, D_MODEL), lambda i, j: (j, 0))],
        out_specs=pl.BlockSpec((tm, D_MODEL), lambda i, j: (i, 0)),
        out_shape=jax.ShapeDtypeStruct((t, D_MODEL), F32),
        compiler_params=pltpu.CompilerParams(dimension_semantics=("parallel", "arbitrary"),
                                             vmem_limit_bytes=FFN_VMEM_LIMIT),
        name="ffn",
    )(x2d, r2d, g, wup_t, wdn)


def _band_tables(bq, nsel):
    qi = jnp.arange(bq)[:, None]
    kj = jnp.arange(BAND)[None, :] - WINDOW
    qc, kc = qi // CHUNK, jnp.floor_divide(kj, CHUNK)
    in_band = (kc <= qc) & (kc >= qc - WINDOW // CHUNK)
    valid = [in_band & (kj >= 0)] if nsel == 2 else []
    valid.append(in_band)
    return _t5_bucket(qi - kj), jnp.stack(valid).astype(jnp.int32)


def kernel(x_prompt, x_sample, cache_attn_k, cache_attn_v, rel_bias_table, ln_mix_g, w_in,
           q_norm_g, k_norm_g, attn_sinks, sgu_norm_g, sgu_w, sgu_b, out_norm_attn_g,
           out_norm_sgu_g, w_out, ln_ffn_g, w_ffn_up, w_ffn_down):
    bp, sp, _ = x_prompt.shape
    bs, ss, _ = x_sample.shape
    depth = w_in.shape[0]
    assert depth == 1 and ss == CHUNK and cache_attn_k.shape[2] == WINDOW
    l = 0
    xp = x_prompt.reshape(bp * sp, D_MODEL)
    xs = x_sample.reshape(bs * ss, D_MODEL)

    bq, units = 2 * CHUNK, ATTN_UNITS
    sinks = attn_sinks[l].reshape(N_HEADS)
    table_t = jnp.pad(rel_bias_table.T, ((0, 0), (0, LANES - N_BUCKETS)))
    bias_p, bias_s, smax_p, sl_p, smax_s, sl_s, bsb = _prep(
        table_t, sinks, *_band_tables(bq, 2), sgu_b[l], bq_s=ss)
    lng = ln_mix_g[l].reshape(1, D_MODEL)
    qg2 = q_norm_g[l].reshape(1, HEAD_DIM)
    kg2 = k_norm_g[l].reshape(1, HEAD_DIM)
    sg = sgu_norm_g[l].reshape(1, SGU_WIDTH)
    gs = out_norm_sgu_g[l].reshape(1, SGU_WIDTH)
    ga = out_norm_attn_g[l].reshape(1, ATTN_WIDTH)
    lnf = ln_ffn_g[l].reshape(1, D_MODEL)

    ws = sgu_w[l]

    q, k, v, gated, wup, wdn, wo, win = _proj(xp, lng, w_in[l], qg2, kg2, sg, ws, bsb,
                                         w_ffn_up[l], w_ffn_down[l], w_out[l],
                                         tm=PROJ_TM, chunk=SGU_CHUNK, emit_sv=False, tf=FFN_TF)
    k3 = k.reshape(bp, sp, KV_WIDTH)
    v3 = v.reshape(bp, sp, KV_WIDTH)
    prev = lambda b, i: (b, jnp.maximum(i * units - 1, 0), 0)
    cur = lambda b, i: (b, i, 0)
    x1p, r2p = _attn(q.reshape(bp, sp, ATTN_WIDTH),
                     [(k3, WINDOW), (k3, units * bq)], [(v3, WINDOW), (v3, units * bq)],
                     [prev, cur], [(0, 0, WINDOW), (1, 0, units * bq)],
                     bias_p, smax_p, sl_p, ga,
                     x_prompt, gated.reshape(bp, sp, SGU_WIDTH), gs, wo,
                     bq=bq, n_units=units, k_step=bq)
    yp = _ffn(x1p.reshape(bp * sp, D_MODEL), r2p.reshape(bp * sp, LANES), lnf, wup, wdn, tm=FFN_TM)

    qs, ks, vs, gated_s, svs = _proj(xs, lng, win, qg2, kg2, sg, ws, bsb,
                                    tm=PROJ_TM, chunk=ss, emit_sv=True)
    seqs = SAMPLE_SEQS
    grp = lambda a, rows: a.reshape(bs // seqs, seqs * rows, a.shape[-1])
    layout = []
    for i in range(seqs):
        layout += [(0, i * WINDOW, WINDOW), (1, i * ss, ss), (-1, 0, BAND - WINDOW - ss)]
    x1s, r2s = _attn(grp(qs, ss),
                     [(grp(cache_attn_k[l].reshape(bs, WINDOW, KV_WIDTH), WINDOW), seqs * WINDOW),
                      (grp(ks, ss), seqs * ss)],
                     [(grp(cache_attn_v[l].reshape(bs, WINDOW, KV_WIDTH), WINDOW), seqs * WINDOW),
                      (grp(vs, ss), seqs * ss)],
                     [cur, cur], layout, bias_s,
                     smax_s, sl_s, ga, grp(xs, ss), grp(gated_s, ss), gs, wo,
                     bq=ss, n_units=seqs, k_step=BAND)
    ys = _ffn(x1s.reshape(bs * ss, D_MODEL), r2s.reshape(bs * ss, LANES), lnf, wup, wdn, tm=FFN_TM)

    keep = min(WINDOW, sp)
    kv_shape = (N_KV_HEADS, HEAD_DIM)
    return (yp.reshape(bp, sp, D_MODEL),
            ys.reshape(bs, ss, D_MODEL),
            k3[:, -keep:].reshape(1, bp, keep, *kv_shape),
            v3[:, -keep:].reshape(1, bp, keep, *kv_shape),
            ks.reshape(1, bs, ss, *kv_shape),
            vs.reshape(1, bs, ss, *kv_shape),
            svs.reshape(1, bs, ss, SGU_WIDTH))
```

```python
import functools
import math

import jax
import jax.numpy as jnp
from jax import lax
from jax.experimental import pallas as pl
from jax.experimental.pallas import tpu as pltpu

D_MODEL = 2048
CHUNK = 64
ATTN_WIDTH = 1024
SGU_WIDTH = 1024
HEAD_DIM = 64
N_HEADS = ATTN_WIDTH // HEAD_DIM
N_KV_HEADS = 2
GQA_GROUP = N_HEADS // N_KV_HEADS
WINDOW = 128
N_BUCKETS = 32
MAX_DISTANCE = 128
SGU_CHUNK = 128
SGU_GROUPS = 8
SGU_GROUP_CH = SGU_WIDTH // SGU_GROUPS
D_FF = 4 * D_MODEL
KV_WIDTH = N_KV_HEADS * HEAD_DIM
IN_WIDTH = ATTN_WIDTH + 2 * KV_WIDTH + 2 * SGU_WIDTH
EPS = 1e-6
NEG_INF = -1e30

LANES = 128
VMEM_LIMIT = 56 * 1024 * 1024
FFN_VMEM_LIMIT = 62 * 1024 * 1024
PROJ_VMEM_LIMIT = 60 * 1024 * 1024
WIN_STAGE_ROWS = 256
PROJ_TM = 512
ATTN_UNITS = 4
SAMPLE_SEQS = 4
FFN_TM = 1024
FFN_TF = 1024
FFN_SUB = 1024

BF16 = jnp.bfloat16
F32 = jnp.float32

_KV0 = ATTN_WIDTH
_U0 = ATTN_WIDTH + 2 * KV_WIDTH
_V0 = _U0 + SGU_WIDTH


def _rms_rows(x, gain):
    ms = jnp.mean(x * x, axis=-1, keepdims=True)
    return x * lax.rsqrt(ms + EPS) * gain


def _head_pair_rms(blk, gain2, lane_lo):
    sq = blk * blk
    lo = jnp.sum(jnp.where(lane_lo, sq, 0.0), axis=-1, keepdims=True)
    hi = jnp.sum(jnp.where(lane_lo, 0.0, sq), axis=-1, keepdims=True)
    inv = 1.0 / HEAD_DIM
    r = jnp.where(lane_lo, lax.rsqrt(lo * inv + EPS), lax.rsqrt(hi * inv + EPS))
    return blk * r * gain2


def _cast_weights(wup_ref, wdn_ref, wo_ref, wupb_ref, wdnb_ref, wob_ref):
    tf = wupb_ref.shape[2]
    for j in range(wupb_ref.shape[0]):
        wupb_ref[j] = wup_ref[:, j * tf:(j + 1) * tf].astype(BF16)
    wdnb_ref[...] = wdn_ref[...].astype(BF16)
    wob_ref[...] = wo_ref[...].astype(BF16)


def _load_win(winf_hbm, winb_hbm, win_scr, stage, sems):
    step, last = pl.program_id(0), pl.num_programs(0) - 1
    rows = stage.shape[1]
    n_chunks = win_scr.shape[0] // rows
    chunk_in = lambda c: pltpu.make_async_copy(
        winf_hbm.at[pl.ds(c * rows, rows), :], stage.at[c % 2], sems.at[c % 2])
    write_out = pltpu.make_async_copy(win_scr, winb_hbm, sems.at[2])

    @pl.when(step == 0)
    def _():
        chunk_in(0).start()
        for c in range(n_chunks):
            if c + 1 < n_chunks:
                chunk_in(c + 1).start()
            chunk_in(c).wait()
            win_scr[c * rows:(c + 1) * rows, :] = stage[c % 2].astype(BF16)
        write_out.start()

    @pl.when(step == last)
    def _():
        write_out.wait()


def _proj_kernel(chunk, emit_sv, cast_w, x_ref, lng_ref, win_ref, qg_ref, kg_ref, sg_ref,
                 ws_ref, bs_ref, *rest):
    if cast_w:
        winb_hbm, win_scr, stage, sems = rest[-4:]
        _cast_weights(*rest[:3], *rest[-7:-4])
        rest = rest[3:-7]
        _load_win(win_ref, winb_hbm, win_scr, stage, sems)
        win_ref = win_scr
    q_ref, k_ref, v_ref, gated_ref = rest[:4]
    tm = x_ref.shape[0]
    h = _rms_rows(x_ref[...], lng_ref[...]).astype(BF16)
    lane_lo = lax.broadcasted_iota(jnp.int32, (tm, LANES), 1) < HEAD_DIM
    qg2 = jnp.concatenate([qg_ref[...], qg_ref[...]], axis=1)
    kg2 = jnp.concatenate([kg_ref[...], kg_ref[...]], axis=1)

    av = jax.nn.gelu(jnp.dot(h, win_ref[:, _V0:IN_WIDTH], preferred_element_type=F32))
    sv = _rms_rows(av, sg_ref[...])
    if emit_sv:
        rest[4][...] = sv
    svb = sv.astype(BF16)
    u = jax.nn.gelu(jnp.dot(h, win_ref[:, _U0:_V0], preferred_element_type=F32))

    zq = jnp.dot(h, win_ref[:, 0:ATTN_WIDTH], preferred_element_type=F32)
    for c in range(ATTN_WIDTH // LANES):
        sl = slice(c * LANES, (c + 1) * LANES)
        q_ref[:, sl] = (_head_pair_rms(zq[:, sl], qg2, lane_lo)
                        * (HEAD_DIM ** -0.5)).astype(BF16)

    zkv = jnp.dot(h, win_ref[:, _KV0:_U0], preferred_element_type=F32)
    k_ref[...] = _head_pair_rms(zkv[:, 0:KV_WIDTH], kg2, lane_lo)
    v_ref[...] = zkv[:, KV_WIDTH:2 * KV_WIDTH]

    row_c = lax.broadcasted_iota(jnp.int32, (chunk, chunk), 0) // CHUNK
    col_c = lax.broadcasted_iota(jnp.int32, (chunk, chunk), 1) // CHUNK
    n_chunks = tm // chunk
    for g in range(SGU_GROUPS):
        cs = slice(g * SGU_GROUP_CH, (g + 1) * SGU_GROUP_CH)
        w = jnp.where(col_c <= row_c, ws_ref[g, 0:chunk, 0:chunk], 0.0).astype(BF16)
        rhs = jnp.concatenate(
            [svb[n * chunk:(n + 1) * chunk, cs] for n in range(n_chunks)], axis=1)
        sp = jnp.dot(w, rhs, preferred_element_type=F32)
        for n in range(n_chunks):
            rs = slice(n * chunk, (n + 1) * chunk)
            gated_ref[rs, cs] = u[rs, cs] * (sp[:, n * SGU_GROUP_CH:(n + 1) * SGU_GROUP_CH]
                                             + bs_ref[g, 0:chunk, :])


def _proj(x2d, lng, win, qg2, kg2, sg, ws, bsb, wup=None, wdn=None, wo=None, *, tm, chunk,
          emit_sv, tf=None):
    t = x2d.shape[0]
    const = lambda shape: pl.BlockSpec(shape, lambda i: (0,) * len(shape),
                                       pipeline_mode=pl.Buffered(1))
    row = lambda w: pl.BlockSpec((tm, w), lambda i: (i, 0))
    out_shape = [jax.ShapeDtypeStruct((t, ATTN_WIDTH), BF16),
                 jax.ShapeDtypeStruct((t, KV_WIDTH), F32),
                 jax.ShapeDtypeStruct((t, KV_WIDTH), F32),
                 jax.ShapeDtypeStruct((t, SGU_WIDTH), F32)]
    out_specs = [row(ATTN_WIDTH), row(KV_WIDTH), row(KV_WIDTH), row(SGU_WIDTH)]
    if emit_sv:
        out_shape.append(jax.ShapeDtypeStruct((t, SGU_WIDTH), F32))
        out_specs.append(row(SGU_WIDTH))
    in_specs = [row(D_MODEL), const((1, D_MODEL)), const((D_MODEL, IN_WIDTH)),
                const((1, HEAD_DIM)), const((1, HEAD_DIM)), const((1, SGU_WIDTH)),
                const(ws.shape), const(bsb.shape)]
    args = [x2d, lng, win, qg2, kg2, sg, ws, bsb]
    scratch = []
    if wup is not None:
        in_specs[2] = pl.BlockSpec(memory_space=pl.ANY)
        scratch = [pltpu.VMEM(win.shape, BF16),
                   pltpu.VMEM((2, WIN_STAGE_ROWS, win.shape[1]), F32),
                   pltpu.SemaphoreType.DMA((3,))]
        n_steps = t // tm
        up_rows, dn_rows, wo_rows = D_MODEL // n_steps, D_FF // n_steps, wo.shape[0] // n_steps
        in_specs += [pl.BlockSpec((up_rows, D_FF), lambda i: (i, 0)),
                     pl.BlockSpec((dn_rows, D_MODEL), lambda i: (i, 0)),
                     pl.BlockSpec((wo_rows, D_MODEL), lambda i: (i, 0))]
        out_specs += [pl.BlockSpec((D_FF // tf, up_rows, tf), lambda i: (0, i, 0)),
                      pl.BlockSpec((dn_rows, D_MODEL), lambda i: (i, 0)),
                      pl.BlockSpec((wo_rows, D_MODEL), lambda i: (i, 0)),
                      pl.BlockSpec(memory_space=pl.ANY)]
        out_shape += [jax.ShapeDtypeStruct((D_FF // tf, D_MODEL, tf), BF16),
                      jax.ShapeDtypeStruct((D_FF, D_MODEL), BF16),
                      jax.ShapeDtypeStruct(wo.shape, BF16),
                      jax.ShapeDtypeStruct(win.shape, BF16)]
        args += [wup, wdn, wo]
    return pl.pallas_call(
        functools.partial(_proj_kernel, chunk, emit_sv, wup is not None),
        grid=(t // tm,),
        in_specs=in_specs,
        out_specs=out_specs,
        out_shape=out_shape,
        scratch_shapes=scratch,
        compiler_params=pltpu.CompilerParams(dimension_semantics=("arbitrary",),
                                             vmem_limit_bytes=PROJ_VMEM_LIMIT),
        name="proj_c%d" % chunk,
    )(*args)


BAND = 2 * LANES
PAIRS = GQA_GROUP // 2


def _build_tables(bq, nsel, tab_ref, sink_ref, bkt_ref, valid_ref, bias_scr, smax_scr, sl_scr):
    bkt = bkt_ref[0:bq, :]
    lane_lo = lax.broadcasted_iota(jnp.int32, (bq, LANES), 1) < HEAD_DIM
    for h0 in range(0, N_HEADS, 2):
        g, pr = h0 // GQA_GROUP, (h0 % GQA_GROUP) // 2
        rs = slice(pr * bq, (pr + 1) * bq)
        sl_scr[g, rs, :] = jnp.where(lane_lo, sink_ref[h0], sink_ref[h0 + 1])
        for half in range(2):
            h = h0 + half
            smax_scr[g, half, rs, :] = jnp.full((bq, LANES), sink_ref[h], F32)
            row = jnp.broadcast_to(tab_ref[h:h + 1, :], (bq, LANES))
            acc = jnp.concatenate(
                [jnp.take_along_axis(row, bkt[:, c * LANES:(c + 1) * LANES], axis=1)
                 for c in range(BAND // LANES)], axis=1)
            for s in range(nsel):
                ok = valid_ref[valid_ref.shape[0] - nsel + s, 0:bq, :] != 0
                bias_scr[s, g, rs, half * BAND:(half + 1) * BAND] = jnp.where(ok, acc, NEG_INF)


def _t5_bucket(n):
    half = N_BUCKETS // 2
    max_exact = half // 2
    offset = jnp.where(n < 0, half, 0)
    a = jnp.abs(n)
    af = jnp.maximum(a, 1).astype(F32)
    large = max_exact + (jnp.log(af / max_exact) / math.log(MAX_DISTANCE / max_exact)
                         * (half - max_exact)).astype(jnp.int32)
    large = jnp.minimum(large, half - 1)
    return offset + jnp.where(a < max_exact, a, large)


def _attn_kernel(bq, n_units, k_step, nsel, n_kparts, k_layout, *refs):
    q_ref = refs[0]
    k_refs = refs[1:1 + n_kparts]
    v_refs = refs[1 + n_kparts:1 + 2 * n_kparts]
    (tab_ref, sink_ref, bkt_ref, valid_ref, ga_ref, x_ref, gated_ref, gs_ref, wo_ref,
     o_ref, r_ref, kk_scr, vv_scr, ao_scr, bias_ref, smax_ref, sl_ref) = refs[1 + 2 * n_kparts:]
    rows = PAIRS * bq

    @pl.when((pl.program_id(0) == 0) & (pl.program_id(1) == 0))
    def _():
        _build_tables(bq, nsel, tab_ref, sink_ref, bkt_ref, valid_ref, bias_ref, smax_ref, sl_ref)

    def split_heads(parts, scr):
        x = jnp.concatenate(
            [parts[p][r0:r0 + n, :] if p >= 0 else jnp.zeros((n, KV_WIDTH), F32)
             for p, r0, n in k_layout], axis=0)
        lo = lax.broadcasted_iota(jnp.int32, x.shape, 1) < HEAD_DIM
        xr = pltpu.roll(x, HEAD_DIM, 1)
        scr[0, 0] = jnp.where(lo, x, 0.0).astype(BF16)
        scr[0, 1] = jnp.where(lo, 0.0, xr).astype(BF16)
        scr[1, 0] = jnp.where(lo, xr, 0.0).astype(BF16)
        scr[1, 1] = jnp.where(lo, 0.0, x).astype(BF16)

    split_heads(k_refs, kk_scr)
    split_heads(v_refs, vv_scr)
    r_i = lax.broadcasted_iota(jnp.int32, (2 * BAND, LANES), 0)
    l_i = lax.broadcasted_iota(jnp.int32, (2 * BAND, LANES), 1)
    ones_blk = jnp.where((r_i < BAND) == (l_i < HEAD_DIM), 1.0, 0.0).astype(BF16)
    lane_lo = lax.broadcasted_iota(jnp.int32, (rows, LANES), 1) < HEAD_DIM
    first_block = pl.program_id(1) == 0

    for u in range(n_units):
        q0, k0 = u * bq, u * k_step
        sel = jnp.where(first_block, 0, 1) if (nsel > 1 and u == 0) else nsel - 1
        for g in range(N_KV_HEADS):
            cols = [slice((PAIRS * g + p) * LANES, (PAIRS * g + p + 1) * LANES)
                    for p in range(PAIRS)]
            qs = jnp.concatenate([q_ref[q0:q0 + bq, c] for c in cols], axis=0)
            k2 = jnp.concatenate([kk_scr[g, 0, k0:k0 + BAND, :],
                                  kk_scr[g, 1, k0:k0 + BAND, :]], axis=0)
            s = lax.dot_general(qs, k2, (((1,), (1,)), ((), ())),
                                preferred_element_type=F32) + bias_ref[sel, g]
            sa, sb = s[:, :BAND], s[:, BAND:]
            ma = jnp.max(jnp.maximum(jnp.maximum(sa[:, :LANES], sa[:, LANES:]), smax_ref[g, 0]),
                         axis=-1, keepdims=True)
            mb = jnp.max(jnp.maximum(jnp.maximum(sb[:, :LANES], sb[:, LANES:]), smax_ref[g, 1]),
                         axis=-1, keepdims=True)
            p = jnp.concatenate([jnp.exp(sa - ma), jnp.exp(sb - mb)], axis=1).astype(BF16)
            v2 = jnp.concatenate([vv_scr[g, 0, k0:k0 + BAND, :],
                                  vv_scr[g, 1, k0:k0 + BAND, :]], axis=0)
            ol = jnp.dot(p, jnp.concatenate([v2, ones_blk], axis=1), preferred_element_type=F32)
            denom = ol[:, LANES:] + jnp.exp(sl_ref[g] - jnp.where(lane_lo, ma, mb))
            out = ol[:, :LANES] / denom
            for p_i, c in enumerate(cols):
                ao_scr[q0:q0 + bq, c] = out[p_i * bq:(p_i + 1) * bq]

    mix = jnp.concatenate([_rms_rows(ao_scr[...], ga_ref[...]).astype(BF16),
                           _rms_rows(gated_ref[...], gs_ref[...]).astype(BF16)], axis=1)
    x1 = x_ref[...] + jnp.dot(mix, wo_ref[...], preferred_element_type=F32)
    o_ref[...] = x1
    rinv = lax.rsqrt(jnp.mean(x1 * x1, axis=-1, keepdims=True) + EPS)
    r_ref[...] = jnp.broadcast_to(rinv * rinv, r_ref.shape)


def _attn(q, k_parts, v_parts, k_maps, k_layout, table, sinks, bucket, valid, ga, x, gated, gs,
          wo, *, bq, n_units, k_step, nsel):
    nb, s, _ = q.shape
    tq = n_units * bq
    k_rows = [rows for _, rows in k_parts]
    kr = sum(n for _, _, n in k_layout)
    cur = lambda b, i: (b, i, 0)
    row = lambda w: pl.BlockSpec((None, tq, w), cur)
    full = lambda a: pl.BlockSpec(a.shape, lambda b, i: (0,) * a.ndim,
                                  pipeline_mode=pl.Buffered(1))
    kv_specs = [pl.BlockSpec((None, r, KV_WIDTH), m) for r, m in zip(k_rows, k_maps)]
    return pl.pallas_call(
        functools.partial(_attn_kernel, bq, n_units, k_step, nsel, len(k_parts), tuple(k_layout)),
        grid=(nb, s // tq),
        in_specs=([row(ATTN_WIDTH)] + kv_specs + kv_specs
                  + [full(table), pl.BlockSpec(memory_space=pltpu.SMEM), full(bucket), full(valid),
                     full(ga), row(D_MODEL), row(SGU_WIDTH), full(gs), full(wo)]),
        out_specs=[row(D_MODEL), row(LANES)],
        out_shape=[jax.ShapeDtypeStruct((nb, s, D_MODEL), F32),
                   jax.ShapeDtypeStruct((nb, s, LANES), F32)],
        scratch_shapes=[pltpu.VMEM((N_KV_HEADS, 2, kr, KV_WIDTH), BF16),
                        pltpu.VMEM((N_KV_HEADS, 2, kr, KV_WIDTH), BF16),
                        pltpu.VMEM((tq, ATTN_WIDTH), F32),
                        pltpu.VMEM((nsel, N_KV_HEADS, PAIRS * bq, 2 * BAND), F32),
                        pltpu.VMEM((N_KV_HEADS, 2, PAIRS * bq, LANES), F32),
                        pltpu.VMEM((N_KV_HEADS, PAIRS * bq, LANES), F32)],
        compiler_params=pltpu.CompilerParams(dimension_semantics=("arbitrary", "arbitrary"),
                                             vmem_limit_bytes=VMEM_LIMIT),
        name="attn_q%d" % bq,
    )(q, *[a for a, _ in k_parts], *[a for a, _ in v_parts], table, sinks, bucket, valid, ga, x,
      gated, gs, wo)


def _ffn_kernel(x_ref, r_ref, g_ref, wup_ref, wdn_ref, y_ref):
    tf = wup_ref.shape[1]
    xg = (x_ref[...] * g_ref[...]).astype(BF16)
    r2 = jnp.tile(r_ref[...], (1, FFN_SUB // LANES))
    for s in range(tf // FFN_SUB):
        cs = slice(s * FFN_SUB, (s + 1) * FFN_SUB)
        z = jnp.dot(xg, wup_ref[:, cs], preferred_element_type=F32)
        a = (jnp.square(jnp.maximum(z, 0.0)) * r2).astype(BF16)
        if s == 0:
            base = jnp.where(pl.program_id(1) == 0, x_ref[...], y_ref[...])
        else:
            base = y_ref[...]
        y_ref[...] = base + jnp.dot(a, wdn_ref[cs, :], preferred_element_type=F32)


def _ffn(x2d, r2d, g, wup_t, wdn, *, tm):
    t = x2d.shape[0]
    n_f, _, tf = wup_t.shape
    return pl.pallas_call(
        _ffn_kernel,
        grid=(t // tm, n_f),
        in_specs=[pl.BlockSpec((tm, D_MODEL), lambda i, j: (i, 0)),
                  pl.BlockSpec((tm, LANES), lambda i, j: (i, 0)),
                  pl.BlockSpec((1, D_MODEL), lambda i, j: (0, 0)),
                  pl.BlockSpec((None, D_MODEL, tf), lambda i, j: (j, 0, 0)),
                  pl.BlockSpec((tf, D_MODEL), lambda i, j: (j, 0))],
        out_specs=pl.BlockSpec((tm, D_MODEL), lambda i, j: (i, 0)),
        out_shape=jax.ShapeDtypeStruct((t, D_MODEL), F32),
        compiler_params=pltpu.CompilerParams(dimension_semantics=("parallel", "arbitrary"),
                                             vmem_limit_bytes=FFN_VMEM_LIMIT),
        name="ffn",
    )(x2d, r2d, g, wup_t, wdn)


def _band_tables(bq, nsel):
    qi = jnp.arange(bq)[:, None]
    kj = jnp.arange(BAND)[None, :] - WINDOW
    qc, kc = qi // CHUNK, jnp.floor_divide(kj, CHUNK)
    in_band = (kc <= qc) & (kc >= qc - WINDOW // CHUNK)
    valid = [in_band & (kj >= 0)] if nsel == 2 else []
    valid.append(in_band)
    return _t5_bucket(qi - kj), jnp.stack(valid).astype(jnp.int32)


def kernel(x_prompt, x_sample, cache_attn_k, cache_attn_v, rel_bias_table, ln_mix_g, w_in,
           q_norm_g, k_norm_g, attn_sinks, sgu_norm_g, sgu_w, sgu_b, out_norm_attn_g,
           out_norm_sgu_g, w_out, ln_ffn_g, w_ffn_up, w_ffn_down):
    bp, sp, _ = x_prompt.shape
    bs, ss, _ = x_sample.shape
    depth = w_in.shape[0]
    assert depth == 1 and ss == CHUNK and cache_attn_k.shape[2] == WINDOW
    l = 0
    xp = x_prompt.reshape(bp * sp, D_MODEL)
    xs = x_sample.reshape(bs * ss, D_MODEL)

    bq, units = 2 * CHUNK, ATTN_UNITS
    sinks = attn_sinks[l].reshape(N_HEADS)
    table_t = jnp.pad(rel_bias_table.T, ((0, 0), (0, LANES - N_BUCKETS)))
    bucket, valid = _band_tables(bq, 2)
    bsb = jnp.broadcast_to(sgu_b[l][:, :, None], (SGU_GROUPS, SGU_CHUNK, LANES))
    lng = ln_mix_g[l].reshape(1, D_MODEL)
    qg2 = q_norm_g[l].reshape(1, HEAD_DIM)
    kg2 = k_norm_g[l].reshape(1, HEAD_DIM)
    sg = sgu_norm_g[l].reshape(1, SGU_WIDTH)
    gs = out_norm_sgu_g[l].reshape(1, SGU_WIDTH)
    ga = out_norm_attn_g[l].reshape(1, ATTN_WIDTH)
    lnf = ln_ffn_g[l].reshape(1, D_MODEL)

    ws = sgu_w[l]

    q, k, v, gated, wup, wdn, wo, win = _proj(xp, lng, w_in[l], qg2, kg2, sg, ws, bsb,
                                         w_ffn_up[l], w_ffn_down[l], w_out[l],
                                         tm=PROJ_TM, chunk=SGU_CHUNK, emit_sv=False, tf=FFN_TF)
    k3 = k.reshape(bp, sp, KV_WIDTH)
    v3 = v.reshape(bp, sp, KV_WIDTH)
    prev = lambda b, i: (b, jnp.maximum(i * units - 1, 0), 0)
    cur = lambda b, i: (b, i, 0)
    x1p, r2p = _attn(q.reshape(bp, sp, ATTN_WIDTH),
                     [(k3, WINDOW), (k3, units * bq)], [(v3, WINDOW), (v3, units * bq)],
                     [prev, cur], [(0, 0, WINDOW), (1, 0, units * bq)],
                     table_t, sinks, bucket, valid, ga,
                     x_prompt, gated.reshape(bp, sp, SGU_WIDTH), gs, wo,
                     bq=bq, n_units=units, k_step=bq, nsel=2)
    yp = _ffn(x1p.reshape(bp * sp, D_MODEL), r2p.reshape(bp * sp, LANES), lnf, wup, wdn, tm=FFN_TM)

    qs, ks, vs, gated_s, svs = _proj(xs, lng, win, qg2, kg2, sg, ws, bsb,
                                    tm=PROJ_TM, chunk=ss, emit_sv=True)
    seqs = SAMPLE_SEQS
    grp = lambda a, rows: a.reshape(bs // seqs, seqs * rows, a.shape[-1])
    layout = []
    for i in range(seqs):
        layout += [(0, i * WINDOW, WINDOW), (1, i * ss, ss), (-1, 0, BAND - WINDOW - ss)]
    x1s, r2s = _attn(grp(qs, ss),
                     [(grp(cache_attn_k[l].reshape(bs, WINDOW, KV_WIDTH), WINDOW), seqs * WINDOW),
                      (grp(ks, ss), seqs * ss)],
                     [(grp(cache_attn_v[l].reshape(bs, WINDOW, KV_WIDTH), WINDOW), seqs * WINDOW),
                      (grp(vs, ss), seqs * ss)],
                     [cur, cur], layout, table_t, sinks, bucket, valid,
                     ga, grp(xs, ss), grp(gated_s, ss), gs, wo,
                     bq=ss, n_units=seqs, k_step=BAND, nsel=1)
    ys = _ffn(x1s.reshape(bs * ss, D_MODEL), r2s.reshape(bs * ss, LANES), lnf, wup, wdn, tm=FFN_TM)

    keep = min(WINDOW, sp)
    kv_shape = (N_KV_HEADS, HEAD_DIM)
    return (yp.reshape(bp, sp, D_MODEL),
            ys.reshape(bs, ss, D_MODEL),
            k3[:, -keep:].reshape(1, bp, keep, *kv_shape),
            v3[:, -keep:].reshape(1, bp, keep, *kv_shape),
            ks.reshape(1, bs, ss, *kv_shape),
            vs.reshape(1, bs, ss, *kv_shape),
            svs.reshape(1, bs, ss, SGU_WIDTH))
```

```python
import functools
import math

import jax
import jax.numpy as jnp
from jax import lax
from jax.experimental import pallas as pl
from jax.experimental.pallas import tpu as pltpu

D_MODEL = 2048
CHUNK = 64
ATTN_WIDTH = 1024
SGU_WIDTH = 1024
HEAD_DIM = 64
N_HEADS = ATTN_WIDTH // HEAD_DIM
N_KV_HEADS = 2
GQA_GROUP = N_HEADS // N_KV_HEADS
WINDOW = 128
N_BUCKETS = 32
MAX_DISTANCE = 128
SGU_CHUNK = 128
SGU_GROUPS = 8
SGU_GROUP_CH = SGU_WIDTH // SGU_GROUPS
D_FF = 4 * D_MODEL
KV_WIDTH = N_KV_HEADS * HEAD_DIM
IN_WIDTH = ATTN_WIDTH + 2 * KV_WIDTH + 2 * SGU_WIDTH
EPS = 1e-6
NEG_INF = -1e30

LANES = 128
VMEM_LIMIT = 56 * 1024 * 1024
FFN_VMEM_LIMIT = 62 * 1024 * 1024
PROJ_VMEM_LIMIT = 60 * 1024 * 1024
WIN_STAGE_ROWS = 256
PROJ_TM = 512
ATTN_UNITS = 4
SAMPLE_SEQS = 4
FFN_TM = 1024
FFN_TF = 1024
FFN_SUB = 1024

BF16 = jnp.bfloat16
F32 = jnp.float32

_KV0 = ATTN_WIDTH
_U0 = ATTN_WIDTH + 2 * KV_WIDTH
_V0 = _U0 + SGU_WIDTH


def _rms_rows(x, gain):
    ms = jnp.mean(x * x, axis=-1, keepdims=True)
    return x * lax.rsqrt(ms + EPS) * gain


def _head_pair_rms(blk, gain2, lane_lo):
    sq = blk * blk
    lo = jnp.sum(jnp.where(lane_lo, sq, 0.0), axis=-1, keepdims=True)
    hi = jnp.sum(jnp.where(lane_lo, 0.0, sq), axis=-1, keepdims=True)
    inv = 1.0 / HEAD_DIM
    r = jnp.where(lane_lo, lax.rsqrt(lo * inv + EPS), lax.rsqrt(hi * inv + EPS))
    return blk * r * gain2


def _cast_weights(wup_ref, wdn_ref, wo_ref, wupb_ref, wdnb_ref, wob_ref):
    tf = wupb_ref.shape[2]
    for j in range(wupb_ref.shape[0]):
        wupb_ref[j] = wup_ref[:, j * tf:(j + 1) * tf].astype(BF16)
    wdnb_ref[...] = wdn_ref[...].astype(BF16)
    wob_ref[...] = wo_ref[...].astype(BF16)


def _load_win(winf_hbm, winb_hbm, win_scr, stage, sems):
    step, last = pl.program_id(0), pl.num_programs(0) - 1
    rows = stage.shape[1]
    n_chunks = win_scr.shape[0] // rows
    chunk_in = lambda c: pltpu.make_async_copy(
        winf_hbm.at[pl.ds(c * rows, rows), :], stage.at[c % 2], sems.at[c % 2])
    write_out = pltpu.make_async_copy(win_scr, winb_hbm, sems.at[2])

    @pl.when(step == 0)
    def _():
        chunk_in(0).start()
        for c in range(n_chunks):
            if c + 1 < n_chunks:
                chunk_in(c + 1).start()
            chunk_in(c).wait()
            win_scr[c * rows:(c + 1) * rows, :] = stage[c % 2].astype(BF16)
        write_out.start()

    @pl.when(step == last)
    def _():
        write_out.wait()


def _proj_kernel(chunk, emit_sv, cast_w, x_ref, lng_ref, win_ref, qg_ref, kg_ref, sg_ref,
                 ws_ref, bs_ref, *rest):
    if cast_w:
        winb_hbm, win_scr, stage, sems = rest[-4:]
        _cast_weights(*rest[:3], *rest[-7:-4])
        rest = rest[3:-7]
        _load_win(win_ref, winb_hbm, win_scr, stage, sems)
        win_ref = win_scr
    q_ref, k_ref, v_ref, gated_ref = rest[:4]
    tm = x_ref.shape[0]
    h = _rms_rows(x_ref[...], lng_ref[...]).astype(BF16)
    lane_lo = lax.broadcasted_iota(jnp.int32, (tm, LANES), 1) < HEAD_DIM
    qg2 = jnp.concatenate([qg_ref[...], qg_ref[...]], axis=1)
    kg2 = jnp.concatenate([kg_ref[...], kg_ref[...]], axis=1)

    av = jax.nn.gelu(jnp.dot(h, win_ref[:, _V0:IN_WIDTH], preferred_element_type=F32))
    sv = _rms_rows(av, sg_ref[...])
    if emit_sv:
        rest[4][...] = sv
    svb = sv.astype(BF16)
    u = jax.nn.gelu(jnp.dot(h, win_ref[:, _U0:_V0], preferred_element_type=F32))

    zq = jnp.dot(h, win_ref[:, 0:ATTN_WIDTH], preferred_element_type=F32)
    for c in range(ATTN_WIDTH // LANES):
        sl = slice(c * LANES, (c + 1) * LANES)
        q_ref[:, sl] = (_head_pair_rms(zq[:, sl], qg2, lane_lo)
                        * (HEAD_DIM ** -0.5)).astype(BF16)

    zkv = jnp.dot(h, win_ref[:, _KV0:_U0], preferred_element_type=F32)
    k_ref[...] = _head_pair_rms(zkv[:, 0:KV_WIDTH], kg2, lane_lo)
    v_ref[...] = zkv[:, KV_WIDTH:2 * KV_WIDTH]

    row_c = lax.broadcasted_iota(jnp.int32, (chunk, chunk), 0) // CHUNK
    col_c = lax.broadcasted_iota(jnp.int32, (chunk, chunk), 1) // CHUNK
    n_chunks = tm // chunk
    for g in range(SGU_GROUPS):
        cs = slice(g * SGU_GROUP_CH, (g + 1) * SGU_GROUP_CH)
        w = jnp.where(col_c <= row_c, ws_ref[g, 0:chunk, 0:chunk], 0.0).astype(BF16)
        rhs = jnp.concatenate(
            [svb[n * chunk:(n + 1) * chunk, cs] for n in range(n_chunks)], axis=1)
        sp = jnp.dot(w, rhs, preferred_element_type=F32)
        for n in range(n_chunks):
            rs = slice(n * chunk, (n + 1) * chunk)
            gated_ref[rs, cs] = u[rs, cs] * (sp[:, n * SGU_GROUP_CH:(n + 1) * SGU_GROUP_CH]
                                             + bs_ref[g, 0:chunk, :])


def _proj(x2d, lng, win, qg2, kg2, sg, ws, bsb, wup=None, wdn=None, wo=None, *, tm, chunk,
          emit_sv, tf=None):
    t = x2d.shape[0]
    const = lambda shape: pl.BlockSpec(shape, lambda i: (0,) * len(shape),
                                       pipeline_mode=pl.Buffered(1))
    row = lambda w: pl.BlockSpec((tm, w), lambda i: (i, 0))
    out_shape = [jax.ShapeDtypeStruct((t, ATTN_WIDTH), BF16),
                 jax.ShapeDtypeStruct((t, KV_WIDTH), F32),
                 jax.ShapeDtypeStruct((t, KV_WIDTH), F32),
                 jax.ShapeDtypeStruct((t, SGU_WIDTH), F32)]
    out_specs = [row(ATTN_WIDTH), row(KV_WIDTH), row(KV_WIDTH), row(SGU_WIDTH)]
    if emit_sv:
        out_shape.append(jax.ShapeDtypeStruct((t, SGU_WIDTH), F32))
        out_specs.append(row(SGU_WIDTH))
    in_specs = [row(D_MODEL), const((1, D_MODEL)), const((D_MODEL, IN_WIDTH)),
                const((1, HEAD_DIM)), const((1, HEAD_DIM)), const((1, SGU_WIDTH)),
                const(ws.shape), const(bsb.shape)]
    args = [x2d, lng, win, qg2, kg2, sg, ws, bsb]
    scratch = []
    if wup is not None:
        in_specs[2] = pl.BlockSpec(memory_space=pl.ANY)
        scratch = [pltpu.VMEM(win.shape, BF16),
                   pltpu.VMEM((2, WIN_STAGE_ROWS, win.shape[1]), F32),
                   pltpu.SemaphoreType.DMA((3,))]
        n_steps = t // tm
        up_rows, dn_rows, wo_rows = D_MODEL // n_steps, D_FF // n_steps, wo.shape[0] // n_steps
        in_specs += [pl.BlockSpec((up_rows, D_FF), lambda i: (i, 0)),
                     pl.BlockSpec((dn_rows, D_MODEL), lambda i: (i, 0)),
                     pl.BlockSpec((wo_rows, D_MODEL), lambda i: (i, 0))]
        out_specs += [pl.BlockSpec((D_FF // tf, up_rows, tf), lambda i: (0, i, 0)),
                      pl.BlockSpec((dn_rows, D_MODEL), lambda i: (i, 0)),
                      pl.BlockSpec((wo_rows, D_MODEL), lambda i: (i, 0)),
                      pl.BlockSpec(memory_space=pl.ANY)]
        out_shape += [jax.ShapeDtypeStruct((D_FF // tf, D_MODEL, tf), BF16),
                      jax.ShapeDtypeStruct((D_FF, D_MODEL), BF16),
                      jax.ShapeDtypeStruct(wo.shape, BF16),
                      jax.ShapeDtypeStruct(win.shape, BF16)]
        args += [wup, wdn, wo]
    return pl.pallas_call(
        functools.partial(_proj_kernel, chunk, emit_sv, wup is not None),
        grid=(t // tm,),
        in_specs=in_specs,
        out_specs=out_specs,
        out_shape=out_shape,
        scratch_shapes=scratch,
        compiler_params=pltpu.CompilerParams(dimension_semantics=("arbitrary",),
                                             vmem_limit_bytes=PROJ_VMEM_LIMIT),
        name="proj_c%d" % chunk,
    )(*args)


BAND = 2 * LANES
PAIRS = GQA_GROUP // 2


def _build_tables(bq, nsel, tab_ref, sink_ref, bkt_ref, valid_ref, bias_scr, smax_scr, sl_scr):
    bkt = bkt_ref[0:bq, :]
    lane_lo = lax.broadcasted_iota(jnp.int32, (bq, LANES), 1) < HEAD_DIM
    for h0 in range(0, N_HEADS, 2):
        g, pr = h0 // GQA_GROUP, (h0 % GQA_GROUP) // 2
        rs = slice(pr * bq, (pr + 1) * bq)
        sl_scr[g, rs, :] = jnp.where(lane_lo, sink_ref[h0], sink_ref[h0 + 1])
        for half in range(2):
            h = h0 + half
            smax_scr[g, half, rs, :] = jnp.full((bq, LANES), sink_ref[h], F32)
            row = jnp.broadcast_to(tab_ref[h:h + 1, :], (bq, LANES))
            acc = jnp.concatenate(
                [jnp.take_along_axis(row, bkt[:, c * LANES:(c + 1) * LANES], axis=1)
                 for c in range(BAND // LANES)], axis=1)
            for s in range(nsel):
                ok = valid_ref[valid_ref.shape[0] - nsel + s, 0:bq, :] != 0
                bias_scr[s, g, rs, half * BAND:(half + 1) * BAND] = jnp.where(ok, acc, NEG_INF)


def _t5_bucket(n):
    half = N_BUCKETS // 2
    max_exact = half // 2
    offset = jnp.where(n < 0, half, 0)
    a = jnp.abs(n)
    af = jnp.maximum(a, 1).astype(F32)
    large = max_exact + (jnp.log(af / max_exact) / math.log(MAX_DISTANCE / max_exact)
                         * (half - max_exact)).astype(jnp.int32)
    large = jnp.minimum(large, half - 1)
    return offset + jnp.where(a < max_exact, a, large)


def _attn_kernel(bq, n_units, k_step, nsel, n_kparts, k_layout, *refs):
    q_ref = refs[0]
    k_refs = refs[1:1 + n_kparts]
    v_refs = refs[1 + n_kparts:1 + 2 * n_kparts]
    (tab_ref, sink_ref, bkt_ref, valid_ref, ga_ref, x_ref, gated_ref, gs_ref, wo_hbm,
     o_ref, r_ref, kk_scr, vv_scr, ao_scr, bias_ref, smax_ref, sl_ref, wo_ref,
     wo_sem) = refs[1 + 2 * n_kparts:]
    rows = PAIRS * bq

    @pl.when((pl.program_id(0) == 0) & (pl.program_id(1) == 0))
    def _():
        load_wo = pltpu.make_async_copy(wo_hbm, wo_ref, wo_sem)
        load_wo.start()
        _build_tables(bq, nsel, tab_ref, sink_ref, bkt_ref, valid_ref, bias_ref, smax_ref, sl_ref)
        load_wo.wait()

    def split_heads(parts, scr):
        x = jnp.concatenate(
            [parts[p][r0:r0 + n, :] if p >= 0 else jnp.zeros((n, KV_WIDTH), F32)
             for p, r0, n in k_layout], axis=0)
        lo = lax.broadcasted_iota(jnp.int32, x.shape, 1) < HEAD_DIM
        xr = pltpu.roll(x, HEAD_DIM, 1)
        scr[0, 0] = jnp.where(lo, x, 0.0).astype(BF16)
        scr[0, 1] = jnp.where(lo, 0.0, xr).astype(BF16)
        scr[1, 0] = jnp.where(lo, xr, 0.0).astype(BF16)
        scr[1, 1] = jnp.where(lo, 0.0, x).astype(BF16)

    split_heads(k_refs, kk_scr)
    split_heads(v_refs, vv_scr)
    r_i = lax.broadcasted_iota(jnp.int32, (2 * BAND, LANES), 0)
    l_i = lax.broadcasted_iota(jnp.int32, (2 * BAND, LANES), 1)
    ones_blk = jnp.where((r_i < BAND) == (l_i < HEAD_DIM), 1.0, 0.0).astype(BF16)
    lane_lo = lax.broadcasted_iota(jnp.int32, (rows, LANES), 1) < HEAD_DIM
    first_block = pl.program_id(1) == 0

    for u in range(n_units):
        q0, k0 = u * bq, u * k_step
        sel = jnp.where(first_block, 0, 1) if (nsel > 1 and u == 0) else nsel - 1
        for g in range(N_KV_HEADS):
            cols = [slice((PAIRS * g + p) * LANES, (PAIRS * g + p + 1) * LANES)
                    for p in range(PAIRS)]
            qs = jnp.concatenate([q_ref[q0:q0 + bq, c] for c in cols], axis=0)
            k2 = jnp.concatenate([kk_scr[g, 0, k0:k0 + BAND, :],
                                  kk_scr[g, 1, k0:k0 + BAND, :]], axis=0)
            s = lax.dot_general(qs, k2, (((1,), (1,)), ((), ())),
                                preferred_element_type=F32) + bias_ref[sel, g]
            sa, sb = s[:, :BAND], s[:, BAND:]
            ma = jnp.max(jnp.maximum(jnp.maximum(sa[:, :LANES], sa[:, LANES:]), smax_ref[g, 0]),
                         axis=-1, keepdims=True)
            mb = jnp.max(jnp.maximum(jnp.maximum(sb[:, :LANES], sb[:, LANES:]), smax_ref[g, 1]),
                         axis=-1, keepdims=True)
            p = jnp.concatenate([jnp.exp(sa - ma), jnp.exp(sb - mb)], axis=1).astype(BF16)
            v2 = jnp.concatenate([vv_scr[g, 0, k0:k0 + BAND, :],
                                  vv_scr[g, 1, k0:k0 + BAND, :]], axis=0)
            ol = jnp.dot(p, jnp.concatenate([v2, ones_blk], axis=1), preferred_element_type=F32)
            denom = ol[:, LANES:] + jnp.exp(sl_ref[g] - jnp.where(lane_lo, ma, mb))
            out = ol[:, :LANES] / denom
            for p_i, c in enumerate(cols):
                ao_scr[q0:q0 + bq, c] = out[p_i * bq:(p_i + 1) * bq]

    mix = jnp.concatenate([_rms_rows(ao_scr[...], ga_ref[...]).astype(BF16),
                           _rms_rows(gated_ref[...], gs_ref[...]).astype(BF16)], axis=1)
    x1 = x_ref[...] + jnp.dot(mix, wo_ref[...], preferred_element_type=F32)
    o_ref[...] = x1
    rinv = lax.rsqrt(jnp.mean(x1 * x1, axis=-1, keepdims=True) + EPS)
    r_ref[...] = jnp.broadcast_to(rinv * rinv, r_ref.shape)


def _attn(q, k_parts, v_parts, k_maps, k_layout, table, sinks, bucket, valid, ga, x, gated, gs,
          wo, *, bq, n_units, k_step, nsel):
    nb, s, _ = q.shape
    tq = n_units * bq
    k_rows = [rows for _, rows in k_parts]
    kr = sum(n for _, _, n in k_layout)
    cur = lambda b, i: (b, i, 0)
    row = lambda w: pl.BlockSpec((None, tq, w), cur)
    full = lambda a: pl.BlockSpec(a.shape, lambda b, i: (0,) * a.ndim,
                                  pipeline_mode=pl.Buffered(1))
    kv_specs = [pl.BlockSpec((None, r, KV_WIDTH), m) for r, m in zip(k_rows, k_maps)]
    return pl.pallas_call(
        functools.partial(_attn_kernel, bq, n_units, k_step, nsel, len(k_parts), tuple(k_layout)),
        grid=(nb, s // tq),
        in_specs=([row(ATTN_WIDTH)] + kv_specs + kv_specs
                  + [full(table), pl.BlockSpec(memory_space=pltpu.SMEM), full(bucket), full(valid),
                     full(ga), row(D_MODEL), row(SGU_WIDTH), full(gs),
                     pl.BlockSpec(memory_space=pl.ANY)]),
        out_specs=[row(D_MODEL), row(LANES)],
        out_shape=[jax.ShapeDtypeStruct((nb, s, D_MODEL), F32),
                   jax.ShapeDtypeStruct((nb, s, LANES), F32)],
        scratch_shapes=[pltpu.VMEM((N_KV_HEADS, 2, kr, KV_WIDTH), BF16),
                        pltpu.VMEM((N_KV_HEADS, 2, kr, KV_WIDTH), BF16),
                        pltpu.VMEM((tq, ATTN_WIDTH), F32),
                        pltpu.VMEM((nsel, N_KV_HEADS, PAIRS * bq, 2 * BAND), F32),
                        pltpu.VMEM((N_KV_HEADS, 2, PAIRS * bq, LANES), F32),
                        pltpu.VMEM((N_KV_HEADS, PAIRS * bq, LANES), F32),
                        pltpu.VMEM(wo.shape, BF16),
                        pltpu.SemaphoreType.DMA],
        compiler_params=pltpu.CompilerParams(dimension_semantics=("arbitrary", "arbitrary"),
                                             vmem_limit_bytes=VMEM_LIMIT),
        name="attn_q%d" % bq,
    )(q, *[a for a, _ in k_parts], *[a for a, _ in v_parts], table, sinks, bucket, valid, ga, x,
      gated, gs, wo)


def _ffn_kernel(x_ref, r_ref, g_ref, wup_ref, wdn_ref, y_ref):
    tf = wup_ref.shape[1]
    xg = (x_ref[...] * g_ref[...]).astype(BF16)
    r2 = jnp.tile(r_ref[...], (1, FFN_SUB // LANES))
    for s in range(tf // FFN_SUB):
        cs = slice(s * FFN_SUB, (s + 1) * FFN_SUB)
        z = jnp.dot(xg, wup_ref[:, cs], preferred_element_type=F32)
        a = (jnp.square(jnp.maximum(z, 0.0)) * r2).astype(BF16)
        if s == 0:
            base = jnp.where(pl.program_id(1) == 0, x_ref[...], y_ref[...])
        else:
            base = y_ref[...]
        y_ref[...] = base + jnp.dot(a, wdn_ref[cs, :], preferred_element_type=F32)


def _ffn(x2d, r2d, g, wup_t, wdn, *, tm):
    t = x2d.shape[0]
    n_f, _, tf = wup_t.shape
    return pl.pallas_call(
        _ffn_kernel,
        grid=(t // tm, n_f),
        in_specs=[pl.BlockSpec((tm, D_MODEL), lambda i, j: (i, 0)),
                  pl.BlockSpec((tm, LANES), lambda i, j: (i, 0)),
                  pl.BlockSpec((1, D_MODEL), lambda i, j: (0, 0)),
                  pl.BlockSpec((None, D_MODEL, tf), lambda i, j: (j, 0, 0)),
                  pl.BlockSpec((tf, D_MODEL), lambda i, j: (j, 0))],
        out_specs=pl.BlockSpec((tm, D_MODEL), lambda i, j: (i, 0)),
        out_shape=jax.ShapeDtypeStruct((t, D_MODEL), F32),
        compiler_params=pltpu.CompilerParams(dimension_semantics=("parallel", "arbitrary"),
                                             vmem_limit_bytes=FFN_VMEM_LIMIT),
        name="ffn",
    )(x2d, r2d, g, wup_t, wdn)


def _band_tables(bq, nsel):
    qi = jnp.arange(bq)[:, None]
    kj = jnp.arange(BAND)[None, :] - WINDOW
    qc, kc = qi // CHUNK, jnp.floor_divide(kj, CHUNK)
    in_band = (kc <= qc) & (kc >= qc - WINDOW // CHUNK)
    valid = [in_band & (kj >= 0)] if nsel == 2 else []
    valid.append(in_band)
    return _t5_bucket(qi - kj), jnp.stack(valid).astype(jnp.int32)


def kernel(x_prompt, x_sample, cache_attn_k, cache_attn_v, rel_bias_table, ln_mix_g, w_in,
           q_norm_g, k_norm_g, attn_sinks, sgu_norm_g, sgu_w, sgu_b, out_norm_attn_g,
           out_norm_sgu_g, w_out, ln_ffn_g, w_ffn_up, w_ffn_down):
    bp, sp, _ = x_prompt.shape
    bs, ss, _ = x_sample.shape
    depth = w_in.shape[0]
    assert depth == 1 and ss == CHUNK and cache_attn_k.shape[2] == WINDOW
    l = 0
    xp = x_prompt.reshape(bp * sp, D_MODEL)
    xs = x_sample.reshape(bs * ss, D_MODEL)

    bq, units = 2 * CHUNK, ATTN_UNITS
    sinks = attn_sinks[l].reshape(N_HEADS)
    table_t = jnp.pad(rel_bias_table.T, ((0, 0), (0, LANES - N_BUCKETS)))
    bucket, valid = _band_tables(bq, 2)
    bsb = jnp.broadcast_to(sgu_b[l][:, :, None], (SGU_GROUPS, SGU_CHUNK, LANES))
    lng = ln_mix_g[l].reshape(1, D_MODEL)
    qg2 = q_norm_g[l].reshape(1, HEAD_DIM)
    kg2 = k_norm_g[l].reshape(1, HEAD_DIM)
    sg = sgu_norm_g[l].reshape(1, SGU_WIDTH)
    gs = out_norm_sgu_g[l].reshape(1, SGU_WIDTH)
    ga = out_norm_attn_g[l].reshape(1, ATTN_WIDTH)
    lnf = ln_ffn_g[l].reshape(1, D_MODEL)

    ws = sgu_w[l]

    q, k, v, gated, wup, wdn, wo, win = _proj(xp, lng, w_in[l], qg2, kg2, sg, ws, bsb,
                                         w_ffn_up[l], w_ffn_down[l], w_out[l],
                                         tm=PROJ_TM, chunk=SGU_CHUNK, emit_sv=False, tf=FFN_TF)
    k3 = k.reshape(bp, sp, KV_WIDTH)
    v3 = v.reshape(bp, sp, KV_WIDTH)
    prev = lambda b, i: (b, jnp.maximum(i * units - 1, 0), 0)
    cur = lambda b, i: (b, i, 0)
    x1p, r2p = _attn(q.reshape(bp, sp, ATTN_WIDTH),
                     [(k3, WINDOW), (k3, units * bq)], [(v3, WINDOW), (v3, units * bq)],
                     [prev, cur], [(0, 0, WINDOW), (1, 0, units * bq)],
                     table_t, sinks, bucket, valid, ga,
                     x_prompt, gated.reshape(bp, sp, SGU_WIDTH), gs, wo,
                     bq=bq, n_units=units, k_step=bq, nsel=2)
    yp = _ffn(x1p.reshape(bp * sp, D_MODEL), r2p.reshape(bp * sp, LANES), lnf, wup, wdn, tm=FFN_TM)

    qs, ks, vs, gated_s, svs = _proj(xs, lng, win, qg2, kg2, sg, ws, bsb,
                                    tm=PROJ_TM, chunk=ss, emit_sv=True)
    seqs = SAMPLE_SEQS
    grp = lambda a, rows: a.reshape(bs // seqs, seqs * rows, a.shape[-1])
    layout = []
    for i in range(seqs):
        layout += [(0, i * WINDOW, WINDOW), (1, i * ss, ss), (-1, 0, BAND - WINDOW - ss)]
    x1s, r2s = _attn(grp(qs, ss),
                     [(grp(cache_attn_k[l].reshape(bs, WINDOW, KV_WIDTH), WINDOW), seqs * WINDOW),
                      (grp(ks, ss), seqs * ss)],
                     [(grp(cache_attn_v[l].reshape(bs, WINDOW, KV_WIDTH), WINDOW), seqs * WINDOW),
                      (grp(vs, ss), seqs * ss)],
                     [cur, cur], layout, table_t, sinks, bucket, valid,
                     ga, grp(xs, ss), grp(gated_s, ss), gs, wo,
                     bq=ss, n_units=seqs, k_step=BAND, nsel=1)
    ys = _ffn(x1s.reshape(bs * ss, D_MODEL), r2s.reshape(bs * ss, LANES), lnf, wup, wdn, tm=FFN_TM)

    keep = min(WINDOW, sp)
    kv_shape = (N_KV_HEADS, HEAD_DIM)
    return (yp.reshape(bp, sp, D_MODEL),
            ys.reshape(bs, ss, D_MODEL),
            k3[:, -keep:].reshape(1, bp, keep, *kv_shape),
            v3[:, -keep:].reshape(1, bp, keep, *kv_shape),
            ks.reshape(1, bs, ss, *kv_shape),
            vs.reshape(1, bs, ss, *kv_shape),
            svs.reshape(1, bs, ss, SGU_WIDTH))
```

```python
import functools
import math

import jax
import jax.numpy as jnp
from jax import lax
from jax.experimental import pallas as pl
from jax.experimental.pallas import tpu as pltpu

D_MODEL = 2048
CHUNK = 64
ATTN_WIDTH = 1024
SGU_WIDTH = 1024
HEAD_DIM = 64
N_HEADS = ATTN_WIDTH // HEAD_DIM
N_KV_HEADS = 2
GQA_GROUP = N_HEADS // N_KV_HEADS
WINDOW = 128
N_BUCKETS = 32
MAX_DISTANCE = 128
SGU_CHUNK = 128
SGU_GROUPS = 8
SGU_GROUP_CH = SGU_WIDTH // SGU_GROUPS
D_FF = 4 * D_MODEL
KV_WIDTH = N_KV_HEADS * HEAD_DIM
IN_WIDTH = ATTN_WIDTH + 2 * KV_WIDTH + 2 * SGU_WIDTH
EPS = 1e-6
NEG_INF = -1e30

LANES = 128
VMEM_LIMIT = 56 * 1024 * 1024
FFN_VMEM_LIMIT = 62 * 1024 * 1024
PROJ_VMEM_LIMIT = 60 * 1024 * 1024
WIN_STAGE_ROWS = 256
PROJ_TM = 512
ATTN_UNITS = 4
SAMPLE_SEQS = 4
FFN_TM = 1024
FFN_TF = 1024
FFN_SUB = 1024

BF16 = jnp.bfloat16
F32 = jnp.float32

_KV0 = ATTN_WIDTH
_U0 = ATTN_WIDTH + 2 * KV_WIDTH
_V0 = _U0 + SGU_WIDTH


def _rms_rows(x, gain):
    ms = jnp.mean(x * x, axis=-1, keepdims=True)
    return x * lax.rsqrt(ms + EPS) * gain


def _head_pair_rms(blk, gain2, lane_lo):
    sq = blk * blk
    lo = jnp.sum(jnp.where(lane_lo, sq, 0.0), axis=-1, keepdims=True)
    hi = jnp.sum(jnp.where(lane_lo, 0.0, sq), axis=-1, keepdims=True)
    inv = 1.0 / HEAD_DIM
    r = jnp.where(lane_lo, lax.rsqrt(lo * inv + EPS), lax.rsqrt(hi * inv + EPS))
    return blk * r * gain2


def _cast_weights(wup_ref, wdn_ref, wo_ref, wupb_ref, wdnb_ref, wob_ref):
    tf = wupb_ref.shape[2]
    for j in range(wupb_ref.shape[0]):
        wupb_ref[j] = wup_ref[:, j * tf:(j + 1) * tf].astype(BF16)
    wdnb_ref[...] = wdn_ref[...].astype(BF16)
    wob_ref[...] = wo_ref[...].astype(BF16)


def _load_win(winf_hbm, winb_hbm, win_scr, stage, sems):
    step, last = pl.program_id(0), pl.num_programs(0) - 1
    rows = stage.shape[1]
    n_chunks = win_scr.shape[0] // rows
    chunk_in = lambda c: pltpu.make_async_copy(
        winf_hbm.at[pl.ds(c * rows, rows), :], stage.at[c % 2], sems.at[c % 2])
    write_out = pltpu.make_async_copy(win_scr, winb_hbm, sems.at[2])

    @pl.when(step == 0)
    def _():
        chunk_in(0).start()
        for c in range(n_chunks):
            if c + 1 < n_chunks:
                chunk_in(c + 1).start()
            chunk_in(c).wait()
            win_scr[c * rows:(c + 1) * rows, :] = stage[c % 2].astype(BF16)
        write_out.start()

    @pl.when(step == last)
    def _():
        write_out.wait()


def _build_gmlp_bias(sgub_ref, bsb_scr):
    @pl.when(pl.program_id(0) == 0)
    def _():
        n = bsb_scr.shape[1]
        eye = (lax.broadcasted_iota(jnp.int32, (n, n), 0)
               == lax.broadcasted_iota(jnp.int32, (n, n), 1))
        for g in range(bsb_scr.shape[0]):
            col = jnp.sum(jnp.where(eye, sgub_ref[g:g + 1, :], 0.0), axis=1, keepdims=True)
            bsb_scr[g] = jnp.broadcast_to(col, bsb_scr.shape[1:])


def _proj_kernel(chunk, emit_sv, cast_w, x_ref, lng_ref, win_ref, qg_ref, kg_ref, sg_ref,
                 ws_ref, sgub_ref, *rest):
    bs_ref, rest = rest[-1], rest[:-1]
    _build_gmlp_bias(sgub_ref, bs_ref)
    if cast_w:
        winb_hbm, win_scr, stage, sems = rest[-4:]
        _cast_weights(*rest[:3], *rest[-7:-4])
        rest = rest[3:-7]
        _load_win(win_ref, winb_hbm, win_scr, stage, sems)
        win_ref = win_scr
    q_ref, k_ref, v_ref, gated_ref = rest[:4]
    tm = x_ref.shape[0]
    h = _rms_rows(x_ref[...], lng_ref[...]).astype(BF16)
    lane_lo = lax.broadcasted_iota(jnp.int32, (tm, LANES), 1) < HEAD_DIM
    qg2 = jnp.concatenate([qg_ref[...], qg_ref[...]], axis=1)
    kg2 = jnp.concatenate([kg_ref[...], kg_ref[...]], axis=1)

    av = jax.nn.gelu(jnp.dot(h, win_ref[:, _V0:IN_WIDTH], preferred_element_type=F32))
    sv = _rms_rows(av, sg_ref[...])
    if emit_sv:
        rest[4][...] = sv
    svb = sv.astype(BF16)
    u = jax.nn.gelu(jnp.dot(h, win_ref[:, _U0:_V0], preferred_element_type=F32))

    zq = jnp.dot(h, win_ref[:, 0:ATTN_WIDTH], preferred_element_type=F32)
    for c in range(ATTN_WIDTH // LANES):
        sl = slice(c * LANES, (c + 1) * LANES)
        q_ref[:, sl] = (_head_pair_rms(zq[:, sl], qg2, lane_lo)
                        * (HEAD_DIM ** -0.5)).astype(BF16)

    zkv = jnp.dot(h, win_ref[:, _KV0:_U0], preferred_element_type=F32)
    k_ref[...] = _head_pair_rms(zkv[:, 0:KV_WIDTH], kg2, lane_lo)
    v_ref[...] = zkv[:, KV_WIDTH:2 * KV_WIDTH]

    row_c = lax.broadcasted_iota(jnp.int32, (chunk, chunk), 0) // CHUNK
    col_c = lax.broadcasted_iota(jnp.int32, (chunk, chunk), 1) // CHUNK
    n_chunks = tm // chunk
    for g in range(SGU_GROUPS):
        cs = slice(g * SGU_GROUP_CH, (g + 1) * SGU_GROUP_CH)
        w = jnp.where(col_c <= row_c, ws_ref[g, 0:chunk, 0:chunk], 0.0).astype(BF16)
        rhs = jnp.concatenate(
            [svb[n * chunk:(n + 1) * chunk, cs] for n in range(n_chunks)], axis=1)
        sp = jnp.dot(w, rhs, preferred_element_type=F32)
        for n in range(n_chunks):
            rs = slice(n * chunk, (n + 1) * chunk)
            gated_ref[rs, cs] = u[rs, cs] * (sp[:, n * SGU_GROUP_CH:(n + 1) * SGU_GROUP_CH]
                                             + bs_ref[g, 0:chunk, :])


def _proj(x2d, lng, win, qg2, kg2, sg, ws, sgu_b, wup=None, wdn=None, wo=None, *, tm, chunk,
          emit_sv, tf=None):
    t = x2d.shape[0]
    const = lambda shape: pl.BlockSpec(shape, lambda i: (0,) * len(shape),
                                       pipeline_mode=pl.Buffered(1))
    row = lambda w: pl.BlockSpec((tm, w), lambda i: (i, 0))
    out_shape = [jax.ShapeDtypeStruct((t, ATTN_WIDTH), BF16),
                 jax.ShapeDtypeStruct((t, KV_WIDTH), F32),
                 jax.ShapeDtypeStruct((t, KV_WIDTH), F32),
                 jax.ShapeDtypeStruct((t, SGU_WIDTH), F32)]
    out_specs = [row(ATTN_WIDTH), row(KV_WIDTH), row(KV_WIDTH), row(SGU_WIDTH)]
    if emit_sv:
        out_shape.append(jax.ShapeDtypeStruct((t, SGU_WIDTH), F32))
        out_specs.append(row(SGU_WIDTH))
    in_specs = [row(D_MODEL), const((1, D_MODEL)), const((D_MODEL, IN_WIDTH)),
                const((1, HEAD_DIM)), const((1, HEAD_DIM)), const((1, SGU_WIDTH)),
                const(ws.shape), const(sgu_b.shape)]
    args = [x2d, lng, win, qg2, kg2, sg, ws, sgu_b]
    scratch = []
    if wup is not None:
        in_specs[2] = pl.BlockSpec(memory_space=pl.ANY)
        scratch = [pltpu.VMEM(win.shape, BF16),
                   pltpu.VMEM((2, WIN_STAGE_ROWS, win.shape[1]), F32),
                   pltpu.SemaphoreType.DMA((3,))]
        n_steps = t // tm
        up_rows, dn_rows, wo_rows = D_MODEL // n_steps, D_FF // n_steps, wo.shape[0] // n_steps
        in_specs += [pl.BlockSpec((up_rows, D_FF), lambda i: (i, 0)),
                     pl.BlockSpec((dn_rows, D_MODEL), lambda i: (i, 0)),
                     pl.BlockSpec((wo_rows, D_MODEL), lambda i: (i, 0))]
        out_specs += [pl.BlockSpec((D_FF // tf, up_rows, tf), lambda i: (0, i, 0)),
                      pl.BlockSpec((dn_rows, D_MODEL), lambda i: (i, 0)),
                      pl.BlockSpec((wo_rows, D_MODEL), lambda i: (i, 0)),
                      pl.BlockSpec(memory_space=pl.ANY)]
        out_shape += [jax.ShapeDtypeStruct((D_FF // tf, D_MODEL, tf), BF16),
                      jax.ShapeDtypeStruct((D_FF, D_MODEL), BF16),
                      jax.ShapeDtypeStruct(wo.shape, BF16),
                      jax.ShapeDtypeStruct(win.shape, BF16)]
        args += [wup, wdn, wo]
    return pl.pallas_call(
        functools.partial(_proj_kernel, chunk, emit_sv, wup is not None),
        grid=(t // tm,),
        in_specs=in_specs,
        out_specs=out_specs,
        out_shape=out_shape,
        scratch_shapes=scratch + [pltpu.VMEM((SGU_GROUPS, SGU_CHUNK, LANES), F32)],
        compiler_params=pltpu.CompilerParams(dimension_semantics=("arbitrary",),
                                             vmem_limit_bytes=PROJ_VMEM_LIMIT),
        name="proj_c%d" % chunk,
    )(*args)


BAND = 2 * LANES
PAIRS = GQA_GROUP // 2


def _build_tables(bq, nsel, tab_ref, sink_ref, bkt_ref, valid_ref, bias_scr, smax_scr, sl_scr):
    bkt = bkt_ref[0:bq, :]
    lane_lo = lax.broadcasted_iota(jnp.int32, (bq, LANES), 1) < HEAD_DIM
    for h0 in range(0, N_HEADS, 2):
        g, pr = h0 // GQA_GROUP, (h0 % GQA_GROUP) // 2
        rs = slice(pr * bq, (pr + 1) * bq)
        sl_scr[g, rs, :] = jnp.where(lane_lo, sink_ref[h0], sink_ref[h0 + 1])
        for half in range(2):
            h = h0 + half
            smax_scr[g, half, rs, :] = jnp.full((bq, LANES), sink_ref[h], F32)
            row = jnp.broadcast_to(tab_ref[h:h + 1, :], (bq, LANES))
            acc = jnp.concatenate(
                [jnp.take_along_axis(row, bkt[:, c * LANES:(c + 1) * LANES], axis=1)
                 for c in range(BAND // LANES)], axis=1)
            for s in range(nsel):
                ok = valid_ref[valid_ref.shape[0] - nsel + s, 0:bq, :] != 0
                bias_scr[s, g, rs, half * BAND:(half + 1) * BAND] = jnp.where(ok, acc, NEG_INF)


def _t5_bucket(n):
    half = N_BUCKETS // 2
    max_exact = half // 2
    offset = jnp.where(n < 0, half, 0)
    a = jnp.abs(n)
    af = jnp.maximum(a, 1).astype(F32)
    large = max_exact + (jnp.log(af / max_exact) / math.log(MAX_DISTANCE / max_exact)
                         * (half - max_exact)).astype(jnp.int32)
    large = jnp.minimum(large, half - 1)
    return offset + jnp.where(a < max_exact, a, large)


def _attn_kernel(bq, n_units, k_step, nsel, n_kparts, k_layout, *refs):
    q_ref = refs[0]
    k_refs = refs[1:1 + n_kparts]
    v_refs = refs[1 + n_kparts:1 + 2 * n_kparts]
    (tab_ref, sink_ref, bkt_ref, valid_ref, ga_ref, x_ref, gated_ref, gs_ref, wo_ref,
     o_ref, r_ref, kk_scr, vv_scr, ao_scr, bias_ref, smax_ref, sl_ref) = refs[1 + 2 * n_kparts:]
    rows = PAIRS * bq

    @pl.when((pl.program_id(0) == 0) & (pl.program_id(1) == 0))
    def _():
        _build_tables(bq, nsel, tab_ref, sink_ref, bkt_ref, valid_ref, bias_ref, smax_ref, sl_ref)

    def split_heads(parts, scr):
        x = jnp.concatenate(
            [parts[p][r0:r0 + n, :] if p >= 0 else jnp.zeros((n, KV_WIDTH), F32)
             for p, r0, n in k_layout], axis=0)
        lo = lax.broadcasted_iota(jnp.int32, x.shape, 1) < HEAD_DIM
        xr = pltpu.roll(x, HEAD_DIM, 1)
        scr[0, 0] = jnp.where(lo, x, 0.0).astype(BF16)
        scr[0, 1] = jnp.where(lo, 0.0, xr).astype(BF16)
        scr[1, 0] = jnp.where(lo, xr, 0.0).astype(BF16)
        scr[1, 1] = jnp.where(lo, 0.0, x).astype(BF16)

    split_heads(k_refs, kk_scr)
    split_heads(v_refs, vv_scr)
    r_i = lax.broadcasted_iota(jnp.int32, (2 * BAND, LANES), 0)
    l_i = lax.broadcasted_iota(jnp.int32, (2 * BAND, LANES), 1)
    ones_blk = jnp.where((r_i < BAND) == (l_i < HEAD_DIM), 1.0, 0.0).astype(BF16)
    lane_lo = lax.broadcasted_iota(jnp.int32, (rows, LANES), 1) < HEAD_DIM
    first_block = pl.program_id(1) == 0

    for u in range(n_units):
        q0, k0 = u * bq, u * k_step
        sel = jnp.where(first_block, 0, 1) if (nsel > 1 and u == 0) else nsel - 1
        for g in range(N_KV_HEADS):
            cols = [slice((PAIRS * g + p) * LANES, (PAIRS * g + p + 1) * LANES)
                    for p in range(PAIRS)]
            qs = jnp.concatenate([q_ref[q0:q0 + bq, c] for c in cols], axis=0)
            k2 = jnp.concatenate([kk_scr[g, 0, k0:k0 + BAND, :],
                                  kk_scr[g, 1, k0:k0 + BAND, :]], axis=0)
            s = lax.dot_general(qs, k2, (((1,), (1,)), ((), ())),
                                preferred_element_type=F32) + bias_ref[sel, g]
            sa, sb = s[:, :BAND], s[:, BAND:]
            ma = jnp.max(jnp.maximum(jnp.maximum(sa[:, :LANES], sa[:, LANES:]), smax_ref[g, 0]),
                         axis=-1, keepdims=True)
            mb = jnp.max(jnp.maximum(jnp.maximum(sb[:, :LANES], sb[:, LANES:]), smax_ref[g, 1]),
                         axis=-1, keepdims=True)
            p = jnp.concatenate([jnp.exp(sa - ma), jnp.exp(sb - mb)], axis=1).astype(BF16)
            v2 = jnp.concatenate([vv_scr[g, 0, k0:k0 + BAND, :],
                                  vv_scr[g, 1, k0:k0 + BAND, :]], axis=0)
            ol = jnp.dot(p, jnp.concatenate([v2, ones_blk], axis=1), preferred_element_type=F32)
            denom = ol[:, LANES:] + jnp.exp(sl_ref[g] - jnp.where(lane_lo, ma, mb))
            out = ol[:, :LANES] / denom
            for p_i, c in enumerate(cols):
                ao_scr[q0:q0 + bq, c] = out[p_i * bq:(p_i + 1) * bq]

    mix = jnp.concatenate([_rms_rows(ao_scr[...], ga_ref[...]).astype(BF16),
                           _rms_rows(gated_ref[...], gs_ref[...]).astype(BF16)], axis=1)
    x1 = x_ref[...] + jnp.dot(mix, wo_ref[...], preferred_element_type=F32)
    o_ref[...] = x1
    rinv = lax.rsqrt(jnp.mean(x1 * x1, axis=-1, keepdims=True) + EPS)
    r_ref[...] = jnp.broadcast_to(rinv * rinv, r_ref.shape)


def _attn(q, k_parts, v_parts, k_maps, k_layout, table, sinks, bucket, valid, ga, x, gated, gs,
          wo, *, bq, n_units, k_step, nsel):
    nb, s, _ = q.shape
    tq = n_units * bq
    k_rows = [rows for _, rows in k_parts]
    kr = sum(n for _, _, n in k_layout)
    cur = lambda b, i: (b, i, 0)
    row = lambda w: pl.BlockSpec((None, tq, w), cur)
    full = lambda a: pl.BlockSpec(a.shape, lambda b, i: (0,) * a.ndim,
                                  pipeline_mode=pl.Buffered(1))
    kv_specs = [pl.BlockSpec((None, r, KV_WIDTH), m) for r, m in zip(k_rows, k_maps)]
    return pl.pallas_call(
        functools.partial(_attn_kernel, bq, n_units, k_step, nsel, len(k_parts), tuple(k_layout)),
        grid=(nb, s // tq),
        in_specs=([row(ATTN_WIDTH)] + kv_specs + kv_specs
                  + [full(table), pl.BlockSpec(memory_space=pltpu.SMEM), full(bucket), full(valid),
                     full(ga), row(D_MODEL), row(SGU_WIDTH), full(gs), full(wo)]),
        out_specs=[row(D_MODEL), row(LANES)],
        out_shape=[jax.ShapeDtypeStruct((nb, s, D_MODEL), F32),
                   jax.ShapeDtypeStruct((nb, s, LANES), F32)],
        scratch_shapes=[pltpu.VMEM((N_KV_HEADS, 2, kr, KV_WIDTH), BF16),
                        pltpu.VMEM((N_KV_HEADS, 2, kr, KV_WIDTH), BF16),
                        pltpu.VMEM((tq, ATTN_WIDTH), F32),
                        pltpu.VMEM((nsel, N_KV_HEADS, PAIRS * bq, 2 * BAND), F32),
                        pltpu.VMEM((N_KV_HEADS, 2, PAIRS * bq, LANES), F32),
                        pltpu.VMEM((N_KV_HEADS, PAIRS * bq, LANES), F32)],
        compiler_params=pltpu.CompilerParams(dimension_semantics=("arbitrary", "arbitrary"),
                                             vmem_limit_bytes=VMEM_LIMIT),
        name="attn_q%d" % bq,
    )(q, *[a for a, _ in k_parts], *[a for a, _ in v_parts], table, sinks, bucket, valid, ga, x,
      gated, gs, wo)


def _ffn_kernel(x_ref, r_ref, g_ref, wup_ref, wdn_ref, y_ref):
    tf = wup_ref.shape[1]
    xg = (x_ref[...] * g_ref[...]).astype(BF16)
    r2 = jnp.tile(r_ref[...], (1, FFN_SUB // LANES))
    for s in range(tf // FFN_SUB):
        cs = slice(s * FFN_SUB, (s + 1) * FFN_SUB)
        z = jnp.dot(xg, wup_ref[:, cs], preferred_element_type=F32)
        a = (jnp.square(jnp.maximum(z, 0.0)) * r2).astype(BF16)
        if s == 0:
            base = jnp.where(pl.program_id(1) == 0, x_ref[...], y_ref[...])
        else:
            base = y_ref[...]
        y_ref[...] = base + jnp.dot(a, wdn_ref[cs, :], preferred_element_type=F32)


def _ffn(x2d, r2d, g, wup_t, wdn, *, tm):
    t = x2d.shape[0]
    n_f, _, tf = wup_t.shape
    return pl.pallas_call(
        _ffn_kernel,
        grid=(t // tm, n_f),
        in_specs=[pl.BlockSpec((tm, D_MODEL), lambda i, j: (i, 0)),
                  pl.BlockSpec((tm, LANES), lambda i, j: (i, 0)),
                  pl.BlockSpec((1, D_MODEL), lambda i, j: (0, 0)),
                  pl.BlockSpec((None, D_MODEL, tf), lambda i, j: (j, 0, 0)),
                  pl.BlockSpec((tf, D_MODEL), lambda i, j: (j, 0))],
        out_specs=pl.BlockSpec((tm, D_MODEL), lambda i, j: (i, 0)),
        out_shape=jax.ShapeDtypeStruct((t, D_MODEL), F32),
        compiler_params=pltpu.CompilerParams(dimension_semantics=("parallel", "arbitrary"),
                                             vmem_limit_bytes=FFN_VMEM_LIMIT),
        name="ffn",
    )(x2d, r2d, g, wup_t, wdn)


def _band_tables(bq, nsel):
    qi = jnp.arange(bq)[:, None]
    kj = jnp.arange(BAND)[None, :] - WINDOW
    qc, kc = qi // CHUNK, jnp.floor_divide(kj, CHUNK)
    in_band = (kc <= qc) & (kc >= qc - WINDOW // CHUNK)
    valid = [in_band & (kj >= 0)] if nsel == 2 else []
    valid.append(in_band)
    return _t5_bucket(qi - kj), jnp.stack(valid).astype(jnp.int32)


def kernel(x_prompt, x_sample, cache_attn_k, cache_attn_v, rel_bias_table, ln_mix_g, w_in,
           q_norm_g, k_norm_g, attn_sinks, sgu_norm_g, sgu_w, sgu_b, out_norm_attn_g,
           out_norm_sgu_g, w_out, ln_ffn_g, w_ffn_up, w_ffn_down):
    bp, sp, _ = x_prompt.shape
    bs, ss, _ = x_sample.shape
    depth = w_in.shape[0]
    assert depth == 1 and ss == CHUNK and cache_attn_k.shape[2] == WINDOW
    l = 0
    xp = x_prompt.reshape(bp * sp, D_MODEL)
    xs = x_sample.reshape(bs * ss, D_MODEL)

    bq, units = 2 * CHUNK, ATTN_UNITS
    sinks = attn_sinks[l].reshape(N_HEADS)
    table_t = jnp.pad(rel_bias_table.T, ((0, 0), (0, LANES - N_BUCKETS)))
    bucket, valid = _band_tables(bq, 2)
    lng = ln_mix_g[l].reshape(1, D_MODEL)
    qg2 = q_norm_g[l].reshape(1, HEAD_DIM)
    kg2 = k_norm_g[l].reshape(1, HEAD_DIM)
    sg = sgu_norm_g[l].reshape(1, SGU_WIDTH)
    gs = out_norm_sgu_g[l].reshape(1, SGU_WIDTH)
    ga = out_norm_attn_g[l].reshape(1, ATTN_WIDTH)
    lnf = ln_ffn_g[l].reshape(1, D_MODEL)

    ws = sgu_w[l]

    q, k, v, gated, wup, wdn, wo, win = _proj(xp, lng, w_in[l], qg2, kg2, sg, ws, sgu_b[l],
                                         w_ffn_up[l], w_ffn_down[l], w_out[l],
                                         tm=PROJ_TM, chunk=SGU_CHUNK, emit_sv=False, tf=FFN_TF)
    k3 = k.reshape(bp, sp, KV_WIDTH)
    v3 = v.reshape(bp, sp, KV_WIDTH)
    prev = lambda b, i: (b, jnp.maximum(i * units - 1, 0), 0)
    cur = lambda b, i: (b, i, 0)
    x1p, r2p = _attn(q.reshape(bp, sp, ATTN_WIDTH),
                     [(k3, WINDOW), (k3, units * bq)], [(v3, WINDOW), (v3, units * bq)],
                     [prev, cur], [(0, 0, WINDOW), (1, 0, units * bq)],
                     table_t, sinks, bucket, valid, ga,
                     x_prompt, gated.reshape(bp, sp, SGU_WIDTH), gs, wo,
                     bq=bq, n_units=units, k_step=bq, nsel=2)
    yp = _ffn(x1p.reshape(bp * sp, D_MODEL), r2p.reshape(bp * sp, LANES), lnf, wup, wdn, tm=FFN_TM)

    qs, ks, vs, gated_s, svs = _proj(xs, lng, win, qg2, kg2, sg, ws, sgu_b[l],
                                    tm=PROJ_TM, chunk=ss, emit_sv=True)
    seqs = SAMPLE_SEQS
    grp = lambda a, rows: a.reshape(bs // seqs, seqs * rows, a.shape[-1])
    layout = []
    for i in range(seqs):
        layout += [(0, i * WINDOW, WINDOW), (1, i * ss, ss), (-1, 0, BAND - WINDOW - ss)]
    x1s, r2s = _attn(grp(qs, ss),
                     [(grp(cache_attn_k[l].reshape(bs, WINDOW, KV_WIDTH), WINDOW), seqs * WINDOW),
                      (grp(ks, ss), seqs * ss)],
                     [(grp(cache_attn_v[l].reshape(bs, WINDOW, KV_WIDTH), WINDOW), seqs * WINDOW),
                      (grp(vs, ss), seqs * ss)],
                     [cur, cur], layout, table_t, sinks, bucket, valid,
                     ga, grp(xs, ss), grp(gated_s, ss), gs, wo,
                     bq=ss, n_units=seqs, k_step=BAND, nsel=1)
    ys = _ffn(x1s.reshape(bs * ss, D_MODEL), r2s.reshape(bs * ss, LANES), lnf, wup, wdn, tm=FFN_TM)

    keep = min(WINDOW, sp)
    kv_shape = (N_KV_HEADS, HEAD_DIM)
    return (yp.reshape(bp, sp, D_MODEL),
            ys.reshape(bs, ss, D_MODEL),
            k3[:, -keep:].reshape(1, bp, keep, *kv_shape),
            v3[:, -keep:].reshape(1, bp, keep, *kv_shape),
            ks.reshape(1, bs, ss, *kv_shape),
            vs.reshape(1, bs, ss, *kv_shape),
            svs.reshape(1, bs, ss, SGU_WIDTH))
```

```python
import functools
import math

import jax
import jax.numpy as jnp
from jax import lax
from jax.experimental import pallas as pl
from jax.experimental.pallas import tpu as pltpu

D_MODEL = 2048
CHUNK = 64
ATTN_WIDTH = 1024
SGU_WIDTH = 1024
HEAD_DIM = 64
N_HEADS = ATTN_WIDTH // HEAD_DIM
N_KV_HEADS = 2
GQA_GROUP = N_HEADS // N_KV_HEADS
WINDOW = 128
N_BUCKETS = 32
MAX_DISTANCE = 128
SGU_CHUNK = 128
SGU_GROUPS = 8
SGU_GROUP_CH = SGU_WIDTH // SGU_GROUPS
D_FF = 4 * D_MODEL
KV_WIDTH = N_KV_HEADS * HEAD_DIM
IN_WIDTH = ATTN_WIDTH + 2 * KV_WIDTH + 2 * SGU_WIDTH
EPS = 1e-6
NEG_INF = -1e30

LANES = 128
VMEM_LIMIT = 56 * 1024 * 1024
FFN_VMEM_LIMIT = 62 * 1024 * 1024
PROJ_VMEM_LIMIT = 60 * 1024 * 1024
WIN_STAGE_ROWS = 256
PROJ_TM = 512
ATTN_UNITS = 4
SAMPLE_SEQS = 4
FFN_TM = 1024
FFN_TF = 1024
FFN_SUB = 1024

BF16 = jnp.bfloat16
F32 = jnp.float32

_KV0 = ATTN_WIDTH
_U0 = ATTN_WIDTH + 2 * KV_WIDTH
_V0 = _U0 + SGU_WIDTH


def _rms_rows(x, gain):
    ms = jnp.mean(x * x, axis=-1, keepdims=True)
    return x * lax.rsqrt(ms + EPS) * gain


def _head_pair_rms(blk, gain2, lane_lo):
    sq = blk * blk
    lo = jnp.sum(jnp.where(lane_lo, sq, 0.0), axis=-1, keepdims=True)
    hi = jnp.sum(jnp.where(lane_lo, 0.0, sq), axis=-1, keepdims=True)
    inv = 1.0 / HEAD_DIM
    r = jnp.where(lane_lo, lax.rsqrt(lo * inv + EPS), lax.rsqrt(hi * inv + EPS))
    return blk * r * gain2


def _cast_weights(wup_ref, wdn_ref, wo_ref, wupb_ref, wdnb_ref, wob_ref):
    tf = wupb_ref.shape[2]
    for j in range(wupb_ref.shape[0]):
        wupb_ref[j] = wup_ref[:, j * tf:(j + 1) * tf].astype(BF16)
    wdnb_ref[...] = wdn_ref[...].astype(BF16)
    wob_ref[...] = wo_ref[...].astype(BF16)


def _load_win(winf_hbm, winb_hbm, win_scr, stage, sems):
    step, last = pl.program_id(0), pl.num_programs(0) - 1
    rows = stage.shape[1]
    n_chunks = win_scr.shape[0] // rows
    chunk_in = lambda c: pltpu.make_async_copy(
        winf_hbm.at[pl.ds(c * rows, rows), :], stage.at[c % 2], sems.at[c % 2])
    write_out = pltpu.make_async_copy(win_scr, winb_hbm, sems.at[2])

    @pl.when(step == 0)
    def _():
        chunk_in(0).start()
        for c in range(n_chunks):
            if c + 1 < n_chunks:
                chunk_in(c + 1).start()
            chunk_in(c).wait()
            win_scr[c * rows:(c + 1) * rows, :] = stage[c % 2].astype(BF16)
        write_out.start()

    @pl.when(step == last)
    def _():
        write_out.wait()


def _build_gmlp_bias(sgub_ref, bsb_scr):
    @pl.when(pl.program_id(0) == 0)
    def _():
        n = bsb_scr.shape[1]
        eye = (lax.broadcasted_iota(jnp.int32, (n, n), 0)
               == lax.broadcasted_iota(jnp.int32, (n, n), 1))
        for g in range(bsb_scr.shape[0]):
            col = jnp.sum(jnp.where(eye, sgub_ref[g:g + 1, :], 0.0), axis=1, keepdims=True)
            bsb_scr[g] = jnp.broadcast_to(col, bsb_scr.shape[1:])


def _proj_kernel(chunk, emit_sv, cast_w, x_ref, lng_ref, win_ref, qg_ref, kg_ref, sg_ref,
                 ws_ref, sgub_ref, *rest):
    bs_ref, rest = rest[-1], rest[:-1]
    _build_gmlp_bias(sgub_ref, bs_ref)
    if cast_w:
        winb_hbm, win_scr, stage, sems = rest[-4:]
        _cast_weights(*rest[:3], *rest[-7:-4])
        rest = rest[3:-7]
        _load_win(win_ref, winb_hbm, win_scr, stage, sems)
        win_ref = win_scr
    q_ref, k_ref, v_ref, gated_ref = rest[:4]
    tm = x_ref.shape[0]
    h = _rms_rows(x_ref[...], lng_ref[...]).astype(BF16)
    lane_lo = lax.broadcasted_iota(jnp.int32, (tm, LANES), 1) < HEAD_DIM
    qg2 = jnp.concatenate([qg_ref[...], qg_ref[...]], axis=1)
    kg2 = jnp.concatenate([kg_ref[...], kg_ref[...]], axis=1)

    av = jax.nn.gelu(jnp.dot(h, win_ref[:, _V0:IN_WIDTH], preferred_element_type=F32))
    sv = _rms_rows(av, sg_ref[...])
    if emit_sv:
        rest[4][...] = sv
    svb = sv.astype(BF16)
    u = jax.nn.gelu(jnp.dot(h, win_ref[:, _U0:_V0], preferred_element_type=F32))

    zq = jnp.dot(h, win_ref[:, 0:ATTN_WIDTH], preferred_element_type=F32)
    for c in range(ATTN_WIDTH // LANES):
        sl = slice(c * LANES, (c + 1) * LANES)
        q_ref[:, sl] = (_head_pair_rms(zq[:, sl], qg2, lane_lo)
                        * (HEAD_DIM ** -0.5)).astype(BF16)

    zkv = jnp.dot(h, win_ref[:, _KV0:_U0], preferred_element_type=F32)
    k_ref[...] = _head_pair_rms(zkv[:, 0:KV_WIDTH], kg2, lane_lo)
    v_ref[...] = zkv[:, KV_WIDTH:2 * KV_WIDTH]

    row_c = lax.broadcasted_iota(jnp.int32, (chunk, chunk), 0) // CHUNK
    col_c = lax.broadcasted_iota(jnp.int32, (chunk, chunk), 1) // CHUNK
    n_chunks = tm // chunk
    for g in range(SGU_GROUPS):
        cs = slice(g * SGU_GROUP_CH, (g + 1) * SGU_GROUP_CH)
        w = jnp.where(col_c <= row_c, ws_ref[g, 0:chunk, 0:chunk], 0.0).astype(BF16)
        rhs = jnp.concatenate(
            [svb[n * chunk:(n + 1) * chunk, cs] for n in range(n_chunks)], axis=1)
        sp = jnp.dot(w, rhs, preferred_element_type=F32)
        for n in range(n_chunks):
            rs = slice(n * chunk, (n + 1) * chunk)
            gated_ref[rs, cs] = u[rs, cs] * (sp[:, n * SGU_GROUP_CH:(n + 1) * SGU_GROUP_CH]
                                             + bs_ref[g, 0:chunk, :])


def _proj(x2d, lng, win, qg2, kg2, sg, ws, sgu_b, wup=None, wdn=None, wo=None, *, tm, chunk,
          emit_sv, tf=None):
    t = x2d.shape[0]
    const = lambda shape: pl.BlockSpec(shape, lambda i: (0,) * len(shape),
                                       pipeline_mode=pl.Buffered(1))
    row = lambda w: pl.BlockSpec((tm, w), lambda i: (i, 0))
    out_shape = [jax.ShapeDtypeStruct((t, ATTN_WIDTH), BF16),
                 jax.ShapeDtypeStruct((t, KV_WIDTH), F32),
                 jax.ShapeDtypeStruct((t, KV_WIDTH), F32),
                 jax.ShapeDtypeStruct((t, SGU_WIDTH), F32)]
    out_specs = [row(ATTN_WIDTH), row(KV_WIDTH), row(KV_WIDTH), row(SGU_WIDTH)]
    if emit_sv:
        out_shape.append(jax.ShapeDtypeStruct((t, SGU_WIDTH), F32))
        out_specs.append(row(SGU_WIDTH))
    in_specs = [row(D_MODEL), const((1, D_MODEL)), const((D_MODEL, IN_WIDTH)),
                const((1, HEAD_DIM)), const((1, HEAD_DIM)), const((1, SGU_WIDTH)),
                const(ws.shape), const(sgu_b.shape)]
    args = [x2d, lng, win, qg2, kg2, sg, ws, sgu_b]
    scratch = []
    if wup is not None:
        in_specs[2] = pl.BlockSpec(memory_space=pl.ANY)
        scratch = [pltpu.VMEM(win.shape, BF16),
                   pltpu.VMEM((2, WIN_STAGE_ROWS, win.shape[1]), F32),
                   pltpu.SemaphoreType.DMA((3,))]
        n_steps = t // tm
        up_rows, dn_rows, wo_rows = D_MODEL // n_steps, D_FF // n_steps, wo.shape[0] // n_steps
        in_specs += [pl.BlockSpec((up_rows, D_FF), lambda i: (i, 0)),
                     pl.BlockSpec((dn_rows, D_MODEL), lambda i: (i, 0)),
                     pl.BlockSpec((wo_rows, D_MODEL), lambda i: (i, 0))]
        out_specs += [pl.BlockSpec((D_FF // tf, up_rows, tf), lambda i: (0, i, 0)),
                      pl.BlockSpec((dn_rows, D_MODEL), lambda i: (i, 0)),
                      pl.BlockSpec((wo_rows, D_MODEL), lambda i: (i, 0)),
                      pl.BlockSpec(memory_space=pl.ANY)]
        out_shape += [jax.ShapeDtypeStruct((D_FF // tf, D_MODEL, tf), BF16),
                      jax.ShapeDtypeStruct((D_FF, D_MODEL), BF16),
                      jax.ShapeDtypeStruct(wo.shape, BF16),
                      jax.ShapeDtypeStruct(win.shape, BF16)]
        args += [wup, wdn, wo]
    return pl.pallas_call(
        functools.partial(_proj_kernel, chunk, emit_sv, wup is not None),
        grid=(t // tm,),
        in_specs=in_specs,
        out_specs=out_specs,
        out_shape=out_shape,
        scratch_shapes=scratch + [pltpu.VMEM((SGU_GROUPS, SGU_CHUNK, LANES), F32)],
        compiler_params=pltpu.CompilerParams(dimension_semantics=("arbitrary",),
                                             vmem_limit_bytes=PROJ_VMEM_LIMIT),
        name="proj_c%d" % chunk,
    )(*args)


BAND = 2 * LANES
PAIRS = GQA_GROUP // 2


def _build_tables(bq, nsel, tab_ref, sink_ref, bkt_ref, valid_ref, bias_scr, smax_scr, sl_scr):
    bkt = bkt_ref[0:bq, :]
    lane_lo = lax.broadcasted_iota(jnp.int32, (bq, LANES), 1) < HEAD_DIM
    for h0 in range(0, N_HEADS, 2):
        g, pr = h0 // GQA_GROUP, (h0 % GQA_GROUP) // 2
        rs = slice(pr * bq, (pr + 1) * bq)
        sl_scr[g, rs, :] = jnp.where(lane_lo, sink_ref[h0], sink_ref[h0 + 1])
        for half in range(2):
            h = h0 + half
            smax_scr[g, half, rs, :] = jnp.full((bq, LANES), sink_ref[h], F32)
            row = jnp.broadcast_to(tab_ref[h:h + 1, :], (bq, LANES))
            acc = jnp.concatenate(
                [jnp.take_along_axis(row, bkt[:, c * LANES:(c + 1) * LANES], axis=1)
                 for c in range(BAND // LANES)], axis=1)
            for s in range(nsel):
                ok = valid_ref[valid_ref.shape[0] - nsel + s, 0:bq, :] != 0
                bias_scr[s, g, rs, half * BAND:(half + 1) * BAND] = jnp.where(ok, acc, NEG_INF)


def _t5_bucket(n):
    half = N_BUCKETS // 2
    max_exact = half // 2
    offset = jnp.where(n < 0, half, 0)
    a = jnp.abs(n)
    af = jnp.maximum(a, 1).astype(F32)
    large = max_exact + (jnp.log(af / max_exact) / math.log(MAX_DISTANCE / max_exact)
                         * (half - max_exact)).astype(jnp.int32)
    large = jnp.minimum(large, half - 1)
    return offset + jnp.where(a < max_exact, a, large)


def _attn_kernel(bq, n_units, k_step, nsel, n_kparts, k_layout, *refs):
    q_ref = refs[0]
    k_refs = refs[1:1 + n_kparts]
    v_refs = refs[1 + n_kparts:1 + 2 * n_kparts]
    (tab_ref, sink_ref, bkt_ref, valid_ref, ga_ref, x_ref, gated_ref, gs_ref, wo_ref,
     o_ref, r_ref, kk_scr, vv_scr, ao_scr, bias_ref, smax_ref, sl_ref) = refs[1 + 2 * n_kparts:]
    rows = PAIRS * bq

    @pl.when((pl.program_id(0) == 0) & (pl.program_id(1) == 0))
    def _():
        _build_tables(bq, nsel, tab_ref, sink_ref, bkt_ref, valid_ref, bias_ref, smax_ref, sl_ref)

    def split_heads(parts, scr):
        x = jnp.concatenate(
            [parts[p][r0:r0 + n, :] if p >= 0 else jnp.zeros((n, KV_WIDTH), F32)
             for p, r0, n in k_layout], axis=0)
        lo = lax.broadcasted_iota(jnp.int32, x.shape, 1) < HEAD_DIM
        xr = pltpu.roll(x, HEAD_DIM, 1)
        scr[0, 0] = jnp.where(lo, x, 0.0).astype(BF16)
        scr[0, 1] = jnp.where(lo, 0.0, xr).astype(BF16)
        scr[1, 0] = jnp.where(lo, xr, 0.0).astype(BF16)
        scr[1, 1] = jnp.where(lo, 0.0, x).astype(BF16)

    split_heads(k_refs, kk_scr)
    split_heads(v_refs, vv_scr)
    r_i = lax.broadcasted_iota(jnp.int32, (2 * BAND, LANES), 0)
    l_i = lax.broadcasted_iota(jnp.int32, (2 * BAND, LANES), 1)
    ones_blk = jnp.where((r_i < BAND) == (l_i < HEAD_DIM), 1.0, 0.0).astype(BF16)
    lane_lo = lax.broadcasted_iota(jnp.int32, (rows, LANES), 1) < HEAD_DIM
    first_block = pl.program_id(1) == 0

    for u in range(n_units):
        q0, k0 = u * bq, u * k_step
        sel = jnp.where(first_block, 0, 1) if (nsel > 1 and u == 0) else nsel - 1
        for g in range(N_KV_HEADS):
            cols = [slice((PAIRS * g + p) * LANES, (PAIRS * g + p + 1) * LANES)
                    for p in range(PAIRS)]
            qs = jnp.concatenate([q_ref[q0:q0 + bq, c] for c in cols], axis=0)
            k2 = jnp.concatenate([kk_scr[g, 0, k0:k0 + BAND, :],
                                  kk_scr[g, 1, k0:k0 + BAND, :]], axis=0)
            s = lax.dot_general(qs, k2, (((1,), (1,)), ((), ())),
                                preferred_element_type=F32) + bias_ref[sel, g]
            sa, sb = s[:, :BAND], s[:, BAND:]
            ma = jnp.max(jnp.maximum(jnp.maximum(sa[:, :LANES], sa[:, LANES:]), smax_ref[g, 0]),
                         axis=-1, keepdims=True)
            mb = jnp.max(jnp.maximum(jnp.maximum(sb[:, :LANES], sb[:, LANES:]), smax_ref[g, 1]),
                         axis=-1, keepdims=True)
            p = jnp.concatenate([jnp.exp(sa - ma), jnp.exp(sb - mb)], axis=1).astype(BF16)
            v2 = jnp.concatenate([vv_scr[g, 0, k0:k0 + BAND, :],
                                  vv_scr[g, 1, k0:k0 + BAND, :]], axis=0)
            ol = jnp.dot(p, jnp.concatenate([v2, ones_blk], axis=1), preferred_element_type=F32)
            denom = ol[:, LANES:] + jnp.exp(sl_ref[g] - jnp.where(lane_lo, ma, mb))
            out = ol[:, :LANES] / denom
            for p_i, c in enumerate(cols):
                ao_scr[q0:q0 + bq, c] = out[p_i * bq:(p_i + 1) * bq]

    mix = jnp.concatenate([_rms_rows(ao_scr[...], ga_ref[...]).astype(BF16),
                           _rms_rows(gated_ref[...], gs_ref[...]).astype(BF16)], axis=1)
    x1 = x_ref[...] + jnp.dot(mix, wo_ref[...], preferred_element_type=F32)
    o_ref[...] = x1
    rinv = lax.rsqrt(jnp.mean(x1 * x1, axis=-1, keepdims=True) + EPS)
    r_ref[...] = jnp.broadcast_to(rinv * rinv, r_ref.shape)


def _attn(q, k_parts, v_parts, k_maps, k_layout, table, sinks, bucket, valid, ga, x, gated, gs,
          wo, *, bq, n_units, k_step, nsel):
    nb, s, _ = q.shape
    tq = n_units * bq
    k_rows = [rows for _, rows in k_parts]
    kr = sum(n for _, _, n in k_layout)
    cur = lambda b, i: (b, i, 0)
    row = lambda w: pl.BlockSpec((None, tq, w), cur)
    full = lambda a: pl.BlockSpec(a.shape, lambda b, i: (0,) * a.ndim,
                                  pipeline_mode=pl.Buffered(1))
    kv_specs = [pl.BlockSpec((None, r, KV_WIDTH), m) for r, m in zip(k_rows, k_maps)]
    return pl.pallas_call(
        functools.partial(_attn_kernel, bq, n_units, k_step, nsel, len(k_parts), tuple(k_layout)),
        grid=(nb, s // tq),
        in_specs=([row(ATTN_WIDTH)] + kv_specs + kv_specs
                  + [full(table), pl.BlockSpec(memory_space=pltpu.SMEM), full(bucket), full(valid),
                     full(ga), row(D_MODEL), row(SGU_WIDTH), full(gs), full(wo)]),
        out_specs=[row(D_MODEL), row(LANES)],
        out_shape=[jax.ShapeDtypeStruct((nb, s, D_MODEL), F32),
                   jax.ShapeDtypeStruct((nb, s, LANES), F32)],
        scratch_shapes=[pltpu.VMEM((N_KV_HEADS, 2, kr, KV_WIDTH), BF16),
                        pltpu.VMEM((N_KV_HEADS, 2, kr, KV_WIDTH), BF16),
                        pltpu.VMEM((tq, ATTN_WIDTH), F32),
                        pltpu.VMEM((nsel, N_KV_HEADS, PAIRS * bq, 2 * BAND), F32),
                        pltpu.VMEM((N_KV_HEADS, 2, PAIRS * bq, LANES), F32),
                        pltpu.VMEM((N_KV_HEADS, PAIRS * bq, LANES), F32)],
        compiler_params=pltpu.CompilerParams(dimension_semantics=("arbitrary", "arbitrary"),
                                             vmem_limit_bytes=VMEM_LIMIT),
        name="attn_q%d" % bq,
    )(q, *[a for a, _ in k_parts], *[a for a, _ in v_parts], table, sinks, bucket, valid, ga, x,
      gated, gs, wo)


def _ffn_kernel(x_ref, r_ref, g_ref, wup_ref, wdn_ref, y_ref):
    tf = wup_ref.shape[1]
    xg = (x_ref[...] * g_ref[...]).astype(BF16)
    r2 = jnp.tile(r_ref[...], (1, FFN_SUB // LANES))
    for s in range(tf // FFN_SUB):
        cs = slice(s * FFN_SUB, (s + 1) * FFN_SUB)
        z = jnp.dot(xg, wup_ref[:, cs], preferred_element_type=F32)
        a = (jnp.square(jnp.maximum(z, 0.0)) * r2).astype(BF16)
        if s == 0:
            base = jnp.where(pl.program_id(1) == 0, x_ref[...], y_ref[...])
        else:
            base = y_ref[...]
        y_ref[...] = base + jnp.dot(a, wdn_ref[cs, :], preferred_element_type=F32)


def _ffn(x2d, r2d, g, wup_t, wdn, *, tm):
    t = x2d.shape[0]
    n_f, _, tf = wup_t.shape
    return pl.pallas_call(
        _ffn_kernel,
        grid=(t // tm, n_f),
        in_specs=[pl.BlockSpec((tm, D_MODEL), lambda i, j: (i, 0)),
                  pl.BlockSpec((tm, LANES), lambda i, j: (i, 0)),
                  pl.BlockSpec((1, D_MODEL), lambda i, j: (0, 0)),
                  pl.BlockSpec((None, D_MODEL, tf), lambda i, j: (j, 0, 0)),
                  pl.BlockSpec((tf, D_MODEL), lambda i, j: (j, 0))],
        out_specs=pl.BlockSpec((tm, D_MODEL), lambda i, j: (i, 0)),
        out_shape=jax.ShapeDtypeStruct((t, D_MODEL), F32),
        input_output_aliases={0: 0},
        compiler_params=pltpu.CompilerParams(dimension_semantics=("parallel", "arbitrary"),
                                             vmem_limit_bytes=FFN_VMEM_LIMIT),
        name="ffn",
    )(x2d, r2d, g, wup_t, wdn)


def _band_tables(bq, nsel):
    qi = jnp.arange(bq)[:, None]
    kj = jnp.arange(BAND)[None, :] - WINDOW
    qc, kc = qi // CHUNK, jnp.floor_divide(kj, CHUNK)
    in_band = (kc <= qc) & (kc >= qc - WINDOW // CHUNK)
    valid = [in_band & (kj >= 0)] if nsel == 2 else []
    valid.append(in_band)
    return _t5_bucket(qi - kj), jnp.stack(valid).astype(jnp.int32)


def kernel(x_prompt, x_sample, cache_attn_k, cache_attn_v, rel_bias_table, ln_mix_g, w_in,
           q_norm_g, k_norm_g, attn_sinks, sgu_norm_g, sgu_w, sgu_b, out_norm_attn_g,
           out_norm_sgu_g, w_out, ln_ffn_g, w_ffn_up, w_ffn_down):
    bp, sp, _ = x_prompt.shape
    bs, ss, _ = x_sample.shape
    depth = w_in.shape[0]
    assert depth == 1 and ss == CHUNK and cache_attn_k.shape[2] == WINDOW
    l = 0
    xp = x_prompt.reshape(bp * sp, D_MODEL)
    xs = x_sample.reshape(bs * ss, D_MODEL)

    bq, units = 2 * CHUNK, ATTN_UNITS
    sinks = attn_sinks[l].reshape(N_HEADS)
    table_t = jnp.pad(rel_bias_table.T, ((0, 0), (0, LANES - N_BUCKETS)))
    bucket, valid = _band_tables(bq, 2)
    lng = ln_mix_g[l].reshape(1, D_MODEL)
    qg2 = q_norm_g[l].reshape(1, HEAD_DIM)
    kg2 = k_norm_g[l].reshape(1, HEAD_DIM)
    sg = sgu_norm_g[l].reshape(1, SGU_WIDTH)
    gs = out_norm_sgu_g[l].reshape(1, SGU_WIDTH)
    ga = out_norm_attn_g[l].reshape(1, ATTN_WIDTH)
    lnf = ln_ffn_g[l].reshape(1, D_MODEL)

    ws = sgu_w[l]

    q, k, v, gated, wup, wdn, wo, win = _proj(xp, lng, w_in[l], qg2, kg2, sg, ws, sgu_b[l],
                                         w_ffn_up[l], w_ffn_down[l], w_out[l],
                                         tm=PROJ_TM, chunk=SGU_CHUNK, emit_sv=False, tf=FFN_TF)
    k3 = k.reshape(bp, sp, KV_WIDTH)
    v3 = v.reshape(bp, sp, KV_WIDTH)
    prev = lambda b, i: (b, jnp.maximum(i * units - 1, 0), 0)
    cur = lambda b, i: (b, i, 0)
    x1p, r2p = _attn(q.reshape(bp, sp, ATTN_WIDTH),
                     [(k3, WINDOW), (k3, units * bq)], [(v3, WINDOW), (v3, units * bq)],
                     [prev, cur], [(0, 0, WINDOW), (1, 0, units * bq)],
                     table_t, sinks, bucket, valid, ga,
                     x_prompt, gated.reshape(bp, sp, SGU_WIDTH), gs, wo,
                     bq=bq, n_units=units, k_step=bq, nsel=2)
    yp = _ffn(x1p.reshape(bp * sp, D_MODEL), r2p.reshape(bp * sp, LANES), lnf, wup, wdn, tm=FFN_TM)

    qs, ks, vs, gated_s, svs = _proj(xs, lng, win, qg2, kg2, sg, ws, sgu_b[l],
                                    tm=PROJ_TM, chunk=ss, emit_sv=True)
    seqs = SAMPLE_SEQS
    grp = lambda a, rows: a.reshape(bs // seqs, seqs * rows, a.shape[-1])
    layout = []
    for i in range(seqs):
        layout += [(0, i * WINDOW, WINDOW), (1, i * ss, ss), (-1, 0, BAND - WINDOW - ss)]
    x1s, r2s = _attn(grp(qs, ss),
                     [(grp(cache_attn_k[l].reshape(bs, WINDOW, KV_WIDTH), WINDOW), seqs * WINDOW),
                      (grp(ks, ss), seqs * ss)],
                     [(grp(cache_attn_v[l].reshape(bs, WINDOW, KV_WIDTH), WINDOW), seqs * WINDOW),
                      (grp(vs, ss), seqs * ss)],
                     [cur, cur], layout, table_t, sinks, bucket, valid,
                     ga, grp(xs, ss), grp(gated_s, ss), gs, wo,
                     bq=ss, n_units=seqs, k_step=BAND, nsel=1)
    ys = _ffn(x1s.reshape(bs * ss, D_MODEL), r2s.reshape(bs * ss, LANES), lnf, wup, wdn, tm=FFN_TM)

    keep = min(WINDOW, sp)
    kv_shape = (N_KV_HEADS, HEAD_DIM)
    return (yp.reshape(bp, sp, D_MODEL),
            ys.reshape(bs, ss, D_MODEL),
            k3[:, -keep:].reshape(1, bp, keep, *kv_shape),
            v3[:, -keep:].reshape(1, bp, keep, *kv_shape),
            ks.reshape(1, bs, ss, *kv_shape),
            vs.reshape(1, bs, ss, *kv_shape),
            svs.reshape(1, bs, ss, SGU_WIDTH))
```
